```python
import jax, jax.numpy as jnp
from jax import lax
import numpy as np

D_MODEL = 1024
BATCH = 2
SEQ = 8192
DEPTH = 1

CHUNK = 64
LEFT_CHUNKS = 8
BAND_CHUNKS = LEFT_CHUNKS + 1
POOL_WIDTH = D_MODEL // 2
POOL_WINDOWS = (2, 4, 8, 16)
N_POOL_GROUPS = len(POOL_WINDOWS)
POOL_GROUP = POOL_WIDTH // N_POOL_GROUPS
N_HEADS = 8
HEAD_DIM = 64
ATTN_WIDTH = N_HEADS * HEAD_DIM
MAX_REL = 64
D_FF = 2816
N_BRANCHES = 2
IN_WIDTH = POOL_WIDTH + 3 * ATTN_WIDTH
EPS = 1e-6

kernel_name = "hybrid_pool_chunkattn_macaron"


def rmsnorm(x, g):
    xf = x.astype(jnp.float32)
    y = xf * lax.rsqrt(jnp.mean(xf * xf, axis=-1, keepdims=True) + EPS)
    return (y * g.astype(jnp.float32)).astype(x.dtype)


def swiglu(x, w_gate, w_up, w_down):
    return (jax.nn.silu(x @ w_gate) * (x @ w_up)) @ w_down


def pool_mixer(u, pool_w, pool_scale):
    b, s, _ = u.shape
    uf = u.astype(jnp.float32).reshape(b, s, N_POOL_GROUPS, POOL_GROUP)
    cz = jnp.concatenate([jnp.zeros((b, 1, N_POOL_GROUPS, POOL_GROUP), jnp.float32),
                          jnp.cumsum(uf, axis=1)], axis=1)
    t = jnp.arange(s)
    outs = []
    for g, w in enumerate(POOL_WINDOWS):
        lo = jnp.maximum(t + 1 - w, 0)
        sums = cz[:, 1:, g] - cz[:, lo, g]
        counts = jnp.minimum(t + 1, w).astype(jnp.float32)[None, :, None]
        outs.append(sums / counts - uf[:, :, g])
    mixed = jnp.stack(outs, axis=2).astype(u.dtype)
    y = jnp.einsum('bsgc,gcd->bsgd', mixed, pool_w).reshape(b, s, POOL_WIDTH)
    return y * pool_scale


def chunk_attention(q, k, v, rel_bias):
    b, s, h, dh = q.shape
    nc = s // CHUNK
    band = BAND_CHUNKS * CHUNK
    qc = q.reshape(b, nc, CHUNK, h, dh)
    pad = ((0, 0), (LEFT_CHUNKS * CHUNK, 0), (0, 0), (0, 0))
    kp = jnp.pad(k, pad).reshape(b, nc + LEFT_CHUNKS, CHUNK, h, dh)
    vp = jnp.pad(v, pad).reshape(b, nc + LEFT_CHUNKS, CHUNK, h, dh)
    idx = jnp.arange(nc)[:, None] + jnp.arange(BAND_CHUNKS)[None, :]
    kb = kp[:, idx].reshape(b, nc, band, h, dh)
    vb = vp[:, idx].reshape(b, nc, band, h, dh)
    scores = jnp.einsum('bnqhd,bnkhd->bhnqk', qc, kb).astype(jnp.float32) * (HEAD_DIM ** -0.5)
    rel = jnp.arange(CHUNK)[:, None] + LEFT_CHUNKS * CHUNK - jnp.arange(band)[None, :]
    rel_idx = jnp.clip(rel, -MAX_REL, MAX_REL) + MAX_REL
    bias = rel_bias.astype(jnp.float32)[:, rel_idx]
    scores = scores + bias[None, :, None]
    key_chunk = jnp.arange(nc)[:, None] - LEFT_CHUNKS + jnp.arange(BAND_CHUNKS)[None, :]
    valid = jnp.repeat(key_chunk >= 0, CHUNK, axis=1)
    scores = jnp.where(valid[None, None, :, None, :], scores, jnp.finfo(jnp.float32).min)
    p = jax.nn.softmax(scores, axis=-1).astype(v.dtype)
    out = jnp.einsum('bhnqk,bnkhd->bnqhd', p, vb)
    return out.reshape(b, s, h * dh)


def _normal(key, shape, scale):
    return jax.random.normal(key, shape, jnp.float32) * scale


def setup_inputs(seed: int = 0) -> dict:
    key = jax.random.key(seed)
    ks = jax.random.split(key, 24)
    L, D = DEPTH, D_MODEL
    gain = lambda k, n: 1.0 + 0.05 * jax.random.normal(k, (L, n), jnp.float32)
    return {
        "x": _normal(ks[0], (BATCH, SEQ, D), 1.0),
        "ffn1_norm": gain(ks[1], D),
        "ffn1_w_gate": _normal(ks[2], (L, D, D_FF), D ** -0.5),
        "ffn1_w_up": _normal(ks[3], (L, D, D_FF), D ** -0.5),
        "ffn1_w_down": _normal(ks[4], (L, D_FF, D), D_FF ** -0.5),
        "mix_norm": gain(ks[5], D),
        "w_in": _normal(ks[6], (L, D, IN_WIDTH), D ** -0.5),
        "pool_w": _normal(ks[7], (L, N_POOL_GROUPS, POOL_GROUP, POOL_GROUP), POOL_GROUP ** -0.5),
        "pool_scale": 1.0 + 0.1 * jax.random.normal(ks[8], (L, POOL_WIDTH), jnp.float32),
        "rel_bias": _normal(ks[9], (L, N_HEADS, 2 * MAX_REL + 1), 0.1),
        "w_branch_pool": _normal(ks[10], (L, POOL_WIDTH, D), POOL_WIDTH ** -0.5),
        "w_branch_attn": _normal(ks[11], (L, ATTN_WIDTH, D), ATTN_WIDTH ** -0.5),
        "w_gate": _normal(ks[12], (L, D, N_BRANCHES * D), D ** -0.5),
        "b_gate": _normal(ks[13], (L, N_BRANCHES * D), 0.02),
        "w_out": _normal(ks[14], (L, D, D), D ** -0.5),
        "ffn2_norm": gain(ks[15], D),
        "ffn2_w_gate": _normal(ks[16], (L, D, D_FF), D ** -0.5),
        "ffn2_w_up": _normal(ks[17], (L, D, D_FF), D ** -0.5),
        "ffn2_w_down": _normal(ks[18], (L, D_FF, D), D_FF ** -0.5),
        "final_norm": 1.0 + 0.05 * jax.random.normal(ks[19], (D,), jnp.float32),
    }


def reference(x, ffn1_norm, ffn1_w_gate, ffn1_w_up, ffn1_w_down, mix_norm, w_in,
              pool_w, pool_scale, rel_bias, w_branch_pool, w_branch_attn, w_gate, b_gate,
              w_out, ffn2_norm, ffn2_w_gate, ffn2_w_up, ffn2_w_down, final_norm):
    b, s, d = x.shape
    for l in range(DEPTH):
        x = x + 0.5 * swiglu(rmsnorm(x, ffn1_norm[l]), ffn1_w_gate[l], ffn1_w_up[l], ffn1_w_down[l])
        h = rmsnorm(x, mix_norm[l])
        u = h @ w_in[l]
        u_pool = u[..., :POOL_WIDTH]
        q = u[..., POOL_WIDTH:POOL_WIDTH + ATTN_WIDTH].reshape(b, s, N_HEADS, HEAD_DIM)
        k = u[..., POOL_WIDTH + ATTN_WIDTH:POOL_WIDTH + 2 * ATTN_WIDTH].reshape(b, s, N_HEADS, HEAD_DIM)
        v = u[..., POOL_WIDTH + 2 * ATTN_WIDTH:].reshape(b, s, N_HEADS, HEAD_DIM)
        y_pool = pool_mixer(u_pool, pool_w[l], pool_scale[l]) @ w_branch_pool[l]
        y_attn = chunk_attention(q, k, v, rel_bias[l]) @ w_branch_attn[l]
        gates = jax.nn.sigmoid(h @ w_gate[l] + b_gate[l]).reshape(b, s, N_BRANCHES, d)
        merged = gates[:, :, 0] * y_pool + gates[:, :, 1] * y_attn
        x = x + merged @ w_out[l]
        x = x + 0.5 * swiglu(rmsnorm(x, ffn2_norm[l]), ffn2_w_gate[l], ffn2_w_up[l], ffn2_w_down[l])
    return rmsnorm(x, final_norm)
```

```python
import functools

import jax
import jax.numpy as jnp
from jax import lax
from jax.experimental import pallas as pl
from jax.experimental.pallas import tpu as pltpu

D_MODEL = 1024
D_FF = 2816
CHUNK = 64
LEFT_CHUNKS = 8
POOL_WIDTH = 512
POOL_WINDOWS = (2, 4, 8, 16)
POOL_GROUP = 128
N_HEADS = 8
HEAD_DIM = 64
ATTN_WIDTH = 512
MAX_REL = 64
IN_WIDTH = POOL_WIDTH + 3 * ATTN_WIDTH
EPS = 1e-6

LANES = 128
TM = 512
FF_CHUNK = 256
Q_BLOCK = 256
KEY_BLOCK = Q_BLOCK + LEFT_CHUNKS * CHUNK
HALO = 16
MASK_VALUE = -1e30
VMEM_LIMIT = 56 * 1024 * 1024

F32 = jnp.float32
BF16 = jnp.bfloat16


def _rmsnorm(x, g):
    ms = jnp.mean(x * x, axis=-1, keepdims=True)
    return x * lax.rsqrt(ms + EPS) * g


def _sigmoid(z):
    return 0.5 + 0.5 * jnp.tanh(0.5 * z)


def _swiglu(hb_ref, wgu_ref, wd_ref, act_ref):
    for c in range(D_FF // FF_CHUNK):
        gu = jnp.dot(hb_ref[...], wgu_ref[:, 2 * c * FF_CHUNK:2 * (c + 1) * FF_CHUNK],
                     preferred_element_type=F32)
        hg = 0.5 * gu[:, :FF_CHUNK]
        up = gu[:, FF_CHUNK:]
        act_ref[:, c * FF_CHUNK:(c + 1) * FF_CHUNK] = ((hg + hg * jnp.tanh(hg)) * up).astype(BF16)
    return jnp.dot(act_ref[...], wd_ref[...], preferred_element_type=F32)


def _ffn1_inproj_kernel(x_ref, n1_ref, wgu_ref, wd_ref, nm_ref, win_ref,
                        x1_ref, up_ref, qkv_ref, hb_ref, act_ref):
    x = x_ref[...]
    hb_ref[...] = _rmsnorm(x, n1_ref[...]).astype(BF16)
    x1 = x + 0.5 * _swiglu(hb_ref, wgu_ref, wd_ref, act_ref)
    x1_ref[...] = x1
    hb_ref[...] = _rmsnorm(x1, nm_ref[...]).astype(BF16)
    up_ref[...] = jnp.dot(hb_ref[...], win_ref[:, :POOL_WIDTH], preferred_element_type=F32)
    for c in range(3):
        lo = POOL_WIDTH + c * ATTN_WIDTH
        qkv_ref[:, c * ATTN_WIDTH:(c + 1) * ATTN_WIDTH] = jnp.dot(
            hb_ref[...], win_ref[:, lo:lo + ATTN_WIDTH], preferred_element_type=F32).astype(BF16)


def _attn_kernel(tiles_per_seq, q_ref, kp_ref, kc_ref, vp_ref, vc_ref, bias_ref,
                 o_ref, kcat_ref, vcat_ref):
    i = pl.program_id(0)
    is_first = (i % tiles_per_seq) == 0
    kcat_ref[0:TM, :] = kp_ref[...]
    kcat_ref[TM:2 * TM, :] = kc_ref[...]
    vcat_ref[0:TM, :] = vp_ref[...]
    vcat_ref[TM:2 * TM, :] = vc_ref[...]
    col = lax.broadcasted_iota(jnp.int32, (1, KEY_BLOCK), 1)
    lane = lax.broadcasted_iota(jnp.int32, (Q_BLOCK, LANES), 1)
    low_half = lane < HEAD_DIM
    for sb in range(TM // Q_BLOCK):
        r0 = sb * Q_BLOCK
        n_invalid = jnp.where(is_first, TM - r0, 0)
        cmask = jnp.where(col < n_invalid, MASK_VALUE, 0.0).astype(F32)
        for p in range(N_HEADS // 2):
            cs = slice(p * LANES, (p + 1) * LANES)
            qp = q_ref[r0:r0 + Q_BLOCK, cs]
            kp = kcat_ref[r0:r0 + KEY_BLOCK, cs]
            vp = vcat_ref[r0:r0 + KEY_BLOCK, cs]
            outs = []
            for par in range(2):
                keep = low_half if par == 0 else jnp.logical_not(low_half)
                qm = jnp.where(keep, qp, jnp.zeros_like(qp))
                s = lax.dot_general(qm, kp, (((1,), (1,)), ((), ())),
                                    preferred_element_type=F32)
                s = s + bias_ref[2 * p + par] + cmask
                m = jnp.max(s, axis=-1, keepdims=True)
                e = jnp.exp(s - m)
                l = jnp.sum(e, axis=-1, keepdims=True)
                o = jnp.dot(e.astype(BF16), vp, preferred_element_type=F32)
                outs.append(o * (1.0 / l))
            o_ref[r0:r0 + Q_BLOCK, cs] = jnp.where(low_half, outs[0], outs[1]).astype(BF16)


def _mix_ffn2_kernel(tiles_per_seq, x1_ref, up_ref, halo_ref, attn_ref, nm_ref, wg_ref,
                     bg_ref, poolw_ref, pscale_ref, wbp_ref, wba_ref, wout_ref,
                     n2_ref, wgu_ref, wd_ref, nf_ref,
                     out_ref, hb_ref, act_ref, ext_ref, pm_ref, mg_ref):
    i = pl.program_id(0)
    tile_in_seq = i % tiles_per_seq
    x1 = x1_ref[...]
    hb_ref[...] = _rmsnorm(x1, nm_ref[...]).astype(BF16)

    keep_halo = jnp.where(tile_in_seq == 0, 0.0, 1.0).astype(F32)
    ext_ref[0:HALO, :] = halo_ref[...] * keep_halo
    ext_ref[HALO:HALO + TM, :] = up_ref[...]
    pos = tile_in_seq * TM + lax.broadcasted_iota(jnp.int32, (TM, 1), 0)
    for g, w in enumerate(POOL_WINDOWS):
        cs = slice(g * POOL_GROUP, (g + 1) * POOL_GROUP)
        u_g = ext_ref[HALO:HALO + TM, cs]
        sums = u_g
        for j in range(1, w):
            sums = sums + ext_ref[HALO - j:HALO - j + TM, cs]
        inv_cnt = 1.0 / jnp.minimum(pos + 1, w).astype(F32)
        mixed = (sums * inv_cnt - u_g).astype(BF16)
        y = jnp.dot(mixed, poolw_ref[g], preferred_element_type=F32)
        pm_ref[:, cs] = (y * pscale_ref[:, cs]).astype(BF16)

    half = D_MODEL // 2
    for c in range(2):
        cs = slice(c * half, (c + 1) * half)
        cs1 = slice(D_MODEL + c * half, D_MODEL + (c + 1) * half)
        y_pool = jnp.dot(pm_ref[...], wbp_ref[:, cs], preferred_element_type=F32)
        g0 = _sigmoid(jnp.dot(hb_ref[...], wg_ref[:, cs], preferred_element_type=F32)
                      + bg_ref[:, cs])
        merged = g0 * y_pool
        y_attn = jnp.dot(attn_ref[...], wba_ref[:, cs], preferred_element_type=F32)
        g1 = _sigmoid(jnp.dot(hb_ref[...], wg_ref[:, cs1], preferred_element_type=F32)
                      + bg_ref[:, cs1])
        mg_ref[:, cs] = (merged + g1 * y_attn).astype(BF16)
    x2 = x1 + jnp.dot(mg_ref[...], wout_ref[...], preferred_element_type=F32)

    hb_ref[...] = _rmsnorm(x2, n2_ref[...]).astype(BF16)
    x3 = x2 + 0.5 * _swiglu(hb_ref, wgu_ref, wd_ref, act_ref)
    out_ref[...] = _rmsnorm(x3, nf_ref[...])


def _resident(shape):
    return pl.BlockSpec(shape, lambda i: (0,) * len(shape), pipeline_mode=pl.Buffered(1))


def _interleave_gate_up(w_gate, w_up):
    n = D_FF // FF_CHUNK
    wg = w_gate.reshape(D_MODEL, n, FF_CHUNK)
    wu = w_up.reshape(D_MODEL, n, FF_CHUNK)
    return jnp.concatenate([wg, wu], axis=2).reshape(D_MODEL, 2 * D_FF).astype(BF16)


def _bias_table(rel_bias):
    r = jnp.arange(Q_BLOCK)[:, None]
    c = jnp.arange(KEY_BLOCK)[None, :]
    dchunk = c // CHUNK - r // CHUNK
    in_band = (dchunk >= 0) & (dchunk <= LEFT_CHUNKS)
    rel = r + LEFT_CHUNKS * CHUNK - c
    idx = jnp.clip(rel, -MAX_REL, MAX_REL) + MAX_REL
    table = rel_bias.astype(F32)[:, idx]
    return jnp.where(in_band[None], table, MASK_VALUE)


def kernel(x, ffn1_norm, ffn1_w_gate, ffn1_w_up, ffn1_w_down, mix_norm, w_in, pool_w, pool_scale,
           rel_bias, w_branch_pool, w_branch_attn, w_gate, b_gate, w_out, ffn2_norm, ffn2_w_gate,
           ffn2_w_up, ffn2_w_down, final_norm):
    b, s, d = x.shape
    assert d == D_MODEL and s % TM == 0 and ffn1_norm.shape[0] == 1
    n_tok = b * s
    n_tiles = n_tok // TM
    tiles_per_seq = s // TM
    xf = x.reshape(n_tok, d)

    row = lambda v: v.reshape(1, -1).astype(F32)
    wgu1 = _interleave_gate_up(ffn1_w_gate[0], ffn1_w_up[0])
    wd1 = ffn1_w_down[0].astype(BF16)
    wgu2 = _interleave_gate_up(ffn2_w_gate[0], ffn2_w_up[0])
    wd2 = ffn2_w_down[0].astype(BF16)
    col_scale = jnp.ones((IN_WIDTH,), F32).at[POOL_WIDTH:POOL_WIDTH + ATTN_WIDTH].set(HEAD_DIM ** -0.5)
    win = (w_in[0] * col_scale[None, :]).astype(BF16)

    cparams = pltpu.CompilerParams(dimension_semantics=("arbitrary",),
                                   vmem_limit_bytes=VMEM_LIMIT)
    tile = lambda width: pl.BlockSpec((TM, width), lambda i: (i, 0))

    x1, up, qkv = pl.pallas_call(
        _ffn1_inproj_kernel,
        grid=(n_tiles,),
        in_specs=[tile(D_MODEL), _resident((1, D_MODEL)), _resident((D_MODEL, 2 * D_FF)),
                  _resident((D_FF, D_MODEL)), _resident((1, D_MODEL)),
                  _resident((D_MODEL, IN_WIDTH))],
        out_specs=[tile(D_MODEL), tile(POOL_WIDTH), tile(3 * ATTN_WIDTH)],
        out_shape=[jax.ShapeDtypeStruct((n_tok, D_MODEL), F32),
                   jax.ShapeDtypeStruct((n_tok, POOL_WIDTH), F32),
                   jax.ShapeDtypeStruct((n_tok, 3 * ATTN_WIDTH), BF16)],
        scratch_shapes=[pltpu.VMEM((TM, D_MODEL), BF16), pltpu.VMEM((TM, D_FF), BF16)],
        compiler_params=cparams,
        name="ffn1_inproj",
    )(xf, row(ffn1_norm), wgu1, wd1, row(mix_norm), win)

    prev = lambda i: jnp.maximum(i - 1, 0)
    attn = pl.pallas_call(
        functools.partial(_attn_kernel, tiles_per_seq),
        grid=(n_tiles,),
        in_specs=[pl.BlockSpec((TM, ATTN_WIDTH), lambda i: (i, 0)),
                  pl.BlockSpec((TM, ATTN_WIDTH), lambda i: (prev(i), 1)),
                  pl.BlockSpec((TM, ATTN_WIDTH), lambda i: (i, 1)),
                  pl.BlockSpec((TM, ATTN_WIDTH), lambda i: (prev(i), 2)),
                  pl.BlockSpec((TM, ATTN_WIDTH), lambda i: (i, 2)),
                  _resident((N_HEADS, Q_BLOCK, KEY_BLOCK))],
        out_specs=tile(ATTN_WIDTH),
        out_shape=jax.ShapeDtypeStruct((n_tok, ATTN_WIDTH), BF16),
        scratch_shapes=[pltpu.VMEM((2 * TM, ATTN_WIDTH), BF16),
                        pltpu.VMEM((2 * TM, ATTN_WIDTH), BF16)],
        compiler_params=cparams,
        name="chunk_attn",
    )(qkv, qkv, qkv, qkv, qkv, _bias_table(rel_bias[0]))

    halo_blocks = TM // HALO
    out = pl.pallas_call(
        functools.partial(_mix_ffn2_kernel, tiles_per_seq),
        grid=(n_tiles,),
        in_specs=[tile(D_MODEL), tile(POOL_WIDTH),
                  pl.BlockSpec((HALO, POOL_WIDTH),
                               lambda i: (jnp.maximum(i * halo_blocks - 1, 0), 0)),
                  tile(ATTN_WIDTH), _resident((1, D_MODEL)),
                  _resident((D_MODEL, 2 * D_MODEL)), _resident((1, 2 * D_MODEL)),
                  _resident((len(POOL_WINDOWS), POOL_GROUP, POOL_GROUP)),
                  _resident((1, POOL_WIDTH)), _resident((POOL_WIDTH, D_MODEL)),
                  _resident((ATTN_WIDTH, D_MODEL)), _resident((D_MODEL, D_MODEL)),
                  _resident((1, D_MODEL)), _resident((D_MODEL, 2 * D_FF)),
                  _resident((D_FF, D_MODEL)), _resident((1, D_MODEL))],
        out_specs=tile(D_MODEL),
        out_shape=jax.ShapeDtypeStruct((n_tok, D_MODEL), F32),
        scratch_shapes=[pltpu.VMEM((TM, D_MODEL), BF16), pltpu.VMEM((TM, D_FF), BF16),
                        pltpu.VMEM((HALO + TM, POOL_WIDTH), F32),
                        pltpu.VMEM((TM, POOL_WIDTH), BF16), pltpu.VMEM((TM, D_MODEL), BF16)],
        compiler_params=cparams,
        name="mix_ffn2",
    )(x1, up, up, attn, row(mix_norm), w_gate[0].astype(BF16), row(b_gate),
      pool_w[0].astype(BF16), row(pool_scale), w_branch_pool[0].astype(BF16),
      w_branch_attn[0].astype(BF16), w_out[0].astype(BF16), row(ffn2_norm), wgu2, wd2,
      row(final_norm))
    return out.reshape(b, s, d)
```

```python
import functools

import jax
import jax.numpy as jnp
from jax import lax
from jax.experimental import pallas as pl
from jax.experimental.pallas import tpu as pltpu

D_MODEL = 1024
D_FF = 2816
CHUNK = 64
LEFT_CHUNKS = 8
POOL_WIDTH = 512
POOL_WINDOWS = (2, 4, 8, 16)
POOL_GROUP = 128
N_HEADS = 8
HEAD_DIM = 64
ATTN_WIDTH = 512
MAX_REL = 64
IN_WIDTH = POOL_WIDTH + 3 * ATTN_WIDTH
EPS = 1e-6

LANES = 128
TM = 512
FF_CHUNK = 512
Q_BLOCK = 256
KEY_BLOCK = Q_BLOCK + LEFT_CHUNKS * CHUNK
HALO = 16
MASK_VALUE = -1e30
VMEM_LIMIT = 56 * 1024 * 1024

F32 = jnp.float32
BF16 = jnp.bfloat16


def _rmsnorm(x, g):
    ms = jnp.mean(x * x, axis=-1, keepdims=True)
    return x * lax.rsqrt(ms + EPS) * g


def _sigmoid(z):
    return 0.5 + 0.5 * jnp.tanh(0.5 * z)


def _swiglu(hb_ref, wg_ref, wu_ref, wd_ref, act_ref):
    for lo in range(0, D_FF, FF_CHUNK):
        cs = slice(lo, min(lo + FF_CHUNK, D_FF))
        hg = 0.5 * jnp.dot(hb_ref[...], wg_ref[:, cs], preferred_element_type=F32)
        up = jnp.dot(hb_ref[...], wu_ref[:, cs], preferred_element_type=F32)
        act_ref[:, cs] = ((hg + hg * jnp.tanh(hg)) * up).astype(BF16)
    return jnp.dot(act_ref[...], wd_ref[...], preferred_element_type=F32)


def _ffn1_inproj_kernel(x_ref, n1_ref, wg_ref, wu_ref, wd_ref, nm_ref, win_ref,
                        x1_ref, up_ref, qkv_ref, hb_ref, act_ref):
    x = x_ref[...]
    hb_ref[...] = _rmsnorm(x, n1_ref[...]).astype(BF16)
    x1 = x + 0.5 * _swiglu(hb_ref, wg_ref, wu_ref, wd_ref, act_ref)
    x1_ref[...] = x1
    hb_ref[...] = _rmsnorm(x1, nm_ref[...]).astype(BF16)
    up_ref[...] = jnp.dot(hb_ref[...], win_ref[:, :POOL_WIDTH], preferred_element_type=F32)
    for c in range(3):
        lo = POOL_WIDTH + c * ATTN_WIDTH
        qkv_ref[:, c * ATTN_WIDTH:(c + 1) * ATTN_WIDTH] = jnp.dot(
            hb_ref[...], win_ref[:, lo:lo + ATTN_WIDTH], preferred_element_type=F32).astype(BF16)


def _attn_kernel(tiles_per_seq, q_ref, kp_ref, kc_ref, vp_ref, vc_ref, bias_ref,
                 o_ref, kcat_ref, vcat_ref):
    i = pl.program_id(0)
    is_first = (i % tiles_per_seq) == 0
    kcat_ref[0:TM, :] = kp_ref[...]
    kcat_ref[TM:2 * TM, :] = kc_ref[...]
    vcat_ref[0:TM, :] = vp_ref[...]
    vcat_ref[TM:2 * TM, :] = vc_ref[...]
    col = lax.broadcasted_iota(jnp.int32, (1, KEY_BLOCK), 1)
    lane = lax.broadcasted_iota(jnp.int32, (Q_BLOCK, LANES), 1)
    low_half = lane < HEAD_DIM
    for sb in range(TM // Q_BLOCK):
        r0 = sb * Q_BLOCK
        n_invalid = jnp.where(is_first, TM - r0, 0)
        cmask = jnp.where(col < n_invalid, MASK_VALUE, 0.0).astype(F32)
        for p in range(N_HEADS // 2):
            cs = slice(p * LANES, (p + 1) * LANES)
            qp = q_ref[r0:r0 + Q_BLOCK, cs]
            kp = kcat_ref[r0:r0 + KEY_BLOCK, cs]
            vp = vcat_ref[r0:r0 + KEY_BLOCK, cs]
            outs = []
            for par in range(2):
                keep = low_half if par == 0 else jnp.logical_not(low_half)
                qm = jnp.where(keep, qp, jnp.zeros_like(qp))
                s = lax.dot_general(qm, kp, (((1,), (1,)), ((), ())),
                                    preferred_element_type=F32)
                s = s + bias_ref[2 * p + par] + cmask
                m = jnp.max(s, axis=-1, keepdims=True)
                e = jnp.exp(s - m)
                l = jnp.sum(e, axis=-1, keepdims=True)
                o = jnp.dot(e.astype(BF16), vp, preferred_element_type=F32)
                outs.append(o * (1.0 / l))
            o_ref[r0:r0 + Q_BLOCK, cs] = jnp.where(low_half, outs[0], outs[1]).astype(BF16)


def _mix_ffn2_kernel(tiles_per_seq, x1_ref, up_ref, halo_ref, attn_ref, nm_ref, wg_ref,
                     bg_ref, poolw_ref, pscale_ref, wbp_ref, wba_ref, wout_ref,
                     n2_ref, wg2_ref, wu2_ref, wd_ref, nf_ref,
                     out_ref, hb_ref, act_ref, ext_ref, pm_ref, mg_ref):
    i = pl.program_id(0)
    tile_in_seq = i % tiles_per_seq
    x1 = x1_ref[...]
    hb_ref[...] = _rmsnorm(x1, nm_ref[...]).astype(BF16)

    keep_halo = jnp.where(tile_in_seq == 0, 0.0, 1.0).astype(F32)
    ext_ref[0:HALO, :] = halo_ref[...] * keep_halo
    ext_ref[HALO:HALO + TM, :] = up_ref[...]
    pos = tile_in_seq * TM + lax.broadcasted_iota(jnp.int32, (TM, 1), 0)
    for g, w in enumerate(POOL_WINDOWS):
        cs = slice(g * POOL_GROUP, (g + 1) * POOL_GROUP)
        u_g = ext_ref[HALO:HALO + TM, cs]
        sums = u_g
        for j in range(1, w):
            sums = sums + ext_ref[HALO - j:HALO - j + TM, cs]
        inv_cnt = 1.0 / jnp.minimum(pos + 1, w).astype(F32)
        mixed = (sums * inv_cnt - u_g).astype(BF16)
        y = jnp.dot(mixed, poolw_ref[g], preferred_element_type=F32)
        pm_ref[:, cs] = (y * pscale_ref[:, cs]).astype(BF16)

    half = D_MODEL // 2
    for c in range(2):
        cs = slice(c * half, (c + 1) * half)
        cs1 = slice(D_MODEL + c * half, D_MODEL + (c + 1) * half)
        y_pool = jnp.dot(pm_ref[...], wbp_ref[:, cs], preferred_element_type=F32)
        g0 = _sigmoid(jnp.dot(hb_ref[...], wg_ref[:, cs], preferred_element_type=F32)
                      + bg_ref[:, cs])
        merged = g0 * y_pool
        y_attn = jnp.dot(attn_ref[...], wba_ref[:, cs], preferred_element_type=F32)
        g1 = _sigmoid(jnp.dot(hb_ref[...], wg_ref[:, cs1], preferred_element_type=F32)
                      + bg_ref[:, cs1])
        mg_ref[:, cs] = (merged + g1 * y_attn).astype(BF16)
    x2 = x1 + jnp.dot(mg_ref[...], wout_ref[...], preferred_element_type=F32)

    hb_ref[...] = _rmsnorm(x2, n2_ref[...]).astype(BF16)
    x3 = x2 + 0.5 * _swiglu(hb_ref, wg2_ref, wu2_ref, wd_ref, act_ref)
    out_ref[...] = _rmsnorm(x3, nf_ref[...])


def _resident(shape):
    return pl.BlockSpec(shape, lambda i: (0,) * len(shape), pipeline_mode=pl.Buffered(1))


def _toeplitz(w):
    n_heads, width = w.shape
    flat = jnp.tile(jnp.pad(w, ((0, 0), (0, 1))), (1, CHUNK))
    rows = flat[:, :CHUNK * width].reshape(n_heads, CHUNK, width)
    return rows[:, :, CHUNK - 1:]


def _bias_table(rel_bias):
    assert MAX_REL == CHUNK
    rb = rel_bias.astype(F32)
    far = rb[:, 2 * MAX_REL]
    t_own = _toeplitz(rb[:, 1:2 * MAX_REL][:, ::-1])
    t_prev = _toeplitz(jnp.concatenate(
        [jnp.broadcast_to(far[:, None], (rb.shape[0], CHUNK)), rb[:, MAX_REL + 1:2 * MAX_REL][:, ::-1]],
        axis=1))
    reps = (1, Q_BLOCK // CHUNK, KEY_BLOCK // CHUNK)
    r = jnp.arange(Q_BLOCK)[:, None]
    c = jnp.arange(KEY_BLOCK)[None, :]
    dchunk = (c // CHUNK - r // CHUNK)[None]
    table = jnp.where((dchunk >= 0) & (dchunk < LEFT_CHUNKS - 1), far[:, None, None], MASK_VALUE)
    table = jnp.where(dchunk == LEFT_CHUNKS - 1, jnp.tile(t_prev, reps), table)
    return jnp.where(dchunk == LEFT_CHUNKS, jnp.tile(t_own, reps), table)


def kernel(x, ffn1_norm, ffn1_w_gate, ffn1_w_up, ffn1_w_down, mix_norm, w_in, pool_w, pool_scale,
           rel_bias, w_branch_pool, w_branch_attn, w_gate, b_gate, w_out, ffn2_norm, ffn2_w_gate,
           ffn2_w_up, ffn2_w_down, final_norm):
    b, s, d = x.shape
    assert d == D_MODEL and s % TM == 0 and ffn1_norm.shape[0] == 1
    n_tok = b * s
    n_tiles = n_tok // TM
    tiles_per_seq = s // TM
    xf = x.reshape(n_tok, d)

    row = lambda v: v.reshape(1, -1).astype(F32)
    cast = lambda w: w[0].astype(BF16)
    col_scale = jnp.ones((IN_WIDTH,), F32).at[POOL_WIDTH:POOL_WIDTH + ATTN_WIDTH].set(HEAD_DIM ** -0.5)
    win = (w_in[0] * col_scale[None, :]).astype(BF16)

    cparams = pltpu.CompilerParams(dimension_semantics=("arbitrary",),
                                   vmem_limit_bytes=VMEM_LIMIT)
    tile = lambda width: pl.BlockSpec((TM, width), lambda i: (i, 0))

    x1, up, qkv = pl.pallas_call(
        _ffn1_inproj_kernel,
        grid=(n_tiles,),
        in_specs=[tile(D_MODEL), _resident((1, D_MODEL)), _resident((D_MODEL, D_FF)),
                  _resident((D_MODEL, D_FF)), _resident((D_FF, D_MODEL)),
                  _resident((1, D_MODEL)), _resident((D_MODEL, IN_WIDTH))],
        out_specs=[tile(D_MODEL), tile(POOL_WIDTH), tile(3 * ATTN_WIDTH)],
        out_shape=[jax.ShapeDtypeStruct((n_tok, D_MODEL), F32),
                   jax.ShapeDtypeStruct((n_tok, POOL_WIDTH), F32),
                   jax.ShapeDtypeStruct((n_tok, 3 * ATTN_WIDTH), BF16)],
        scratch_shapes=[pltpu.VMEM((TM, D_MODEL), BF16), pltpu.VMEM((TM, D_FF), BF16)],
        compiler_params=cparams,
        name="ffn1_inproj",
    )(xf, row(ffn1_norm), cast(ffn1_w_gate), cast(ffn1_w_up), cast(ffn1_w_down),
      row(mix_norm), win)

    prev = lambda i: jnp.maximum(i - 1, 0)
    attn = pl.pallas_call(
        functools.partial(_attn_kernel, tiles_per_seq),
        grid=(n_tiles,),
        in_specs=[pl.BlockSpec((TM, ATTN_WIDTH), lambda i: (i, 0)),
                  pl.BlockSpec((TM, ATTN_WIDTH), lambda i: (prev(i), 1)),
                  pl.BlockSpec((TM, ATTN_WIDTH), lambda i: (i, 1)),
                  pl.BlockSpec((TM, ATTN_WIDTH), lambda i: (prev(i), 2)),
                  pl.BlockSpec((TM, ATTN_WIDTH), lambda i: (i, 2)),
                  _resident((N_HEADS, Q_BLOCK, KEY_BLOCK))],
        out_specs=tile(ATTN_WIDTH),
        out_shape=jax.ShapeDtypeStruct((n_tok, ATTN_WIDTH), BF16),
        scratch_shapes=[pltpu.VMEM((2 * TM, ATTN_WIDTH), BF16),
                        pltpu.VMEM((2 * TM, ATTN_WIDTH), BF16)],
        compiler_params=cparams,
        name="chunk_attn",
    )(qkv, qkv, qkv, qkv, qkv, _bias_table(rel_bias[0]))

    halo_blocks = TM // HALO
    out = pl.pallas_call(
        functools.partial(_mix_ffn2_kernel, tiles_per_seq),
        grid=(n_tiles,),
        in_specs=[tile(D_MODEL), tile(POOL_WIDTH),
                  pl.BlockSpec((HALO, POOL_WIDTH),
                               lambda i: (jnp.maximum(i * halo_blocks - 1, 0), 0)),
                  tile(ATTN_WIDTH), _resident((1, D_MODEL)),
                  _resident((D_MODEL, 2 * D_MODEL)), _resident((1, 2 * D_MODEL)),
                  _resident((len(POOL_WINDOWS), POOL_GROUP, POOL_GROUP)),
                  _resident((1, POOL_WIDTH)), _resident((POOL_WIDTH, D_MODEL)),
                  _resident((ATTN_WIDTH, D_MODEL)), _resident((D_MODEL, D_MODEL)),
                  _resident((1, D_MODEL)), _resident((D_MODEL, D_FF)),
                  _resident((D_MODEL, D_FF)), _resident((D_FF, D_MODEL)),
                  _resident((1, D_MODEL))],
        out_specs=tile(D_MODEL),
        out_shape=jax.ShapeDtypeStruct((n_tok, D_MODEL), F32),
        scratch_shapes=[pltpu.VMEM((TM, D_MODEL), BF16), pltpu.VMEM((TM, D_FF), BF16),
                        pltpu.VMEM((HALO + TM, POOL_WIDTH), F32),
                        pltpu.VMEM((TM, POOL_WIDTH), BF16), pltpu.VMEM((TM, D_MODEL), BF16)],
        compiler_params=cparams,
        name="mix_ffn2",
    )(x1, up, up, attn, row(mix_norm), cast(w_gate), row(b_gate), cast(pool_w),
      row(pool_scale), cast(w_branch_pool), cast(w_branch_attn), cast(w_out),
      row(ffn2_norm), cast(ffn2_w_gate), cast(ffn2_w_up), cast(ffn2_w_down), row(final_norm))
    return out.reshape(b, s, d)
```

```python
import functools

import jax
import jax.numpy as jnp
from jax import lax
from jax.experimental import pallas as pl
from jax.experimental.pallas import tpu as pltpu

D_MODEL = 1024
D_FF = 2816
CHUNK = 64
LEFT_CHUNKS = 8
POOL_WIDTH = 512
POOL_WINDOWS = (2, 4, 8, 16)
POOL_GROUP = 128
N_HEADS = 8
HEAD_DIM = 64
ATTN_WIDTH = 512
MAX_REL = 64
IN_WIDTH = POOL_WIDTH + 3 * ATTN_WIDTH
EPS = 1e-6

LANES = 128
TM = 512
FF_CHUNK = 512
Q_BLOCK = 256
KEY_BLOCK = Q_BLOCK + LEFT_CHUNKS * CHUNK
HALO = 16
N_CAST = 8
N_TILES = 32
SIDE_ROWS = ((D_MODEL, N_TILES),
             (POOL_WIDTH, N_TILES),
             (ATTN_WIDTH, N_TILES),
             (D_MODEL, N_TILES),
             (D_MODEL, N_TILES),
             (D_MODEL, N_TILES),
             (D_FF, N_TILES // 2),
             (POOL_WIDTH, N_TILES))
MASK_VALUE = -1e30
VMEM_LIMIT = 56 * 1024 * 1024

F32 = jnp.float32
BF16 = jnp.bfloat16


def _rmsnorm(x, g):
    ms = jnp.mean(x * x, axis=-1, keepdims=True)
    return x * lax.rsqrt(ms + EPS) * g


def _sigmoid(z):
    return 0.5 + 0.5 * jnp.tanh(0.5 * z)


def _swiglu(hb_ref, wg_ref, wu_ref, wd_ref, act_ref):
    for lo in range(0, D_FF, FF_CHUNK):
        cs = slice(lo, min(lo + FF_CHUNK, D_FF))
        hg = 0.5 * jnp.dot(hb_ref[...], wg_ref[:, cs], preferred_element_type=F32)
        up = jnp.dot(hb_ref[...], wu_ref[:, cs], preferred_element_type=F32)
        act_ref[:, cs] = ((hg + hg * jnp.tanh(hg)) * up).astype(BF16)
    return jnp.dot(act_ref[...], wd_ref[...], preferred_element_type=F32)


def _ffn1_inproj_kernel(x_ref, n1_ref, wg_blk, wu_blk, wd_blk, nm_ref, win_blk, *rest):
    n_side = len(SIDE_ROWS)
    side_in = rest[:n_side]
    x1_ref, up_ref, qkv_ref = rest[n_side:n_side + 3]
    side_out = rest[n_side + 3:2 * n_side + 3]
    wg_ref, wu_ref, wd_ref, win_ref, hb_ref, act_ref = rest[2 * n_side + 3:]
    i = pl.program_id(0)

    @pl.when(i < N_CAST)
    def _stage_weights():
        for blk, dst in ((wg_blk, wg_ref), (wu_blk, wu_ref), (wd_blk, wd_ref), (win_blk, win_ref)):
            rows = blk.shape[0]
            dst[pl.ds(pl.multiple_of(i * rows, rows), rows), :] = blk[...].astype(BF16)

    @pl.when(i >= N_CAST)
    def _token_tile():
        x = x_ref[...]
        hb_ref[...] = _rmsnorm(x, n1_ref[...]).astype(BF16)
        x1 = x + 0.5 * _swiglu(hb_ref, wg_ref, wu_ref, wd_ref, act_ref)
        x1_ref[...] = x1
        hb_ref[...] = _rmsnorm(x1, nm_ref[...]).astype(BF16)
        up_ref[...] = jnp.dot(hb_ref[...], win_ref[:, :POOL_WIDTH], preferred_element_type=F32)
        for c in range(3):
            lo = POOL_WIDTH + c * ATTN_WIDTH
            y = jnp.dot(hb_ref[...], win_ref[:, lo:lo + ATTN_WIDTH], preferred_element_type=F32)
            if c == 0:
                y = y * (HEAD_DIM ** -0.5)
            qkv_ref[:, c * ATTN_WIDTH:(c + 1) * ATTN_WIDTH] = y.astype(BF16)
        for src, dst, (_, steps) in zip(side_in, side_out, SIDE_ROWS):
            @pl.when(i - N_CAST < steps)
            def _cast_block(src=src, dst=dst):
                dst[...] = src[...].astype(BF16)


def _band_kind(dchunk):
    if dchunk < 0 or dchunk > LEFT_CHUNKS:
        return 0
    if dchunk < LEFT_CHUNKS - 1:
        return 1
    return 2 if dchunk == LEFT_CHUNKS - 1 else 3


def _attn_kernel(tiles_per_seq, q_ref, kp_ref, kc_ref, vp_ref, vc_ref, pieces_ref,
                 o_ref, bias_ref, kcat_ref, vcat_ref):
    i = pl.program_id(0)
    is_first = (i % tiles_per_seq) == 0

    @pl.when(i == 0)
    def _build_bias_table():
        low = lax.broadcasted_iota(jnp.int32, (CHUNK, LANES), 1) < CHUNK

        def per_head(h, carry):
            for qc in range(Q_BLOCK // CHUNK):
                for kp in range(KEY_BLOCK // LANES):
                    ka, kb = _band_kind(2 * kp - qc), _band_kind(2 * kp + 1 - qc)
                    blk = pieces_ref[h, ka]
                    if kb != ka:
                        blk = jnp.where(low, blk, pieces_ref[h, kb])
                    bias_ref[h, qc * CHUNK:(qc + 1) * CHUNK, kp * LANES:(kp + 1) * LANES] = blk
            return carry

        lax.fori_loop(0, N_HEADS, per_head, 0)

    kcat_ref[0:TM, :] = kp_ref[...]
    kcat_ref[TM:2 * TM, :] = kc_ref[...]
    vcat_ref[0:TM, :] = vp_ref[...]
    vcat_ref[TM:2 * TM, :] = vc_ref[...]
    col = lax.broadcasted_iota(jnp.int32, (1, KEY_BLOCK), 1)
    lane = lax.broadcasted_iota(jnp.int32, (Q_BLOCK, LANES), 1)
    low_half = lane < HEAD_DIM
    for sb in range(TM // Q_BLOCK):
        r0 = sb * Q_BLOCK
        n_invalid = jnp.where(is_first, TM - r0, 0)
        cmask = jnp.where(col < n_invalid, MASK_VALUE, 0.0).astype(F32)
        for p in range(N_HEADS // 2):
            cs = slice(p * LANES, (p + 1) * LANES)
            qp = q_ref[r0:r0 + Q_BLOCK, cs]
            kp = kcat_ref[r0:r0 + KEY_BLOCK, cs]
            vp = vcat_ref[r0:r0 + KEY_BLOCK, cs]
            outs = []
            for par in range(2):
                keep = low_half if par == 0 else jnp.logical_not(low_half)
                qm = jnp.where(keep, qp, jnp.zeros_like(qp))
                s = lax.dot_general(qm, kp, (((1,), (1,)), ((), ())),
                                    preferred_element_type=F32)
                s = s + bias_ref[2 * p + par] + cmask
                m = jnp.max(s, axis=-1, keepdims=True)
                e = jnp.exp(s - m)
                l = jnp.sum(e, axis=-1, keepdims=True)
                o = jnp.dot(e.astype(BF16), vp, preferred_element_type=F32)
                outs.append(o * (1.0 / l))
            o_ref[r0:r0 + Q_BLOCK, cs] = jnp.where(low_half, outs[0], outs[1]).astype(BF16)


def _mix_ffn2_kernel(tiles_per_seq, x1_ref, up_ref, halo_ref, attn_ref, nm_ref, wg_ref,
                     bg_ref, poolw_ref, pscale_ref, wbp_ref, wba_ref, wout_ref,
                     n2_ref, wg2_ref, wu2_ref, wd_ref, nf_ref,
                     out_ref, hb_ref, act_ref, ext_ref, pm_ref, mg_ref):
    i = pl.program_id(0)
    tile_in_seq = i % tiles_per_seq
    x1 = x1_ref[...]
    hb_ref[...] = _rmsnorm(x1, nm_ref[...]).astype(BF16)

    keep_halo = jnp.where(tile_in_seq == 0, 0.0, 1.0).astype(F32)
    ext_ref[0:HALO, :] = halo_ref[...] * keep_halo
    ext_ref[HALO:HALO + TM, :] = up_ref[...]
    pos = tile_in_seq * TM + lax.broadcasted_iota(jnp.int32, (TM, 1), 0)
    for g, w in enumerate(POOL_WINDOWS):
        cs = slice(g * POOL_GROUP, (g + 1) * POOL_GROUP)
        u_g = ext_ref[HALO:HALO + TM, cs]
        sums = u_g
        for j in range(1, w):
            sums = sums + ext_ref[HALO - j:HALO - j + TM, cs]
        inv_cnt = 1.0 / jnp.minimum(pos + 1, w).astype(F32)
        mixed = (sums * inv_cnt - u_g).astype(BF16)
        y = jnp.dot(mixed, poolw_ref[g], preferred_element_type=F32)
        pm_ref[:, cs] = (y * pscale_ref[:, cs]).astype(BF16)

    half = D_MODEL // 2
    for c in range(2):
        cs = slice(c * half, (c + 1) * half)
        cs1 = slice(D_MODEL + c * half, D_MODEL + (c + 1) * half)
        y_pool = jnp.dot(pm_ref[...], wbp_ref[:, cs], preferred_element_type=F32)
        g0 = _sigmoid(jnp.dot(hb_ref[...], wg_ref[:, cs], preferred_element_type=F32)
                      + bg_ref[:, cs])
        merged = g0 * y_pool
        y_attn = jnp.dot(attn_ref[...], wba_ref[:, cs], preferred_element_type=F32)
        g1 = _sigmoid(jnp.dot(hb_ref[...], wg_ref[:, cs1], preferred_element_type=F32)
                      + bg_ref[:, cs1])
        mg_ref[:, cs] = (merged + g1 * y_attn).astype(BF16)
    x2 = x1 + jnp.dot(mg_ref[...], wout_ref[...], preferred_element_type=F32)

    hb_ref[...] = _rmsnorm(x2, n2_ref[...]).astype(BF16)
    x3 = x2 + 0.5 * _swiglu(hb_ref, wg2_ref, wu2_ref, wd_ref, act_ref)
    out_ref[...] = _rmsnorm(x3, nf_ref[...])


def _resident(shape):
    return pl.BlockSpec(shape, lambda i: (0,) * len(shape), pipeline_mode=pl.Buffered(1))


def _toeplitz(w):
    n_heads, width = w.shape
    flat = jnp.tile(jnp.pad(w, ((0, 0), (0, 1))), (1, CHUNK))
    rows = flat[:, :CHUNK * width].reshape(n_heads, CHUNK, width)
    return rows[:, :, CHUNK - 1:]


def _bias_pieces(rel_bias):
    assert MAX_REL == CHUNK
    rb = rel_bias.astype(F32)
    n_heads = rb.shape[0]
    far = rb[:, 2 * MAX_REL]
    t_own = _toeplitz(rb[:, 1:2 * MAX_REL][:, ::-1])
    t_prev = _toeplitz(jnp.concatenate(
        [jnp.broadcast_to(far[:, None], (n_heads, CHUNK)), rb[:, MAX_REL + 1:2 * MAX_REL][:, ::-1]],
        axis=1))
    full = (n_heads, CHUNK, LANES)
    return jnp.stack([jnp.full(full, MASK_VALUE, F32),
                      jnp.broadcast_to(far[:, None, None], full),
                      jnp.concatenate([t_prev, t_prev], axis=2),
                      jnp.concatenate([t_own, t_own], axis=2)], axis=1)


def kernel(x, ffn1_norm, ffn1_w_gate, ffn1_w_up, ffn1_w_down, mix_norm, w_in, pool_w, pool_scale,
           rel_bias, w_branch_pool, w_branch_attn, w_gate, b_gate, w_out, ffn2_norm, ffn2_w_gate,
           ffn2_w_up, ffn2_w_down, final_norm):
    b, s, d = x.shape
    assert d == D_MODEL and s % TM == 0 and ffn1_norm.shape[0] == 1
    n_tok = b * s
    n_tiles = n_tok // TM
    assert n_tiles == N_TILES
    tiles_per_seq = s // TM
    xf = x.reshape(n_tok, d)

    row = lambda v: v.reshape(1, -1).astype(F32)
    cparams = pltpu.CompilerParams(dimension_semantics=("arbitrary",),
                                   vmem_limit_bytes=VMEM_LIMIT)
    tile = lambda width: pl.BlockSpec((TM, width), lambda i: (i, 0))

    tok = lambda i: jnp.maximum(i - N_CAST, 0)
    tile1 = lambda width: pl.BlockSpec((TM, width), lambda i: (tok(i), 0))
    staged = lambda w: pl.BlockSpec((w.shape[0] // N_CAST, w.shape[1]),
                                    lambda i: (jnp.minimum(i, N_CAST - 1), 0))
    own_w = [ffn1_w_gate[0], ffn1_w_up[0], ffn1_w_down[0], w_in[0]]
    side_w = [w_gate[0], w_branch_pool[0], w_branch_attn[0], w_out[0], ffn2_w_gate[0],
              ffn2_w_up[0], ffn2_w_down[0], pool_w[0].reshape(POOL_WIDTH, POOL_GROUP)]
    def side_specs():
        specs = []
        for w, (rows, steps) in zip(side_w, SIDE_ROWS):
            assert w.shape[0] == rows
            specs.append(pl.BlockSpec(
                (rows // steps, w.shape[1]),
                functools.partial(lambda steps, i: (jnp.minimum(tok(i), steps - 1), 0), steps)))
        return specs

    res = pl.pallas_call(
        _ffn1_inproj_kernel,
        grid=(N_CAST + n_tiles,),
        in_specs=[tile1(D_MODEL), _resident((1, D_MODEL)), staged(own_w[0]), staged(own_w[1]),
                  staged(own_w[2]), _resident((1, D_MODEL)), staged(own_w[3])] + side_specs(),
        out_specs=[tile1(D_MODEL), tile1(POOL_WIDTH), tile1(3 * ATTN_WIDTH)] + side_specs(),
        out_shape=[jax.ShapeDtypeStruct((n_tok, D_MODEL), F32),
                   jax.ShapeDtypeStruct((n_tok, POOL_WIDTH), F32),
                   jax.ShapeDtypeStruct((n_tok, 3 * ATTN_WIDTH), BF16)]
                  + [jax.ShapeDtypeStruct(w.shape, BF16) for w in side_w],
        scratch_shapes=[pltpu.VMEM(w.shape, BF16) for w in own_w]
                       + [pltpu.VMEM((TM, D_MODEL), BF16), pltpu.VMEM((TM, D_FF), BF16)],
        compiler_params=cparams,
        name="ffn1_inproj",
    )(xf, row(ffn1_norm), own_w[0], own_w[1], own_w[2], row(mix_norm), own_w[3], *side_w)
    x1, up, qkv = res[:3]
    wgate_b, wbp_b, wba_b, wout_b, wg2_b, wu2_b, wd2_b, poolw_b = res[3:]

    prev = lambda i: jnp.maximum(i - 1, 0)
    attn = pl.pallas_call(
        functools.partial(_attn_kernel, tiles_per_seq),
        grid=(n_tiles,),
        in_specs=[pl.BlockSpec((TM, ATTN_WIDTH), lambda i: (i, 0)),
                  pl.BlockSpec((TM, ATTN_WIDTH), lambda i: (prev(i), 1)),
                  pl.BlockSpec((TM, ATTN_WIDTH), lambda i: (i, 1)),
                  pl.BlockSpec((TM, ATTN_WIDTH), lambda i: (prev(i), 2)),
                  pl.BlockSpec((TM, ATTN_WIDTH), lambda i: (i, 2)),
                  _resident((N_HEADS, 4, CHUNK, LANES))],
        out_specs=tile(ATTN_WIDTH),
        out_shape=jax.ShapeDtypeStruct((n_tok, ATTN_WIDTH), BF16),
        scratch_shapes=[pltpu.VMEM((N_HEADS, Q_BLOCK, KEY_BLOCK), F32),
                        pltpu.VMEM((2 * TM, ATTN_WIDTH), BF16),
                        pltpu.VMEM((2 * TM, ATTN_WIDTH), BF16)],
        compiler_params=cparams,
        name="chunk_attn",
    )(qkv, qkv, qkv, qkv, qkv, _bias_pieces(rel_bias[0]))

    halo_blocks = TM // HALO
    out = pl.pallas_call(
        functools.partial(_mix_ffn2_kernel, tiles_per_seq),
        grid=(n_tiles,),
        in_specs=[tile(D_MODEL), tile(POOL_WIDTH),
                  pl.BlockSpec((HALO, POOL_WIDTH),
                               lambda i: (jnp.maximum(i * halo_blocks - 1, 0), 0)),
                  tile(ATTN_WIDTH), _resident((1, D_MODEL)),
                  _resident((D_MODEL, 2 * D_MODEL)), _resident((1, 2 * D_MODEL)),
                  _resident((len(POOL_WINDOWS), POOL_GROUP, POOL_GROUP)),
                  _resident((1, POOL_WIDTH)), _resident((POOL_WIDTH, D_MODEL)),
                  _resident((ATTN_WIDTH, D_MODEL)), _resident((D_MODEL, D_MODEL)),
                  _resident((1, D_MODEL)), _resident((D_MODEL, D_FF)),
                  _resident((D_MODEL, D_FF)), _resident((D_FF, D_MODEL)),
                  _resident((1, D_MODEL))],
        out_specs=tile(D_MODEL),
        out_shape=jax.ShapeDtypeStruct((n_tok, D_MODEL), F32),
        scratch_shapes=[pltpu.VMEM((TM, D_MODEL), BF16), pltpu.VMEM((TM, D_FF), BF16),
                        pltpu.VMEM((HALO + TM, POOL_WIDTH), F32),
                        pltpu.VMEM((TM, POOL_WIDTH), BF16), pltpu.VMEM((TM, D_MODEL), BF16)],
        compiler_params=cparams,
        name="mix_ffn2",
    )(x1, up, up, attn, row(mix_norm), wgate_b, row(b_gate),
      poolw_b.reshape(len(POOL_WINDOWS), POOL_GROUP, POOL_GROUP), row(pool_scale), wbp_b, wba_b,
      wout_b, row(ffn2_norm), wg2_b, wu2_b, wd2_b, row(final_norm))
    return out.reshape(b, s, d)
```

```python
import functools

import jax
import jax.numpy as jnp
from jax import lax
from jax.experimental import pallas as pl
from jax.experimental.pallas import tpu as pltpu

D_MODEL = 1024
D_FF = 2816
CHUNK = 64
LEFT_CHUNKS = 8
POOL_WIDTH = 512
POOL_WINDOWS = (2, 4, 8, 16)
POOL_GROUP = 128
N_HEADS = 8
HEAD_DIM = 64
ATTN_WIDTH = 512
MAX_REL = 64
IN_WIDTH = POOL_WIDTH + 3 * ATTN_WIDTH
EPS = 1e-6

LANES = 128
TM = 512
FF_CHUNK = 512
Q_BLOCK = 256
KEY_BLOCK = Q_BLOCK + LEFT_CHUNKS * CHUNK
HALO = 16
N_CAST = 8
N_TILES = 32
SIDE_ROWS = ((D_MODEL, N_TILES),
             (POOL_WIDTH, N_TILES),
             (ATTN_WIDTH, N_TILES),
             (D_MODEL, N_TILES),
             (D_MODEL, N_TILES),
             (D_MODEL, N_TILES),
             (D_FF, N_TILES // 2),
             (POOL_WIDTH, N_TILES))
MASK_VALUE = -1e30
LOG2E = 1.4426950408889634
Q_SCALE = HEAD_DIM ** -0.5 * LOG2E
VMEM_LIMIT = 56 * 1024 * 1024

F32 = jnp.float32
BF16 = jnp.bfloat16


def _rmsnorm(x, g):
    ms = jnp.mean(x * x, axis=-1, keepdims=True)
    return x * lax.rsqrt(ms + EPS) * g


def _sigmoid(z):
    return 0.5 + 0.5 * jnp.tanh(0.5 * z)


def _swiglu(hb_ref, wg_ref, wu_ref, wd_ref, act_ref):
    for lo in range(0, D_FF, FF_CHUNK):
        cs = slice(lo, min(lo + FF_CHUNK, D_FF))
        hg = 0.5 * jnp.dot(hb_ref[...], wg_ref[:, cs], preferred_element_type=F32)
        up = jnp.dot(hb_ref[...], wu_ref[:, cs], preferred_element_type=F32)
        act_ref[:, cs] = ((hg + hg * jnp.tanh(hg)) * up).astype(BF16)
    return jnp.dot(act_ref[...], wd_ref[...], preferred_element_type=F32)


def _ffn1_inproj_kernel(x_ref, n1_ref, wg_blk, wu_blk, wd_blk, nm_ref, win_blk, *rest):
    n_side = len(SIDE_ROWS)
    side_in = rest[:n_side]
    x1_ref, up_ref, qkv_ref = rest[n_side:n_side + 3]
    side_out = rest[n_side + 3:2 * n_side + 3]
    wg_ref, wu_ref, wd_ref, win_ref, hb_ref, act_ref = rest[2 * n_side + 3:]
    i = pl.program_id(0)

    @pl.when(i < N_CAST)
    def _stage_weights():
        for blk, dst in ((wg_blk, wg_ref), (wu_blk, wu_ref), (wd_blk, wd_ref), (win_blk, win_ref)):
            rows = blk.shape[0]
            dst[pl.ds(pl.multiple_of(i * rows, rows), rows), :] = blk[...].astype(BF16)

    @pl.when(i >= N_CAST)
    def _token_tile():
        x = x_ref[...]
        hb_ref[...] = _rmsnorm(x, n1_ref[...]).astype(BF16)
        x1 = x + 0.5 * _swiglu(hb_ref, wg_ref, wu_ref, wd_ref, act_ref)
        x1_ref[...] = x1
        hb_ref[...] = _rmsnorm(x1, nm_ref[...]).astype(BF16)
        up_ref[...] = jnp.dot(hb_ref[...], win_ref[:, :POOL_WIDTH], preferred_element_type=F32)
        for c in range(3):
            lo = POOL_WIDTH + c * ATTN_WIDTH
            y = jnp.dot(hb_ref[...], win_ref[:, lo:lo + ATTN_WIDTH], preferred_element_type=F32)
            if c == 0:
                y = y * Q_SCALE
            qkv_ref[:, c * ATTN_WIDTH:(c + 1) * ATTN_WIDTH] = y.astype(BF16)
        for src, dst, (_, steps) in zip(side_in, side_out, SIDE_ROWS):
            @pl.when(i - N_CAST < steps)
            def _cast_block(src=src, dst=dst):
                dst[...] = src[...].astype(BF16)


def _band_kind(dchunk):
    if dchunk < 0 or dchunk > LEFT_CHUNKS:
        return 0
    if dchunk < LEFT_CHUNKS - 1:
        return 1
    return 2 if dchunk == LEFT_CHUNKS - 1 else 3


def _attn_kernel(tiles_per_seq, q_ref, kp_ref, kc_ref, vp_ref, vc_ref, pieces_ref,
                 o_ref, bias_ref, kcat_ref, v_even_ref, v_odd_ref):
    i = pl.program_id(0)
    is_first = (i % tiles_per_seq) == 0
    n_qblocks = TM // Q_BLOCK

    @pl.when(i == 0)
    def _build_bias_tables():
        low = lax.broadcasted_iota(jnp.int32, (CHUNK, LANES), 1) < CHUNK

        def per_head(h, carry):
            for variant in range(1 + n_qblocks):
                n_before = 0 if variant == 0 else (TM - (variant - 1) * Q_BLOCK) // LANES
                for qc in range(Q_BLOCK // CHUNK):
                    for kp in range(KEY_BLOCK // LANES):
                        ka, kb = _band_kind(2 * kp - qc), _band_kind(2 * kp + 1 - qc)
                        if kp < n_before:
                            ka = kb = 0
                        blk = pieces_ref[h, ka]
                        if kb != ka:
                            blk = jnp.where(low, blk, pieces_ref[h, kb])
                        bias_ref[variant, h, qc * CHUNK:(qc + 1) * CHUNK,
                                 kp * LANES:(kp + 1) * LANES] = blk
            return carry

        lax.fori_loop(0, N_HEADS, per_head, 0)

    kcat_ref[0:TM, :] = kp_ref[...]
    kcat_ref[TM:2 * TM, :] = kc_ref[...]
    even_lanes = (lax.broadcasted_iota(jnp.int32, (TM, ATTN_WIDTH), 1) % LANES) < HEAD_DIM
    ones = jnp.ones((TM, ATTN_WIDTH), BF16)
    for src, rows in ((vp_ref, slice(0, TM)), (vc_ref, slice(TM, 2 * TM))):
        v = src[...]
        v_even_ref[rows, :] = jnp.where(even_lanes, v, ones)
        v_odd_ref[rows, :] = jnp.where(even_lanes, ones, v)

    low_half = lax.broadcasted_iota(jnp.int32, (Q_BLOCK, LANES), 1) < HEAD_DIM
    for sb in range(n_qblocks):
        r0 = sb * Q_BLOCK
        variant = jnp.where(is_first, 1 + sb, 0)
        for p in range(N_HEADS // 2):
            cs = slice(p * LANES, (p + 1) * LANES)
            qp = q_ref[r0:r0 + Q_BLOCK, cs]
            kp = kcat_ref[r0:r0 + KEY_BLOCK, cs]
            outs = []
            for par, v_ref in enumerate((v_even_ref, v_odd_ref)):
                keep = low_half if par == 0 else jnp.logical_not(low_half)
                qm = jnp.where(keep, qp, jnp.zeros_like(qp))
                s = lax.dot_general(qm, kp, (((1,), (1,)), ((), ())),
                                    preferred_element_type=F32)
                s = s + bias_ref[variant, 2 * p + par]
                e = jnp.exp2(s - jnp.max(s, axis=-1, keepdims=True))
                o = jnp.dot(e.astype(BF16), v_ref[r0:r0 + KEY_BLOCK, cs],
                            preferred_element_type=F32)
                outs.append(o / pltpu.roll(o, HEAD_DIM, axis=1))
            o_ref[r0:r0 + Q_BLOCK, cs] = jnp.where(low_half, outs[0], outs[1]).astype(BF16)


def _mix_ffn2_kernel(tiles_per_seq, x1_ref, up_ref, halo_ref, attn_ref, nm_ref, wg_ref,
                     bg_ref, poolw_ref, pscale_ref, wbp_ref, wba_ref, wout_ref,
                     n2_ref, wg2_ref, wu2_ref, wd_ref, nf_ref,
                     out_ref, hb_ref, act_ref, ext_ref, pm_ref, mg_ref):
    i = pl.program_id(0)
    tile_in_seq = i % tiles_per_seq
    x1 = x1_ref[...]
    hb_ref[...] = _rmsnorm(x1, nm_ref[...]).astype(BF16)

    keep_halo = jnp.where(tile_in_seq == 0, 0.0, 1.0).astype(F32)
    ext_ref[0:HALO, :] = halo_ref[...] * keep_halo
    ext_ref[HALO:HALO + TM, :] = up_ref[...]
    pos = tile_in_seq * TM + lax.broadcasted_iota(jnp.int32, (TM, 1), 0)
    for g, w in enumerate(POOL_WINDOWS):
        cs = slice(g * POOL_GROUP, (g + 1) * POOL_GROUP)
        u_g = ext_ref[HALO:HALO + TM, cs]
        sums = u_g
        for j in range(1, w):
            sums = sums + ext_ref[HALO - j:HALO - j + TM, cs]
        inv_cnt = 1.0 / jnp.minimum(pos + 1, w).astype(F32)
        mixed = (sums * inv_cnt - u_g).astype(BF16)
        y = jnp.dot(mixed, poolw_ref[g], preferred_element_type=F32)
        pm_ref[:, cs] = (y * pscale_ref[:, cs]).astype(BF16)

    half = D_MODEL // 2
    for c in range(2):
        cs = slice(c * half, (c + 1) * half)
        cs1 = slice(D_MODEL + c * half, D_MODEL + (c + 1) * half)
        y_pool = jnp.dot(pm_ref[...], wbp_ref[:, cs], preferred_element_type=F32)
        g0 = _sigmoid(jnp.dot(hb_ref[...], wg_ref[:, cs], preferred_element_type=F32)
                      + bg_ref[:, cs])
        merged = g0 * y_pool
        y_attn = jnp.dot(attn_ref[...], wba_ref[:, cs], preferred_element_type=F32)
        g1 = _sigmoid(jnp.dot(hb_ref[...], wg_ref[:, cs1], preferred_element_type=F32)
                      + bg_ref[:, cs1])
        mg_ref[:, cs] = (merged + g1 * y_attn).astype(BF16)
    x2 = x1 + jnp.dot(mg_ref[...], wout_ref[...], preferred_element_type=F32)

    hb_ref[...] = _rmsnorm(x2, n2_ref[...]).astype(BF16)
    x3 = x2 + 0.5 * _swiglu(hb_ref, wg2_ref, wu2_ref, wd_ref, act_ref)
    out_ref[...] = _rmsnorm(x3, nf_ref[...])


def _resident(shape):
    return pl.BlockSpec(shape, lambda i: (0,) * len(shape), pipeline_mode=pl.Buffered(1))


def _toeplitz(w):
    n_heads, width = w.shape
    flat = jnp.tile(jnp.pad(w, ((0, 0), (0, 1))), (1, CHUNK))
    rows = flat[:, :CHUNK * width].reshape(n_heads, CHUNK, width)
    return rows[:, :, CHUNK - 1:]


def _bias_pieces(rel_bias):
    assert MAX_REL == CHUNK
    rb = rel_bias.astype(F32) * LOG2E
    n_heads = rb.shape[0]
    far = rb[:, 2 * MAX_REL]
    t_own = _toeplitz(rb[:, 1:2 * MAX_REL][:, ::-1])
    t_prev = _toeplitz(jnp.concatenate(
        [jnp.broadcast_to(far[:, None], (n_heads, CHUNK)), rb[:, MAX_REL + 1:2 * MAX_REL][:, ::-1]],
        axis=1))
    full = (n_heads, CHUNK, LANES)
    return jnp.stack([jnp.full(full, MASK_VALUE, F32),
                      jnp.broadcast_to(far[:, None, None], full),
                      jnp.concatenate([t_prev, t_prev], axis=2),
                      jnp.concatenate([t_own, t_own], axis=2)], axis=1)


def kernel(x, ffn1_norm, ffn1_w_gate, ffn1_w_up, ffn1_w_down, mix_norm, w_in, pool_w, pool_scale,
           rel_bias, w_branch_pool, w_branch_attn, w_gate, b_gate, w_out, ffn2_norm, ffn2_w_gate,
           ffn2_w_up, ffn2_w_down, final_norm):
    b, s, d = x.shape
    assert d == D_MODEL and s % TM == 0 and ffn1_norm.shape[0] == 1
    n_tok = b * s
    n_tiles = n_tok // TM
    assert n_tiles == N_TILES
    tiles_per_seq = s // TM
    xf = x.reshape(n_tok, d)

    row = lambda v: v.reshape(1, -1).astype(F32)
    cparams = pltpu.CompilerParams(dimension_semantics=("arbitrary",),
                                   vmem_limit_bytes=VMEM_LIMIT)
    tile = lambda width: pl.BlockSpec((TM, width), lambda i: (i, 0))

    tok = lambda i: jnp.maximum(i - N_CAST, 0)
    tile1 = lambda width: pl.BlockSpec((TM, width), lambda i: (tok(i), 0))
    staged = lambda w: pl.BlockSpec((w.shape[0] // N_CAST, w.shape[1]),
                                    lambda i: (jnp.minimum(i, N_CAST - 1), 0))
    own_w = [ffn1_w_gate[0], ffn1_w_up[0], ffn1_w_down[0], w_in[0]]
    side_w = [w_gate[0], w_branch_pool[0], w_branch_attn[0], w_out[0], ffn2_w_gate[0],
              ffn2_w_up[0], ffn2_w_down[0], pool_w[0].reshape(POOL_WIDTH, POOL_GROUP)]
    def side_specs():
        specs = []
        for w, (rows, steps) in zip(side_w, SIDE_ROWS):
            assert w.shape[0] == rows
            specs.append(pl.BlockSpec(
                (rows // steps, w.shape[1]),
                functools.partial(lambda steps, i: (jnp.minimum(tok(i), steps - 1), 0), steps)))
        return specs

    res = pl.pallas_call(
        _ffn1_inproj_kernel,
        grid=(N_CAST + n_tiles,),
        in_specs=[tile1(D_MODEL), _resident((1, D_MODEL)), staged(own_w[0]), staged(own_w[1]),
                  staged(own_w[2]), _resident((1, D_MODEL)), staged(own_w[3])] + side_specs(),
        out_specs=[tile1(D_MODEL), tile1(POOL_WIDTH), tile1(3 * ATTN_WIDTH)] + side_specs(),
        out_shape=[jax.ShapeDtypeStruct((n_tok, D_MODEL), F32),
                   jax.ShapeDtypeStruct((n_tok, POOL_WIDTH), F32),
                   jax.ShapeDtypeStruct((n_tok, 3 * ATTN_WIDTH), BF16)]
                  + [jax.ShapeDtypeStruct(w.shape, BF16) for w in side_w],
        scratch_shapes=[pltpu.VMEM(w.shape, BF16) for w in own_w]
                       + [pltpu.VMEM((TM, D_MODEL), BF16), pltpu.VMEM((TM, D_FF), BF16)],
        compiler_params=cparams,
        name="ffn1_inproj",
    )(xf, row(ffn1_norm), own_w[0], own_w[1], own_w[2], row(mix_norm), own_w[3], *side_w)
    x1, up, qkv = res[:3]
    wgate_b, wbp_b, wba_b, wout_b, wg2_b, wu2_b, wd2_b, poolw_b = res[3:]

    prev = lambda i: jnp.maximum(i - 1, 0)
    attn = pl.pallas_call(
        functools.partial(_attn_kernel, tiles_per_seq),
        grid=(n_tiles,),
        in_specs=[pl.BlockSpec((TM, ATTN_WIDTH), lambda i: (i, 0)),
                  pl.BlockSpec((TM, ATTN_WIDTH), lambda i: (prev(i), 1)),
                  pl.BlockSpec((TM, ATTN_WIDTH), lambda i: (i, 1)),
                  pl.BlockSpec((TM, ATTN_WIDTH), lambda i: (prev(i), 2)),
                  pl.BlockSpec((TM, ATTN_WIDTH), lambda i: (i, 2)),
                  _resident((N_HEADS, 4, CHUNK, LANES))],
        out_specs=tile(ATTN_WIDTH),
        out_shape=jax.ShapeDtypeStruct((n_tok, ATTN_WIDTH), BF16),
        scratch_shapes=[pltpu.VMEM((1 + TM // Q_BLOCK, N_HEADS, Q_BLOCK, KEY_BLOCK), F32),
                        pltpu.VMEM((2 * TM, ATTN_WIDTH), BF16),
                        pltpu.VMEM((2 * TM, ATTN_WIDTH), BF16),
                        pltpu.VMEM((2 * TM, ATTN_WIDTH), BF16)],
        compiler_params=cparams,
        name="chunk_attn",
    )(qkv, qkv, qkv, qkv, qkv, _bias_pieces(rel_bias[0]))

    halo_blocks = TM // HALO
    out = pl.pallas_call(
        functools.partial(_mix_ffn2_kernel, tiles_per_seq),
        grid=(n_tiles,),
        in_specs=[tile(D_MODEL), tile(POOL_WIDTH),
                  pl.BlockSpec((HALO, POOL_WIDTH),
                               lambda i: (jnp.maximum(i * halo_blocks - 1, 0), 0)),
                  tile(ATTN_WIDTH), _resident((1, D_MODEL)),
                  _resident((D_MODEL, 2 * D_MODEL)), _resident((1, 2 * D_MODEL)),
                  _resident((len(POOL_WINDOWS), POOL_GROUP, POOL_GROUP)),
                  _resident((1, POOL_WIDTH)), _resident((POOL_WIDTH, D_MODEL)),
                  _resident((ATTN_WIDTH, D_MODEL)), _resident((D_MODEL, D_MODEL)),
                  _resident((1, D_MODEL)), _resident((D_MODEL, D_FF)),
                  _resident((D_MODEL, D_FF)), _resident((D_FF, D_MODEL)),
                  _resident((1, D_MODEL))],
        out_specs=tile(D_MODEL),
        out_shape=jax.ShapeDtypeStruct((n_tok, D_MODEL), F32),
        scratch_shapes=[pltpu.VMEM((TM, D_MODEL), BF16), pltpu.VMEM((TM, D_FF), BF16),
                        pltpu.VMEM((HALO + TM, POOL_WIDTH), F32),
                        pltpu.VMEM((TM, POOL_WIDTH), BF16), pltpu.VMEM((TM, D_MODEL), BF16)],
        compiler_params=cparams,
        name="mix_ffn2",
    )(x1, up, up, attn, row(mix_norm), wgate_b, row(b_gate),
      poolw_b.reshape(len(POOL_WINDOWS), POOL_GROUP, POOL_GROUP), row(pool_scale), wbp_b, wba_b,
      wout_b, row(ffn2_norm), wg2_b, wu2_b, wd2_b, row(final_norm))
    return out.reshape(b, s, d)
```

```python
import functools

import jax
import jax.numpy as jnp
from jax import lax
from jax.experimental import pallas as pl
from jax.experimental.pallas import tpu as pltpu

D_MODEL = 1024
D_FF = 2816
CHUNK = 64
LEFT_CHUNKS = 8
POOL_WIDTH = 512
POOL_WINDOWS = (2, 4, 8, 16)
POOL_GROUP = 128
N_HEADS = 8
HEAD_DIM = 64
ATTN_WIDTH = 512
MAX_REL = 64
IN_WIDTH = POOL_WIDTH + 3 * ATTN_WIDTH
EPS = 1e-6

LANES = 128
TM = 512
FF_CHUNK = 512
Q_BLOCK = 256
KEY_BLOCK = Q_BLOCK + LEFT_CHUNKS * CHUNK
SCORE_LOOKAHEAD = 2
HALO = 16
N_CAST = 8
N_TILES = 32
SIDE_ROWS = ((D_MODEL, N_TILES),
             (POOL_WIDTH, N_TILES),
             (ATTN_WIDTH, N_TILES),
             (D_MODEL, N_TILES),
             (D_MODEL, N_TILES),
             (D_MODEL, N_TILES),
             (D_FF, N_TILES // 2),
             (POOL_WIDTH, N_TILES))
MASK_VALUE = -1e30
LOG2E = 1.4426950408889634
Q_SCALE = HEAD_DIM ** -0.5 * LOG2E
VMEM_LIMIT = 56 * 1024 * 1024

F32 = jnp.float32
BF16 = jnp.bfloat16


def _rmsnorm(x, g):
    ms = jnp.mean(x * x, axis=-1, keepdims=True)
    return x * lax.rsqrt(ms + EPS) * g


def _sigmoid(z):
    return 0.5 + 0.5 * jnp.tanh(0.5 * z)


def _swiglu(hb_ref, wg_ref, wu_ref, wd_ref, act_ref):
    for lo in range(0, D_FF, FF_CHUNK):
        cs = slice(lo, min(lo + FF_CHUNK, D_FF))
        hg = 0.5 * jnp.dot(hb_ref[...], wg_ref[:, cs], preferred_element_type=F32)
        up = jnp.dot(hb_ref[...], wu_ref[:, cs], preferred_element_type=F32)
        act_ref[:, cs] = ((hg + hg * jnp.tanh(hg)) * up).astype(BF16)
    return jnp.dot(act_ref[...], wd_ref[...], preferred_element_type=F32)


def _ffn1_inproj_kernel(x_ref, n1_ref, wg_blk, wu_blk, wd_blk, nm_ref, win_blk, *rest):
    n_side = len(SIDE_ROWS)
    side_in = rest[:n_side]
    x1_ref, up_ref, qkv_ref = rest[n_side:n_side + 3]
    side_out = rest[n_side + 3:2 * n_side + 3]
    wg_ref, wu_ref, wd_ref, win_ref, hb_ref, act_ref = rest[2 * n_side + 3:]
    i = pl.program_id(0)

    @pl.when(i < N_CAST)
    def _stage_weights():
        for blk, dst in ((wg_blk, wg_ref), (wu_blk, wu_ref), (wd_blk, wd_ref), (win_blk, win_ref)):
            rows = blk.shape[0]
            dst[pl.ds(pl.multiple_of(i * rows, rows), rows), :] = blk[...].astype(BF16)

    @pl.when(i >= N_CAST)
    def _token_tile():
        x = x_ref[...]
        hb_ref[...] = _rmsnorm(x, n1_ref[...]).astype(BF16)
        x1 = x + 0.5 * _swiglu(hb_ref, wg_ref, wu_ref, wd_ref, act_ref)
        x1_ref[...] = x1
        hb_ref[...] = _rmsnorm(x1, nm_ref[...]).astype(BF16)
        up_ref[...] = jnp.dot(hb_ref[...], win_ref[:, :POOL_WIDTH], preferred_element_type=F32)
        for c in range(3):
            lo = POOL_WIDTH + c * ATTN_WIDTH
            y = jnp.dot(hb_ref[...], win_ref[:, lo:lo + ATTN_WIDTH], preferred_element_type=F32)
            if c == 0:
                y = y * Q_SCALE
            qkv_ref[:, c * ATTN_WIDTH:(c + 1) * ATTN_WIDTH] = y.astype(BF16)
        for src, dst, (_, steps) in zip(side_in, side_out, SIDE_ROWS):
            @pl.when(i - N_CAST < steps)
            def _cast_block(src=src, dst=dst):
                dst[...] = src[...].astype(BF16)


def _band_kind(dchunk):
    if dchunk < 0 or dchunk > LEFT_CHUNKS:
        return 0
    if dchunk < LEFT_CHUNKS - 1:
        return 1
    return 2 if dchunk == LEFT_CHUNKS - 1 else 3


def _attn_kernel(tiles_per_seq, q_ref, kp_ref, kc_ref, vp_ref, vc_ref, pieces_ref,
                 o_ref, bias_ref, v_even_ref, v_odd_ref):
    i = pl.program_id(0)
    is_first = (i % tiles_per_seq) == 0
    n_qblocks = TM // Q_BLOCK

    @pl.when(i == 0)
    def _build_bias_tables():
        low = lax.broadcasted_iota(jnp.int32, (CHUNK, LANES), 1) < CHUNK

        def per_head(h, carry):
            for variant in range(1 + n_qblocks):
                n_before = 0 if variant == 0 else (TM - (variant - 1) * Q_BLOCK) // LANES
                for qc in range(Q_BLOCK // CHUNK):
                    for kp in range(KEY_BLOCK // LANES):
                        ka, kb = _band_kind(2 * kp - qc), _band_kind(2 * kp + 1 - qc)
                        if kp < n_before:
                            ka = kb = 0
                        blk = pieces_ref[h, ka]
                        if kb != ka:
                            blk = jnp.where(low, blk, pieces_ref[h, kb])
                        bias_ref[variant, h, qc * CHUNK:(qc + 1) * CHUNK,
                                 kp * LANES:(kp + 1) * LANES] = blk
            return carry

        lax.fori_loop(0, N_HEADS, per_head, 0)

    even_lanes = (lax.broadcasted_iota(jnp.int32, (TM, ATTN_WIDTH), 1) % LANES) < HEAD_DIM
    ones = jnp.ones((TM, ATTN_WIDTH), BF16)
    for src, rows in ((vp_ref, slice(0, TM)), (vc_ref, slice(TM, 2 * TM))):
        v = src[...]
        v_even_ref[rows, :] = jnp.where(even_lanes, v, ones)
        v_odd_ref[rows, :] = jnp.where(even_lanes, ones, v)

    low_half = lax.broadcasted_iota(jnp.int32, (Q_BLOCK, LANES), 1) < HEAD_DIM
    units = [(sb, p, par) for sb in range(n_qblocks) for p in range(N_HEADS // 2)
             for par in range(2)]

    def scores(unit):
        sb, p, par = unit
        r0, cs = sb * Q_BLOCK, slice(p * LANES, (p + 1) * LANES)
        qp = q_ref[r0:r0 + Q_BLOCK, cs]
        qm = jnp.where(low_half if par == 0 else jnp.logical_not(low_half), qp, jnp.zeros_like(qp))
        nt = (((1,), (1,)), ((), ()))
        s = jnp.concatenate(
            [lax.dot_general(qm, kp_ref[r0:TM, cs], nt, preferred_element_type=F32),
             lax.dot_general(qm, kc_ref[0:r0 + Q_BLOCK, cs], nt, preferred_element_type=F32)],
            axis=1)
        return s + bias_ref[jnp.where(is_first, 1 + sb, 0), 2 * p + par]

    pending = [scores(u) for u in units[:SCORE_LOOKAHEAD]]
    outs = []
    for n, (sb, p, par) in enumerate(units):
        r0, cs = sb * Q_BLOCK, slice(p * LANES, (p + 1) * LANES)
        s = pending.pop(0)
        e = jnp.exp2(s - jnp.max(s, axis=-1, keepdims=True))
        if n + SCORE_LOOKAHEAD < len(units):
            pending.append(scores(units[n + SCORE_LOOKAHEAD]))
        v_ref = v_even_ref if par == 0 else v_odd_ref
        o = jnp.dot(e.astype(BF16), v_ref[r0:r0 + KEY_BLOCK, cs], preferred_element_type=F32)
        outs.append(o / pltpu.roll(o, HEAD_DIM, axis=1))
        if par == 1:
            o_ref[r0:r0 + Q_BLOCK, cs] = jnp.where(low_half, outs[-2], outs[-1]).astype(BF16)


def _mix_ffn2_kernel(tiles_per_seq, x1_ref, up_ref, halo_ref, attn_ref, nm_ref, wg_ref,
                     bg_ref, poolw_ref, pscale_ref, wbp_ref, wba_ref, wout_ref,
                     n2_ref, wg2_ref, wu2_ref, wd_ref, nf_ref,
                     out_ref, hb_ref, act_ref, ext_ref, pm_ref, mg_ref):
    i = pl.program_id(0)
    tile_in_seq = i % tiles_per_seq
    x1 = x1_ref[...]
    hb_ref[...] = _rmsnorm(x1, nm_ref[...]).astype(BF16)

    keep_halo = jnp.where(tile_in_seq == 0, 0.0, 1.0).astype(F32)
    ext_ref[0:HALO, :] = halo_ref[...] * keep_halo
    ext_ref[HALO:HALO + TM, :] = up_ref[...]
    pos = tile_in_seq * TM + lax.broadcasted_iota(jnp.int32, (TM, 1), 0)
    for g, w in enumerate(POOL_WINDOWS):
        cs = slice(g * POOL_GROUP, (g + 1) * POOL_GROUP)
        u_g = ext_ref[HALO:HALO + TM, cs]
        sums = u_g
        for j in range(1, w):
            sums = sums + ext_ref[HALO - j:HALO - j + TM, cs]
        inv_cnt = 1.0 / jnp.minimum(pos + 1, w).astype(F32)
        mixed = (sums * inv_cnt - u_g).astype(BF16)
        y = jnp.dot(mixed, poolw_ref[g], preferred_element_type=F32)
        pm_ref[:, cs] = (y * pscale_ref[:, cs]).astype(BF16)

    half = D_MODEL // 2
    for c in range(2):
        cs = slice(c * half, (c + 1) * half)
        cs1 = slice(D_MODEL + c * half, D_MODEL + (c + 1) * half)
        y_pool = jnp.dot(pm_ref[...], wbp_ref[:, cs], preferred_element_type=F32)
        g0 = _sigmoid(jnp.dot(hb_ref[...], wg_ref[:, cs], preferred_element_type=F32)
                      + bg_ref[:, cs])
        merged = g0 * y_pool
        y_attn = jnp.dot(attn_ref[...], wba_ref[:, cs], preferred_element_type=F32)
        g1 = _sigmoid(jnp.dot(hb_ref[...], wg_ref[:, cs1], preferred_element_type=F32)
                      + bg_ref[:, cs1])
        mg_ref[:, cs] = (merged + g1 * y_attn).astype(BF16)
    x2 = x1 + jnp.dot(mg_ref[...], wout_ref[...], preferred_element_type=F32)

    hb_ref[...] = _rmsnorm(x2, n2_ref[...]).astype(BF16)
    x3 = x2 + 0.5 * _swiglu(hb_ref, wg2_ref, wu2_ref, wd_ref, act_ref)
    out_ref[...] = _rmsnorm(x3, nf_ref[...])


def _resident(shape):
    return pl.BlockSpec(shape, lambda i: (0,) * len(shape), pipeline_mode=pl.Buffered(1))


def _toeplitz(w):
    n_heads, width = w.shape
    flat = jnp.tile(jnp.pad(w, ((0, 0), (0, 1))), (1, CHUNK))
    rows = flat[:, :CHUNK * width].reshape(n_heads, CHUNK, width)
    return rows[:, :, CHUNK - 1:]


def _bias_pieces(rel_bias):
    assert MAX_REL == CHUNK
    rb = rel_bias.astype(F32) * LOG2E
    n_heads = rb.shape[0]
    far = rb[:, 2 * MAX_REL]
    t_own = _toeplitz(rb[:, 1:2 * MAX_REL][:, ::-1])
    t_prev = _toeplitz(jnp.concatenate(
        [jnp.broadcast_to(far[:, None], (n_heads, CHUNK)), rb[:, MAX_REL + 1:2 * MAX_REL][:, ::-1]],
        axis=1))
    full = (n_heads, CHUNK, LANES)
    return jnp.stack([jnp.full(full, MASK_VALUE, F32),
                      jnp.broadcast_to(far[:, None, None], full),
                      jnp.concatenate([t_prev, t_prev], axis=2),
                      jnp.concatenate([t_own, t_own], axis=2)], axis=1)


def kernel(x, ffn1_norm, ffn1_w_gate, ffn1_w_up, ffn1_w_down, mix_norm, w_in, pool_w, pool_scale,
           rel_bias, w_branch_pool, w_branch_attn, w_gate, b_gate, w_out, ffn2_norm, ffn2_w_gate,
           ffn2_w_up, ffn2_w_down, final_norm):
    b, s, d = x.shape
    assert d == D_MODEL and s % TM == 0 and ffn1_norm.shape[0] == 1
    n_tok = b * s
    n_tiles = n_tok // TM
    assert n_tiles == N_TILES
    tiles_per_seq = s // TM
    xf = x.reshape(n_tok, d)

    row = lambda v: v.reshape(1, -1).astype(F32)
    cparams = pltpu.CompilerParams(dimension_semantics=("arbitrary",),
                                   vmem_limit_bytes=VMEM_LIMIT)
    tile = lambda width: pl.BlockSpec((TM, width), lambda i: (i, 0))

    tok = lambda i: jnp.maximum(i - N_CAST, 0)
    tile1 = lambda width: pl.BlockSpec((TM, width), lambda i: (tok(i), 0))
    staged = lambda w: pl.BlockSpec((w.shape[0] // N_CAST, w.shape[1]),
                                    lambda i: (jnp.minimum(i, N_CAST - 1), 0))
    own_w = [ffn1_w_gate[0], ffn1_w_up[0], ffn1_w_down[0], w_in[0]]
    side_w = [w_gate[0], w_branch_pool[0], w_branch_attn[0], w_out[0], ffn2_w_gate[0],
              ffn2_w_up[0], ffn2_w_down[0], pool_w[0].reshape(POOL_WIDTH, POOL_GROUP)]
    def side_specs():
        specs = []
        for w, (rows, steps) in zip(side_w, SIDE_ROWS):
            assert w.shape[0] == rows
            specs.append(pl.BlockSpec(
                (rows // steps, w.shape[1]),
                functools.partial(lambda steps, i: (jnp.minimum(tok(i), steps - 1), 0), steps)))
        return specs

    res = pl.pallas_call(
        _ffn1_inproj_kernel,
        grid=(N_CAST + n_tiles,),
        in_specs=[tile1(D_MODEL), _resident((1, D_MODEL)), staged(own_w[0]), staged(own_w[1]),
                  staged(own_w[2]), _resident((1, D_MODEL)), staged(own_w[3])] + side_specs(),
        out_specs=[tile1(D_MODEL), tile1(POOL_WIDTH), tile1(3 * ATTN_WIDTH)] + side_specs(),
        out_shape=[jax.ShapeDtypeStruct((n_tok, D_MODEL), F32),
                   jax.ShapeDtypeStruct((n_tok, POOL_WIDTH), F32),
                   jax.ShapeDtypeStruct((n_tok, 3 * ATTN_WIDTH), BF16)]
                  + [jax.ShapeDtypeStruct(w.shape, BF16) for w in side_w],
        scratch_shapes=[pltpu.VMEM(w.shape, BF16) for w in own_w]
                       + [pltpu.VMEM((TM, D_MODEL), BF16), pltpu.VMEM((TM, D_FF), BF16)],
        compiler_params=cparams,
        name="ffn1_inproj",
    )(xf, row(ffn1_norm), own_w[0], own_w[1], own_w[2], row(mix_norm), own_w[3], *side_w)
    x1, up, qkv = res[:3]
    wgate_b, wbp_b, wba_b, wout_b, wg2_b, wu2_b, wd2_b, poolw_b = res[3:]

    prev = lambda i: jnp.maximum(i - 1, 0)
    attn = pl.pallas_call(
        functools.partial(_attn_kernel, tiles_per_seq),
        grid=(n_tiles,),
        in_specs=[pl.BlockSpec((TM, ATTN_WIDTH), lambda i: (i, 0)),
                  pl.BlockSpec((TM, ATTN_WIDTH), lambda i: (prev(i), 1)),
                  pl.BlockSpec((TM, ATTN_WIDTH), lambda i: (i, 1)),
                  pl.BlockSpec((TM, ATTN_WIDTH), lambda i: (prev(i), 2)),
                  pl.BlockSpec((TM, ATTN_WIDTH), lambda i: (i, 2)),
                  _resident((N_HEADS, 4, CHUNK, LANES))],
        out_specs=tile(ATTN_WIDTH),
        out_shape=jax.ShapeDtypeStruct((n_tok, ATTN_WIDTH), BF16),
        scratch_shapes=[pltpu.VMEM((1 + TM // Q_BLOCK, N_HEADS, Q_BLOCK, KEY_BLOCK), F32),
                        pltpu.VMEM((2 * TM, ATTN_WIDTH), BF16),
                        pltpu.VMEM((2 * TM, ATTN_WIDTH), BF16)],
        compiler_params=cparams,
        name="chunk_attn",
    )(qkv, qkv, qkv, qkv, qkv, _bias_pieces(rel_bias[0]))

    halo_blocks = TM // HALO
    out = pl.pallas_call(
        functools.partial(_mix_ffn2_kernel, tiles_per_seq),
        grid=(n_tiles,),
        in_specs=[tile(D_MODEL), tile(POOL_WIDTH),
                  pl.BlockSpec((HALO, POOL_WIDTH),
                               lambda i: (jnp.maximum(i * halo_blocks - 1, 0), 0)),
                  tile(ATTN_WIDTH), _resident((1, D_MODEL)),
                  _resident((D_MODEL, 2 * D_MODEL)), _resident((1, 2 * D_MODEL)),
                  _resident((len(POOL_WINDOWS), POOL_GROUP, POOL_GROUP)),
                  _resident((1, POOL_WIDTH)), _resident((POOL_WIDTH, D_MODEL)),
                  _resident((ATTN_WIDTH, D_MODEL)), _resident((D_MODEL, D_MODEL)),
                  _resident((1, D_MODEL)), _resident((D_MODEL, D_FF)),
                  _resident((D_MODEL, D_FF)), _resident((D_FF, D_MODEL)),
                  _resident((1, D_MODEL))],
        out_specs=tile(D_MODEL),
        out_shape=jax.ShapeDtypeStruct((n_tok, D_MODEL), F32),
        scratch_shapes=[pltpu.VMEM((TM, D_MODEL), BF16), pltpu.VMEM((TM, D_FF), BF16),
                        pltpu.VMEM((HALO + TM, POOL_WIDTH), F32),
                        pltpu.VMEM((TM, POOL_WIDTH), BF16), pltpu.VMEM((TM, D_MODEL), BF16)],
        compiler_params=cparams,
        name="mix_ffn2",
    )(x1, up, up, attn, row(mix_norm), wgate_b, row(b_gate),
      poolw_b.reshape(len(POOL_WINDOWS), POOL_GROUP, POOL_GROUP), row(pool_scale), wbp_b, wba_b,
      wout_b, row(ffn2_norm), wg2_b, wu2_b, wd2_b, row(final_norm))
    return out.reshape(b, s, d)
```

```python
import functools

import jax
import jax.numpy as jnp
from jax import lax
from jax.experimental import pallas as pl
from jax.experimental.pallas import tpu as pltpu

D_MODEL = 1024
D_FF = 2816
CHUNK = 64
LEFT_CHUNKS = 8
POOL_WIDTH = 512
POOL_WINDOWS = (2, 4, 8, 16)
POOL_GROUP = 128
N_HEADS = 8
HEAD_DIM = 64
ATTN_WIDTH = 512
MAX_REL = 64
IN_WIDTH = POOL_WIDTH + 3 * ATTN_WIDTH
EPS = 1e-6

LANES = 128
TM = 512
FF_CHUNK = 512
Q_BLOCK = 256
KEY_BLOCK = Q_BLOCK + LEFT_CHUNKS * CHUNK
SCORE_LOOKAHEAD = 2
HALO = 16
POOL_PAD = 8
N_CAST = 8
N_TILES = 32
SIDE_ROWS = ((D_MODEL, N_TILES),
             (POOL_WIDTH, N_TILES),
             (ATTN_WIDTH, N_TILES),
             (D_MODEL, N_TILES),
             (D_MODEL, N_TILES),
             (D_MODEL, N_TILES),
             (D_FF, N_TILES // 2),
             (POOL_WIDTH, N_TILES))
MASK_VALUE = -1e30
LOG2E = 1.4426950408889634
Q_SCALE = HEAD_DIM ** -0.5 * LOG2E
VMEM_LIMIT = 56 * 1024 * 1024

F32 = jnp.float32
BF16 = jnp.bfloat16


def _rmsnorm(x, g):
    ms = jnp.mean(x * x, axis=-1, keepdims=True)
    return x * lax.rsqrt(ms + EPS) * g


def _sigmoid(z):
    return 0.5 + 0.5 * jnp.tanh(0.5 * z)


def _swiglu(hb_ref, wg_ref, wu_ref, wd_ref, act_ref):
    for lo in range(0, D_FF, FF_CHUNK):
        cs = slice(lo, min(lo + FF_CHUNK, D_FF))
        hg = 0.5 * jnp.dot(hb_ref[...], wg_ref[:, cs], preferred_element_type=F32)
        up = jnp.dot(hb_ref[...], wu_ref[:, cs], preferred_element_type=F32)
        act_ref[:, cs] = ((hg + hg * jnp.tanh(hg)) * up).astype(BF16)
    return jnp.dot(act_ref[...], wd_ref[...], preferred_element_type=F32)


def _ffn1_inproj_kernel(x_ref, n1_ref, wg_blk, wu_blk, wd_blk, nm_ref, win_blk, *rest):
    n_side = len(SIDE_ROWS)
    side_in = rest[:n_side]
    x1_ref, up_ref, qkv_ref = rest[n_side:n_side + 3]
    side_out = rest[n_side + 3:2 * n_side + 3]
    wg_ref, wu_ref, wd_ref, win_ref, hb_ref, act_ref = rest[2 * n_side + 3:]
    i = pl.program_id(0)

    @pl.when(i < N_CAST)
    def _stage_weights():
        for blk, dst in ((wg_blk, wg_ref), (wu_blk, wu_ref), (wd_blk, wd_ref), (win_blk, win_ref)):
            rows = blk.shape[0]
            dst[pl.ds(pl.multiple_of(i * rows, rows), rows), :] = blk[...].astype(BF16)

    @pl.when(i >= N_CAST)
    def _token_tile():
        x = x_ref[...]
        hb_ref[...] = _rmsnorm(x, n1_ref[...]).astype(BF16)
        x1 = x + 0.5 * _swiglu(hb_ref, wg_ref, wu_ref, wd_ref, act_ref)
        x1_ref[...] = x1
        hb_ref[...] = _rmsnorm(x1, nm_ref[...]).astype(BF16)
        up_ref[...] = jnp.dot(hb_ref[...], win_ref[:, :POOL_WIDTH], preferred_element_type=F32)
        for c in range(3):
            lo = POOL_WIDTH + c * ATTN_WIDTH
            y = jnp.dot(hb_ref[...], win_ref[:, lo:lo + ATTN_WIDTH], preferred_element_type=F32)
            if c == 0:
                y = y * Q_SCALE
            qkv_ref[:, c * ATTN_WIDTH:(c + 1) * ATTN_WIDTH] = y.astype(BF16)
        for src, dst, (_, steps) in zip(side_in, side_out, SIDE_ROWS):
            @pl.when(i - N_CAST < steps)
            def _cast_block(src=src, dst=dst):
                dst[...] = src[...].astype(BF16)


def _band_kind(dchunk):
    if dchunk < 0 or dchunk > LEFT_CHUNKS:
        return 0
    if dchunk < LEFT_CHUNKS - 1:
        return 1
    return 2 if dchunk == LEFT_CHUNKS - 1 else 3


def _attn_kernel(tiles_per_seq, q_ref, kp_ref, kc_ref, vp_ref, vc_ref, pieces_ref,
                 o_ref, bias_ref, v_even_ref, v_odd_ref):
    i = pl.program_id(0)
    is_first = (i % tiles_per_seq) == 0
    n_qblocks = TM // Q_BLOCK

    @pl.when(i == 0)
    def _build_bias_tables():
        low = lax.broadcasted_iota(jnp.int32, (CHUNK, LANES), 1) < CHUNK

        def per_head(h, carry):
            for variant in range(1 + n_qblocks):
                n_before = 0 if variant == 0 else (TM - (variant - 1) * Q_BLOCK) // LANES
                for qc in range(Q_BLOCK // CHUNK):
                    for kp in range(KEY_BLOCK // LANES):
                        ka, kb = _band_kind(2 * kp - qc), _band_kind(2 * kp + 1 - qc)
                        if kp < n_before:
                            ka = kb = 0
                        blk = pieces_ref[h, ka]
                        if kb != ka:
                            blk = jnp.where(low, blk, pieces_ref[h, kb])
                        bias_ref[variant, h, qc * CHUNK:(qc + 1) * CHUNK,
                                 kp * LANES:(kp + 1) * LANES] = blk
            return carry

        lax.fori_loop(0, N_HEADS, per_head, 0)

    even_lanes = (lax.broadcasted_iota(jnp.int32, (TM, ATTN_WIDTH), 1) % LANES) < HEAD_DIM
    ones = jnp.ones((TM, ATTN_WIDTH), BF16)
    for src, rows in ((vp_ref, slice(0, TM)), (vc_ref, slice(TM, 2 * TM))):
        v = src[...]
        v_even_ref[rows, :] = jnp.where(even_lanes, v, ones)
        v_odd_ref[rows, :] = jnp.where(even_lanes, ones, v)

    low_half = lax.broadcasted_iota(jnp.int32, (Q_BLOCK, LANES), 1) < HEAD_DIM
    units = [(sb, p, par) for sb in range(n_qblocks) for p in range(N_HEADS // 2)
             for par in range(2)]

    def scores(unit):
        sb, p, par = unit
        r0, cs = sb * Q_BLOCK, slice(p * LANES, (p + 1) * LANES)
        qp = q_ref[r0:r0 + Q_BLOCK, cs]
        qm = jnp.where(low_half if par == 0 else jnp.logical_not(low_half), qp, jnp.zeros_like(qp))
        nt = (((1,), (1,)), ((), ()))
        s = jnp.concatenate(
            [lax.dot_general(qm, kp_ref[r0:TM, cs], nt, preferred_element_type=F32),
             lax.dot_general(qm, kc_ref[0:r0 + Q_BLOCK, cs], nt, preferred_element_type=F32)],
            axis=1)
        return s + bias_ref[jnp.where(is_first, 1 + sb, 0), 2 * p + par]

    pending = [scores(u) for u in units[:SCORE_LOOKAHEAD]]
    outs = []
    for n, (sb, p, par) in enumerate(units):
        r0, cs = sb * Q_BLOCK, slice(p * LANES, (p + 1) * LANES)
        s = pending.pop(0)
        e = jnp.exp2(s - jnp.max(s, axis=-1, keepdims=True))
        if n + SCORE_LOOKAHEAD < len(units):
            pending.append(scores(units[n + SCORE_LOOKAHEAD]))
        v_ref = v_even_ref if par == 0 else v_odd_ref
        o = jnp.dot(e.astype(BF16), v_ref[r0:r0 + KEY_BLOCK, cs], preferred_element_type=F32)
        outs.append(o / pltpu.roll(o, HEAD_DIM, axis=1))
        if par == 1:
            o_ref[r0:r0 + Q_BLOCK, cs] = jnp.where(low_half, outs[-2], outs[-1]).astype(BF16)


def _mix_ffn2_kernel(tiles_per_seq, x1_ref, up_ref, halo_ref, attn_ref, nm_ref, wg_ref,
                     bg_ref, poolw_ref, pscale_ref, wbp_ref, wba_ref, wout_ref,
                     n2_ref, wg2_ref, wu2_ref, wd_ref, nf_ref,
                     out_ref, hb_ref, act_ref, ext_ref, lvl_a_ref, lvl_b_ref, pm_ref, mg_ref, ga_ref):
    i = pl.program_id(0)
    tile_in_seq = i % tiles_per_seq
    x1 = x1_ref[...]
    hb_ref[...] = _rmsnorm(x1, nm_ref[...]).astype(BF16)

    base = POOL_PAD + HALO
    keep_halo = jnp.where(tile_in_seq == 0, 0.0, 1.0).astype(F32)
    lvl_refs = (lvl_a_ref, lvl_b_ref)
    for ref in (ext_ref,) + lvl_refs:
        ref[0:POOL_PAD, :] = jnp.zeros((POOL_PAD, POOL_WIDTH), F32)
    ext_ref[POOL_PAD:base, :] = halo_ref[...] * keep_halo
    ext_ref[base:base + TM, :] = up_ref[...]
    pos_top = tile_in_seq * TM + lax.broadcasted_iota(jnp.int32, (HALO, 1), 0)

    def pool_group(g):
        w = POOL_WINDOWS[g]
        cs = slice(g * POOL_GROUP, (g + 1) * POOL_GROUP)
        src, shift, level = ext_ref, 1, 0
        while 2 * shift < w:
            lvl_ref = lvl_refs[level % 2]
            lvl_ref[POOL_PAD:base + TM, cs] = (src[POOL_PAD:base + TM, cs]
                                              + src[POOL_PAD - shift:base + TM - shift, cs])
            src, shift, level = lvl_ref, 2 * shift, level + 1
        sums = src[base:base + TM, cs] + src[base - shift:base + TM - shift, cs]
        inv_top = 1.0 / jnp.minimum(pos_top + 1, w).astype(F32)
        mean = jnp.concatenate([sums[:HALO] * inv_top, sums[HALO:] * (1.0 / w)], axis=0)
        mixed = (mean - ext_ref[base:base + TM, cs]).astype(BF16)
        y = jnp.dot(mixed, poolw_ref[g], preferred_element_type=F32)
        pm_ref[:, cs] = (y * pscale_ref[:, cs]).astype(BF16)

    half = D_MODEL // 2
    for c, groups in enumerate(((3, 0), (1, 2))):
        cs = slice(c * half, (c + 1) * half)
        cs1 = slice(D_MODEL + c * half, D_MODEL + (c + 1) * half)
        y_attn = jnp.dot(attn_ref[...], wba_ref[:, cs], preferred_element_type=F32)
        g1 = _sigmoid(jnp.dot(hb_ref[...], wg_ref[:, cs1], preferred_element_type=F32)
                      + bg_ref[:, cs1])
        ga_ref[:, cs] = g1 * y_attn
        for g in groups:
            pool_group(g)

    for c in range(2):
        cs = slice(c * half, (c + 1) * half)
        g0 = _sigmoid(jnp.dot(hb_ref[...], wg_ref[:, cs], preferred_element_type=F32)
                      + bg_ref[:, cs])
        y_pool = jnp.dot(pm_ref[...], wbp_ref[:, cs], preferred_element_type=F32)
        mg_ref[:, cs] = (g0 * y_pool + ga_ref[:, cs]).astype(BF16)
    x2 = x1 + jnp.dot(mg_ref[...], wout_ref[...], preferred_element_type=F32)

    hb_ref[...] = _rmsnorm(x2, n2_ref[...]).astype(BF16)
    x3 = x2 + 0.5 * _swiglu(hb_ref, wg2_ref, wu2_ref, wd_ref, act_ref)
    out_ref[...] = _rmsnorm(x3, nf_ref[...])


def _resident(shape):
    return pl.BlockSpec(shape, lambda i: (0,) * len(shape), pipeline_mode=pl.Buffered(1))


def _toeplitz(w):
    n_heads, width = w.shape
    flat = jnp.tile(jnp.pad(w, ((0, 0), (0, 1))), (1, CHUNK))
    rows = flat[:, :CHUNK * width].reshape(n_heads, CHUNK, width)
    return rows[:, :, CHUNK - 1:]


def _bias_pieces(rel_bias):
    assert MAX_REL == CHUNK
    rb = rel_bias.astype(F32) * LOG2E
    n_heads = rb.shape[0]
    far = rb[:, 2 * MAX_REL]
    t_own = _toeplitz(rb[:, 1:2 * MAX_REL][:, ::-1])
    t_prev = _toeplitz(jnp.concatenate(
        [jnp.broadcast_to(far[:, None], (n_heads, CHUNK)), rb[:, MAX_REL + 1:2 * MAX_REL][:, ::-1]],
        axis=1))
    full = (n_heads, CHUNK, LANES)
    return jnp.stack([jnp.full(full, MASK_VALUE, F32),
                      jnp.broadcast_to(far[:, None, None], full),
                      jnp.concatenate([t_prev, t_prev], axis=2),
                      jnp.concatenate([t_own, t_own], axis=2)], axis=1)


def kernel(x, ffn1_norm, ffn1_w_gate, ffn1_w_up, ffn1_w_down, mix_norm, w_in, pool_w, pool_scale,
           rel_bias, w_branch_pool, w_branch_attn, w_gate, b_gate, w_out, ffn2_norm, ffn2_w_gate,
           ffn2_w_up, ffn2_w_down, final_norm):
    b, s, d = x.shape
    assert d == D_MODEL and s % TM == 0 and ffn1_norm.shape[0] == 1
    n_tok = b * s
    n_tiles = n_tok // TM
    assert n_tiles == N_TILES
    tiles_per_seq = s // TM
    xf = x.reshape(n_tok, d)

    row = lambda v: v.reshape(1, -1).astype(F32)
    cparams = pltpu.CompilerParams(dimension_semantics=("arbitrary",),
                                   vmem_limit_bytes=VMEM_LIMIT)
    tile = lambda width: pl.BlockSpec((TM, width), lambda i: (i, 0))

    tok = lambda i: jnp.maximum(i - N_CAST, 0)
    tile1 = lambda width: pl.BlockSpec((TM, width), lambda i: (tok(i), 0))
    staged = lambda w: pl.BlockSpec((w.shape[0] // N_CAST, w.shape[1]),
                                    lambda i: (jnp.minimum(i, N_CAST - 1), 0))
    own_w = [ffn1_w_gate[0], ffn1_w_up[0], ffn1_w_down[0], w_in[0]]
    side_w = [w_gate[0], w_branch_pool[0], w_branch_attn[0], w_out[0], ffn2_w_gate[0],
              ffn2_w_up[0], ffn2_w_down[0], pool_w[0].reshape(POOL_WIDTH, POOL_GROUP)]
    def side_specs():
        specs = []
        for w, (rows, steps) in zip(side_w, SIDE_ROWS):
            assert w.shape[0] == rows
            specs.append(pl.BlockSpec(
                (rows // steps, w.shape[1]),
                functools.partial(lambda steps, i: (jnp.minimum(tok(i), steps - 1), 0), steps)))
        return specs

    res = pl.pallas_call(
        _ffn1_inproj_kernel,
        grid=(N_CAST + n_tiles,),
        in_specs=[tile1(D_MODEL), _resident((1, D_MODEL)), staged(own_w[0]), staged(own_w[1]),
                  staged(own_w[2]), _resident((1, D_MODEL)), staged(own_w[3])] + side_specs(),
        out_specs=[tile1(D_MODEL), tile1(POOL_WIDTH), tile1(3 * ATTN_WIDTH)] + side_specs(),
        out_shape=[jax.ShapeDtypeStruct((n_tok, D_MODEL), F32),
                   jax.ShapeDtypeStruct((n_tok, POOL_WIDTH), F32),
                   jax.ShapeDtypeStruct((n_tok, 3 * ATTN_WIDTH), BF16)]
                  + [jax.ShapeDtypeStruct(w.shape, BF16) for w in side_w],
        scratch_shapes=[pltpu.VMEM(w.shape, BF16) for w in own_w]
                       + [pltpu.VMEM((TM, D_MODEL), BF16), pltpu.VMEM((TM, D_FF), BF16)],
        compiler_params=cparams,
        name="ffn1_inproj",
    )(xf, row(ffn1_norm), own_w[0], own_w[1], own_w[2], row(mix_norm), own_w[3], *side_w)
    x1, up, qkv = res[:3]
    wgate_b, wbp_b, wba_b, wout_b, wg2_b, wu2_b, wd2_b, poolw_b = res[3:]

    prev = lambda i: jnp.maximum(i - 1, 0)
    attn = pl.pallas_call(
        functools.partial(_attn_kernel, tiles_per_seq),
        grid=(n_tiles,),
        in_specs=[pl.BlockSpec((TM, ATTN_WIDTH), lambda i: (i, 0)),
                  pl.BlockSpec((TM, ATTN_WIDTH), lambda i: (prev(i), 1)),
                  pl.BlockSpec((TM, ATTN_WIDTH), lambda i: (i, 1)),
                  pl.BlockSpec((TM, ATTN_WIDTH), lambda i: (prev(i), 2)),
                  pl.BlockSpec((TM, ATTN_WIDTH), lambda i: (i, 2)),
                  _resident((N_HEADS, 4, CHUNK, LANES))],
        out_specs=tile(ATTN_WIDTH),
        out_shape=jax.ShapeDtypeStruct((n_tok, ATTN_WIDTH), BF16),
        scratch_shapes=[pltpu.VMEM((1 + TM // Q_BLOCK, N_HEADS, Q_BLOCK, KEY_BLOCK), F32),
                        pltpu.VMEM((2 * TM, ATTN_WIDTH), BF16),
                        pltpu.VMEM((2 * TM, ATTN_WIDTH), BF16)],
        compiler_params=cparams,
        name="chunk_attn",
    )(qkv, qkv, qkv, qkv, qkv, _bias_pieces(rel_bias[0]))

    halo_blocks = TM // HALO
    out = pl.pallas_call(
        functools.partial(_mix_ffn2_kernel, tiles_per_seq),
        grid=(n_tiles,),
        in_specs=[tile(D_MODEL), tile(POOL_WIDTH),
                  pl.BlockSpec((HALO, POOL_WIDTH),
                               lambda i: (jnp.maximum(i * halo_blocks - 1, 0), 0)),
                  tile(ATTN_WIDTH), _resident((1, D_MODEL)),
                  _resident((D_MODEL, 2 * D_MODEL)), _resident((1, 2 * D_MODEL)),
                  _resident((len(POOL_WINDOWS), POOL_GROUP, POOL_GROUP)),
                  _resident((1, POOL_WIDTH)), _resident((POOL_WIDTH, D_MODEL)),
                  _resident((ATTN_WIDTH, D_MODEL)), _resident((D_MODEL, D_MODEL)),
                  _resident((1, D_MODEL)), _resident((D_MODEL, D_FF)),
                  _resident((D_MODEL, D_FF)), _resident((D_FF, D_MODEL)),
                  _resident((1, D_MODEL))],
        out_specs=tile(D_MODEL),
        out_shape=jax.ShapeDtypeStruct((n_tok, D_MODEL), F32),
        scratch_shapes=[pltpu.VMEM((TM, D_MODEL), BF16), pltpu.VMEM((TM, D_FF), BF16),
                        pltpu.VMEM((POOL_PAD + HALO + TM, POOL_WIDTH), F32),
                        pltpu.VMEM((POOL_PAD + HALO + TM, POOL_WIDTH), F32),
                        pltpu.VMEM((POOL_PAD + HALO + TM, POOL_WIDTH), F32),
                        pltpu.VMEM((TM, POOL_WIDTH), BF16), pltpu.VMEM((TM, D_MODEL), BF16),
                        pltpu.VMEM((TM, D_MODEL), F32)],
        compiler_params=cparams,
        name="mix_ffn2",
    )(x1, up, up, attn, row(mix_norm), wgate_b, row(b_gate),
      poolw_b.reshape(len(POOL_WINDOWS), POOL_GROUP, POOL_GROUP), row(pool_scale), wbp_b, wba_b,
      wout_b, row(ffn2_norm), wg2_b, wu2_b, wd2_b, row(final_norm))
    return out.reshape(b, s, d)
```

```python
import functools

import jax
import jax.numpy as jnp
from jax import lax
from jax.experimental import pallas as pl
from jax.experimental.pallas import tpu as pltpu

D_MODEL = 1024
D_FF = 2816
CHUNK = 64
LEFT_CHUNKS = 8
POOL_WIDTH = 512
POOL_WINDOWS = (2, 4, 8, 16)
POOL_GROUP = 128
N_HEADS = 8
HEAD_DIM = 64
ATTN_WIDTH = 512
MAX_REL = 64
IN_WIDTH = POOL_WIDTH + 3 * ATTN_WIDTH
EPS = 1e-6

LANES = 128
TM = 512
FF_CHUNK = 512
Q_BLOCK = 256
KEY_BLOCK = Q_BLOCK + LEFT_CHUNKS * CHUNK
SCORE_LOOKAHEAD = 2
HALO = 16
POOL_PAD = 8
N_CAST = 8
N_TILES = 32
SIDE_ROWS = ((D_MODEL, N_TILES),
             (POOL_WIDTH, N_TILES),
             (ATTN_WIDTH, N_TILES),
             (D_MODEL, N_TILES),
             (D_MODEL, N_TILES),
             (D_MODEL, N_TILES),
             (D_FF, N_TILES // 2),
             (POOL_WIDTH, N_TILES))
MASK_VALUE = -1e30
LOG2E = 1.4426950408889634
Q_SCALE = HEAD_DIM ** -0.5 * LOG2E
VMEM_LIMIT = 56 * 1024 * 1024

F32 = jnp.float32
BF16 = jnp.bfloat16


def _rmsnorm(x, g):
    ms = jnp.mean(x * x, axis=-1, keepdims=True)
    return x * lax.rsqrt(ms + EPS) * g


def _sigmoid(z):
    return 0.5 + 0.5 * jnp.tanh(0.5 * z)


def _norm_split(x, g):
    rscale = lax.rsqrt(jnp.mean(x * x, axis=-1, keepdims=True) + EPS)
    return (x * g).astype(BF16), jnp.broadcast_to(rscale, (x.shape[0], LANES))


def _row_scale(y, rscale):
    return jnp.concatenate([y[:, lo:lo + LANES] * rscale for lo in range(0, y.shape[1], LANES)],
                           axis=1)


def _swiglu(hb_ref, rscale, wg_ref, wu_ref, wd_ref, act_ref):
    _swiglu_act(hb_ref, rscale, wg_ref, wu_ref, act_ref)
    return jnp.dot(act_ref[...], wd_ref[...], preferred_element_type=F32)


def _swiglu_act(hb_ref, rscale, wg_ref, wu_ref, act_ref):
    half_rscale = 0.5 * rscale
    for lo in range(0, D_FF, FF_CHUNK):
        cs = slice(lo, min(lo + FF_CHUNK, D_FF))
        hg = _row_scale(jnp.dot(hb_ref[...], wg_ref[:, cs], preferred_element_type=F32),
                        half_rscale)
        up = _row_scale(jnp.dot(hb_ref[...], wu_ref[:, cs], preferred_element_type=F32), rscale)
        act_ref[:, cs] = ((hg + hg * jnp.tanh(hg)) * up).astype(BF16)


def _ffn1_inproj_kernel(x_ref, n1_ref, wg_blk, wu_blk, wd_blk, nm_ref, win_blk, *rest):
    n_side = len(SIDE_ROWS)
    side_in = rest[:n_side]
    x1_ref, up_ref, qkv_ref = rest[n_side:n_side + 3]
    side_out = rest[n_side + 3:2 * n_side + 3]
    wg_ref, wu_ref, wd_ref, win_ref, hb_ref, act_ref = rest[2 * n_side + 3:]
    i = pl.program_id(0)

    @pl.when(i < N_CAST)
    def _stage_weights():
        for blk, dst in ((wg_blk, wg_ref), (wu_blk, wu_ref), (wd_blk, wd_ref), (win_blk, win_ref)):
            rows = blk.shape[0]
            dst[pl.ds(pl.multiple_of(i * rows, rows), rows), :] = blk[...].astype(BF16)

    @pl.when(i >= N_CAST)
    def _token_tile():
        x = x_ref[...]
        hb_ref[...], rscale = _norm_split(x, n1_ref[...])
        x1 = x + 0.5 * _swiglu(hb_ref, rscale, wg_ref, wu_ref, wd_ref, act_ref)
        x1_ref[...] = x1
        hb_ref[...], rscale = _norm_split(x1, nm_ref[...])
        up_ref[...] = _row_scale(
            jnp.dot(hb_ref[...], win_ref[:, :POOL_WIDTH], preferred_element_type=F32), rscale)
        for c in range(3):
            lo = POOL_WIDTH + c * ATTN_WIDTH
            y = jnp.dot(hb_ref[...], win_ref[:, lo:lo + ATTN_WIDTH], preferred_element_type=F32)
            y = _row_scale(y, rscale * Q_SCALE if c == 0 else rscale)
            qkv_ref[:, c * ATTN_WIDTH:(c + 1) * ATTN_WIDTH] = y.astype(BF16)
        for src, dst, (_, steps) in zip(side_in, side_out, SIDE_ROWS):
            @pl.when(i - N_CAST < steps)
            def _cast_block(src=src, dst=dst):
                dst[...] = src[...].astype(BF16)


def _band_kind(dchunk):
    if dchunk < 0 or dchunk > LEFT_CHUNKS:
        return 0
    if dchunk < LEFT_CHUNKS - 1:
        return 1
    return 2 if dchunk == LEFT_CHUNKS - 1 else 3


def _attn_kernel(tiles_per_seq, q_ref, kp_ref, kc_ref, vp_ref, vc_ref, pieces_ref,
                 o_ref, bias_ref, v_even_ref, v_odd_ref):
    i = pl.program_id(0)
    is_first = (i % tiles_per_seq) == 0
    n_qblocks = TM // Q_BLOCK

    @pl.when(i == 0)
    def _build_bias_tables():
        low = lax.broadcasted_iota(jnp.int32, (CHUNK, LANES), 1) < CHUNK

        def per_head(h, carry):
            for variant in range(1 + n_qblocks):
                n_before = 0 if variant == 0 else (TM - (variant - 1) * Q_BLOCK) // LANES
                for qc in range(Q_BLOCK // CHUNK):
                    for kp in range(KEY_BLOCK // LANES):
                        ka, kb = _band_kind(2 * kp - qc), _band_kind(2 * kp + 1 - qc)
                        if kp < n_before:
                            ka = kb = 0
                        blk = pieces_ref[h, ka]
                        if kb != ka:
                            blk = jnp.where(low, blk, pieces_ref[h, kb])
                        bias_ref[variant, h, qc * CHUNK:(qc + 1) * CHUNK,
                                 kp * LANES:(kp + 1) * LANES] = blk
            return carry

        lax.fori_loop(0, N_HEADS, per_head, 0)

    even_lanes = (lax.broadcasted_iota(jnp.int32, (TM, ATTN_WIDTH), 1) % LANES) < HEAD_DIM
    ones = jnp.ones((TM, ATTN_WIDTH), BF16)
    for src, rows in ((vp_ref, slice(0, TM)), (vc_ref, slice(TM, 2 * TM))):
        v = src[...]
        v_even_ref[rows, :] = jnp.where(even_lanes, v, ones)
        v_odd_ref[rows, :] = jnp.where(even_lanes, ones, v)

    low_half = lax.broadcasted_iota(jnp.int32, (Q_BLOCK, LANES), 1) < HEAD_DIM
    units = [(sb, p, par) for sb in range(n_qblocks) for p in range(N_HEADS // 2)
             for par in range(2)]

    def scores(unit):
        sb, p, par = unit
        r0, cs = sb * Q_BLOCK, slice(p * LANES, (p + 1) * LANES)
        qp = q_ref[r0:r0 + Q_BLOCK, cs]
        qm = jnp.where(low_half if par == 0 else jnp.logical_not(low_half), qp, jnp.zeros_like(qp))
        nt = (((1,), (1,)), ((), ()))
        s = jnp.concatenate(
            [lax.dot_general(qm, kp_ref[r0:TM, cs], nt, preferred_element_type=F32),
             lax.dot_general(qm, kc_ref[0:r0 + Q_BLOCK, cs], nt, preferred_element_type=F32)],
            axis=1)
        return s + bias_ref[jnp.where(is_first, 1 + sb, 0), 2 * p + par]

    pending = [scores(u) for u in units[:SCORE_LOOKAHEAD]]
    outs = []
    for n, (sb, p, par) in enumerate(units):
        r0, cs = sb * Q_BLOCK, slice(p * LANES, (p + 1) * LANES)
        s = pending.pop(0)
        e = jnp.exp2(s - jnp.max(s, axis=-1, keepdims=True))
        if n + SCORE_LOOKAHEAD < len(units):
            pending.append(scores(units[n + SCORE_LOOKAHEAD]))
        v_ref = v_even_ref if par == 0 else v_odd_ref
        o = jnp.dot(e.astype(BF16), v_ref[r0:r0 + KEY_BLOCK, cs], preferred_element_type=F32)
        outs.append(o / pltpu.roll(o, HEAD_DIM, axis=1))
        if par == 1:
            o_ref[r0:r0 + Q_BLOCK, cs] = jnp.where(low_half, outs[-2], outs[-1]).astype(BF16)


def _mix_ffn2_kernel(tiles_per_seq, x1_ref, up_ref, halo_ref, attn_ref, nm_ref, wg_ref,
                     bg_ref, poolw_ref, pscale_ref, wbp_ref, wba_ref, wout_ref,
                     n2_ref, wg2_ref, wu2_ref, wd_ref, nf_ref,
                     out_ref, hb_ref, act_ref, ext_ref, lvl_a_ref, lvl_b_ref, pm_ref, mg_ref, ga_ref):
    i = pl.program_id(0)
    tile_in_seq = i % tiles_per_seq
    x1 = x1_ref[...]
    hb_ref[...], rscale = _norm_split(x1, nm_ref[...])

    base = POOL_PAD + HALO
    keep_halo = jnp.where(tile_in_seq == 0, 0.0, 1.0).astype(F32)
    lvl_refs = (lvl_a_ref, lvl_b_ref)
    for ref in (ext_ref,) + lvl_refs:
        ref[0:POOL_PAD, :] = jnp.zeros((POOL_PAD, POOL_WIDTH), F32)
    ext_ref[POOL_PAD:base, :] = halo_ref[...] * keep_halo
    ext_ref[base:base + TM, :] = up_ref[...]
    pos_top = tile_in_seq * TM + lax.broadcasted_iota(jnp.int32, (HALO, 1), 0)

    def pool_group(g):
        w = POOL_WINDOWS[g]
        cs = slice(g * POOL_GROUP, (g + 1) * POOL_GROUP)
        src, shift, level = ext_ref, 1, 0
        while 2 * shift < w:
            lvl_ref = lvl_refs[level % 2]
            lvl_ref[POOL_PAD:base + TM, cs] = (src[POOL_PAD:base + TM, cs]
                                              + src[POOL_PAD - shift:base + TM - shift, cs])
            src, shift, level = lvl_ref, 2 * shift, level + 1
        sums = src[base:base + TM, cs] + src[base - shift:base + TM - shift, cs]
        inv_top = 1.0 / jnp.minimum(pos_top + 1, w).astype(F32)
        mean = jnp.concatenate([sums[:HALO] * inv_top, sums[HALO:] * (1.0 / w)], axis=0)
        mixed = (mean - ext_ref[base:base + TM, cs]).astype(BF16)
        y = jnp.dot(mixed, poolw_ref[g], preferred_element_type=F32)
        pm_ref[:, cs] = (y * pscale_ref[:, cs]).astype(BF16)

    half = D_MODEL // 2

    def gate(cols):
        z = jnp.dot(hb_ref[...], wg_ref[:, cols], preferred_element_type=F32)
        return _sigmoid(_row_scale(z, rscale) + bg_ref[:, cols])

    for c in range(2):
        cs = slice(c * half, (c + 1) * half)
        y_attn = jnp.dot(attn_ref[...], wba_ref[:, cs], preferred_element_type=F32)
        ga_ref[:, cs] = gate(slice(D_MODEL + c * half, D_MODEL + (c + 1) * half)) * y_attn
    g0 = [gate(slice(0, half))]
    for g in (3, 0, 1, 2):
        pool_group(g)

    for c in range(2):
        cs = slice(c * half, (c + 1) * half)
        if c == len(g0):
            g0.append(gate(cs))
        y_pool = jnp.dot(pm_ref[...], wbp_ref[:, cs], preferred_element_type=F32)
        mg_ref[:, cs] = (g0[c] * y_pool + ga_ref[:, cs]).astype(BF16)
    x2 = x1 + jnp.dot(mg_ref[...], wout_ref[...], preferred_element_type=F32)

    hb_ref[...], rscale2 = _norm_split(x2, n2_ref[...])
    _swiglu_act(hb_ref, rscale2, wg2_ref, wu2_ref, act_ref)
    hm = TM // 2
    for rows in (slice(0, hm), slice(hm, TM)):
        y = jnp.dot(act_ref[rows, :], wd_ref[...], preferred_element_type=F32)
        out_ref[rows, :] = _rmsnorm(x2[rows] + 0.5 * y, nf_ref[...])


def _resident(shape):
    return pl.BlockSpec(shape, lambda i: (0,) * len(shape), pipeline_mode=pl.Buffered(1))


def _toeplitz(w):
    n_heads, width = w.shape
    flat = jnp.tile(jnp.pad(w, ((0, 0), (0, 1))), (1, CHUNK))
    rows = flat[:, :CHUNK * width].reshape(n_heads, CHUNK, width)
    return rows[:, :, CHUNK - 1:]


def _bias_pieces(rel_bias):
    assert MAX_REL == CHUNK
    rb = rel_bias.astype(F32) * LOG2E
    n_heads = rb.shape[0]
    far = rb[:, 2 * MAX_REL]
    t_own = _toeplitz(rb[:, 1:2 * MAX_REL][:, ::-1])
    t_prev = _toeplitz(jnp.concatenate(
        [jnp.broadcast_to(far[:, None], (n_heads, CHUNK)), rb[:, MAX_REL + 1:2 * MAX_REL][:, ::-1]],
        axis=1))
    full = (n_heads, CHUNK, LANES)
    return jnp.stack([jnp.full(full, MASK_VALUE, F32),
                      jnp.broadcast_to(far[:, None, None], full),
                      jnp.concatenate([t_prev, t_prev], axis=2),
                      jnp.concatenate([t_own, t_own], axis=2)], axis=1)


def kernel(x, ffn1_norm, ffn1_w_gate, ffn1_w_up, ffn1_w_down, mix_norm, w_in, pool_w, pool_scale,
           rel_bias, w_branch_pool, w_branch_attn, w_gate, b_gate, w_out, ffn2_norm, ffn2_w_gate,
           ffn2_w_up, ffn2_w_down, final_norm):
    b, s, d = x.shape
    assert d == D_MODEL and s % TM == 0 and ffn1_norm.shape[0] == 1
    n_tok = b * s
    n_tiles = n_tok // TM
    assert n_tiles == N_TILES
    tiles_per_seq = s // TM
    xf = x.reshape(n_tok, d)

    row = lambda v: v.reshape(1, -1).astype(F32)
    cparams = pltpu.CompilerParams(dimension_semantics=("arbitrary",),
                                   vmem_limit_bytes=VMEM_LIMIT)
    tile = lambda width: pl.BlockSpec((TM, width), lambda i: (i, 0))

    tok = lambda i: jnp.maximum(i - N_CAST, 0)
    tile1 = lambda width: pl.BlockSpec((TM, width), lambda i: (tok(i), 0))
    staged = lambda w: pl.BlockSpec((w.shape[0] // N_CAST, w.shape[1]),
                                    lambda i: (jnp.minimum(i, N_CAST - 1), 0))
    own_w = [ffn1_w_gate[0], ffn1_w_up[0], ffn1_w_down[0], w_in[0]]
    side_w = [w_gate[0], w_branch_pool[0], w_branch_attn[0], w_out[0], ffn2_w_gate[0],
              ffn2_w_up[0], ffn2_w_down[0], pool_w[0].reshape(POOL_WIDTH, POOL_GROUP)]
    def side_specs():
        specs = []
        for w, (rows, steps) in zip(side_w, SIDE_ROWS):
            assert w.shape[0] == rows
            specs.append(pl.BlockSpec(
                (rows // steps, w.shape[1]),
                functools.partial(lambda steps, i: (jnp.minimum(tok(i), steps - 1), 0), steps)))
        return specs

    res = pl.pallas_call(
        _ffn1_inproj_kernel,
        grid=(N_CAST + n_tiles,),
        in_specs=[tile1(D_MODEL), _resident((1, D_MODEL)), staged(own_w[0]), staged(own_w[1]),
                  staged(own_w[2]), _resident((1, D_MODEL)), staged(own_w[3])] + side_specs(),
        out_specs=[tile1(D_MODEL), tile1(POOL_WIDTH), tile1(3 * ATTN_WIDTH)] + side_specs(),
        out_shape=[jax.ShapeDtypeStruct((n_tok, D_MODEL), F32),
                   jax.ShapeDtypeStruct((n_tok, POOL_WIDTH), F32),
                   jax.ShapeDtypeStruct((n_tok, 3 * ATTN_WIDTH), BF16)]
                  + [jax.ShapeDtypeStruct(w.shape, BF16) for w in side_w],
        scratch_shapes=[pltpu.VMEM(w.shape, BF16) for w in own_w]
                       + [pltpu.VMEM((TM, D_MODEL), BF16), pltpu.VMEM((TM, D_FF), BF16)],
        compiler_params=cparams,
        name="ffn1_inproj",
    )(xf, row(ffn1_norm), own_w[0], own_w[1], own_w[2], row(mix_norm), own_w[3], *side_w)
    x1, up, qkv = res[:3]
    wgate_b, wbp_b, wba_b, wout_b, wg2_b, wu2_b, wd2_b, poolw_b = res[3:]

    prev = lambda i: jnp.maximum(i - 1, 0)
    attn = pl.pallas_call(
        functools.partial(_attn_kernel, tiles_per_seq),
        grid=(n_tiles,),
        in_specs=[pl.BlockSpec((TM, ATTN_WIDTH), lambda i: (i, 0)),
                  pl.BlockSpec((TM, ATTN_WIDTH), lambda i: (prev(i), 1)),
                  pl.BlockSpec((TM, ATTN_WIDTH), lambda i: (i, 1)),
                  pl.BlockSpec((TM, ATTN_WIDTH), lambda i: (prev(i), 2)),
                  pl.BlockSpec((TM, ATTN_WIDTH), lambda i: (i, 2)),
                  _resident((N_HEADS, 4, CHUNK, LANES))],
        out_specs=tile(ATTN_WIDTH),
        out_shape=jax.ShapeDtypeStruct((n_tok, ATTN_WIDTH), BF16),
        scratch_shapes=[pltpu.VMEM((1 + TM // Q_BLOCK, N_HEADS, Q_BLOCK, KEY_BLOCK), F32),
                        pltpu.VMEM((2 * TM, ATTN_WIDTH), BF16),
                        pltpu.VMEM((2 * TM, ATTN_WIDTH), BF16)],
        compiler_params=cparams,
        name="chunk_attn",
    )(qkv, qkv, qkv, qkv, qkv, _bias_pieces(rel_bias[0]))

    halo_blocks = TM // HALO
    out = pl.pallas_call(
        functools.partial(_mix_ffn2_kernel, tiles_per_seq),
        grid=(n_tiles,),
        in_specs=[tile(D_MODEL), tile(POOL_WIDTH),
                  pl.BlockSpec((HALO, POOL_WIDTH),
                               lambda i: (jnp.maximum(i * halo_blocks - 1, 0), 0)),
                  tile(ATTN_WIDTH), _resident((1, D_MODEL)),
                  _resident((D_MODEL, 2 * D_MODEL)), _resident((1, 2 * D_MODEL)),
                  _resident((len(POOL_WINDOWS), POOL_GROUP, POOL_GROUP)),
                  _resident((1, POOL_WIDTH)), _resident((POOL_WIDTH, D_MODEL)),
                  _resident((ATTN_WIDTH, D_MODEL)), _resident((D_MODEL, D_MODEL)),
                  _resident((1, D_MODEL)), _resident((D_MODEL, D_FF)),
                  _resident((D_MODEL, D_FF)), _resident((D_FF, D_MODEL)),
                  _resident((1, D_MODEL))],
        out_specs=tile(D_MODEL),
        out_shape=jax.ShapeDtypeStruct((n_tok, D_MODEL), F32),
        scratch_shapes=[pltpu.VMEM((TM, D_MODEL), BF16), pltpu.VMEM((TM, D_FF), BF16),
                        pltpu.VMEM((POOL_PAD + HALO + TM, POOL_WIDTH), F32),
                        pltpu.VMEM((POOL_PAD + HALO + TM, POOL_WIDTH), F32),
                        pltpu.VMEM((POOL_PAD + HALO + TM, POOL_WIDTH), F32),
                        pltpu.VMEM((TM, POOL_WIDTH), BF16), pltpu.VMEM((TM, D_MODEL), BF16),
                        pltpu.VMEM((TM, D_MODEL), F32)],
        compiler_params=cparams,
        name="mix_ffn2",
    )(x1, up, up, attn, row(mix_norm), wgate_b, row(b_gate),
      poolw_b.reshape(len(POOL_WINDOWS), POOL_GROUP, POOL_GROUP), row(pool_scale), wbp_b, wba_b,
      wout_b, row(ffn2_norm), wg2_b, wu2_b, wd2_b, row(final_norm))
    return out.reshape(b, s, d)
```

```python
import functools

import jax
import jax.numpy as jnp
from jax import lax
from jax.experimental import pallas as pl
from jax.experimental.pallas import tpu as pltpu

D_MODEL = 1024
D_FF = 2816
CHUNK = 64
LEFT_CHUNKS = 8
POOL_WIDTH = 512
POOL_WINDOWS = (2, 4, 8, 16)
POOL_GROUP = 128
N_HEADS = 8
HEAD_DIM = 64
ATTN_WIDTH = 512
MAX_REL = 64
IN_WIDTH = POOL_WIDTH + 3 * ATTN_WIDTH
EPS = 1e-6

LANES = 128
TM = 512
FF_CHUNK = 256
Q_BLOCK = 256
KEY_BLOCK = Q_BLOCK + LEFT_CHUNKS * CHUNK
SCORE_LOOKAHEAD = 2
HALO = 16
POOL_PAD = 8
N_CAST = 8
N_TILES = 32
SIDE_ROWS = ((D_MODEL, N_TILES, 0.5),
             (POOL_WIDTH, N_TILES, 0.5),
             (ATTN_WIDTH, N_TILES, 0.5),
             (D_MODEL, N_TILES, 1.0),
             (D_MODEL, N_TILES, 1.0),
             (D_MODEL, N_TILES, 1.0),
             (D_FF, N_TILES // 2, 0.5),
             (POOL_WIDTH, N_TILES, 1.0))
MASK_VALUE = -1e30
LOG2E = 1.4426950408889634
Q_SCALE = HEAD_DIM ** -0.5 * LOG2E
VMEM_LIMIT = 56 * 1024 * 1024

F32 = jnp.float32
BF16 = jnp.bfloat16


def _rmsnorm(x, g):
    ms = jnp.mean(x * x, axis=-1, keepdims=True)
    return x * lax.rsqrt(ms + EPS) * g


def _norm_split(x, g):
    rscale = lax.rsqrt(jnp.mean(x * x, axis=-1, keepdims=True) + EPS)
    return (x * g).astype(BF16), jnp.broadcast_to(rscale, (x.shape[0], LANES))


def _row_scale(y, rscale):
    return jnp.concatenate([y[:, lo:lo + LANES] * rscale for lo in range(0, y.shape[1], LANES)],
                           axis=1)


def _swiglu(hb_ref, rscale, wg_ref, wu_ref, wd_ref, act_ref):
    _swiglu_act(hb_ref, rscale, wg_ref, wu_ref, act_ref)
    return jnp.dot(act_ref[...], wd_ref[...], preferred_element_type=F32)


def _swiglu_act(hb_ref, rscale, wg_ref, wu_ref, act_ref):
    half_rscale = 0.5 * rscale
    for lo in range(0, D_FF, FF_CHUNK):
        cs = slice(lo, min(lo + FF_CHUNK, D_FF))
        hg = _row_scale(jnp.dot(hb_ref[...], wg_ref[:, cs], preferred_element_type=F32),
                        half_rscale)
        up = _row_scale(jnp.dot(hb_ref[...], wu_ref[:, cs], preferred_element_type=F32), rscale)
        act_ref[:, cs] = ((hg + hg * jnp.tanh(hg)) * up).astype(BF16)


def _ffn1_inproj_kernel(x_ref, n1_ref, wg_blk, wu_blk, wd_blk, nm_ref, win_blk, *rest):
    n_side = len(SIDE_ROWS)
    side_in = rest[:n_side]
    x1_ref, up_ref, qkv_ref = rest[n_side:n_side + 3]
    side_out = rest[n_side + 3:2 * n_side + 3]
    wg_ref, wu_ref, wd_ref, win_ref, hb_ref, act_ref = rest[2 * n_side + 3:]
    i = pl.program_id(0)

    @pl.when(i < N_CAST)
    def _stage_weights():
        for blk, dst, scale in ((wg_blk, wg_ref, 1.0), (wu_blk, wu_ref, 1.0),
                                (wd_blk, wd_ref, 0.5), (win_blk, win_ref, 1.0)):
            rows = blk.shape[0]
            w = blk[...] if scale == 1.0 else blk[...] * scale
            dst[pl.ds(pl.multiple_of(i * rows, rows), rows), :] = w.astype(BF16)

    @pl.when(i >= N_CAST)
    def _token_tile():
        x = x_ref[...]
        hb_ref[...], rscale = _norm_split(x, n1_ref[...])
        x1 = x + _swiglu(hb_ref, rscale, wg_ref, wu_ref, wd_ref, act_ref)
        x1_ref[...] = x1
        hb_ref[...], rscale = _norm_split(x1, nm_ref[...])
        up_ref[...] = _row_scale(
            jnp.dot(hb_ref[...], win_ref[:, :POOL_WIDTH], preferred_element_type=F32), rscale)
        for c in range(3):
            lo = POOL_WIDTH + c * ATTN_WIDTH
            y = jnp.dot(hb_ref[...], win_ref[:, lo:lo + ATTN_WIDTH], preferred_element_type=F32)
            y = _row_scale(y, rscale * Q_SCALE if c == 0 else rscale)
            qkv_ref[:, c * ATTN_WIDTH:(c + 1) * ATTN_WIDTH] = y.astype(BF16)
        for src, dst, (_, steps, scale) in zip(side_in, side_out, SIDE_ROWS):
            @pl.when(i - N_CAST < steps)
            def _cast_block(src=src, dst=dst, scale=scale):
                w = src[...] if scale == 1.0 else src[...] * scale
                dst[...] = w.astype(BF16)


def _band_kind(dchunk):
    if dchunk < 0 or dchunk > LEFT_CHUNKS:
        return 0
    if dchunk < LEFT_CHUNKS - 1:
        return 1
    return 2 if dchunk == LEFT_CHUNKS - 1 else 3


def _attn_kernel(tiles_per_seq, q_ref, kp_ref, kc_ref, vp_ref, vc_ref, pieces_ref,
                 o_ref, bias_ref, v_even_ref, v_odd_ref):
    i = pl.program_id(0)
    is_first = (i % tiles_per_seq) == 0
    n_qblocks = TM // Q_BLOCK

    @pl.when(i == 0)
    def _build_bias_tables():
        low = lax.broadcasted_iota(jnp.int32, (CHUNK, LANES), 1) < CHUNK

        def per_head(h, carry):
            for variant in range(1 + n_qblocks):
                n_before = 0 if variant == 0 else (TM - (variant - 1) * Q_BLOCK) // LANES
                for qc in range(Q_BLOCK // CHUNK):
                    for kp in range(KEY_BLOCK // LANES):
                        ka, kb = _band_kind(2 * kp - qc), _band_kind(2 * kp + 1 - qc)
                        if kp < n_before:
                            ka = kb = 0
                        blk = pieces_ref[h, ka]
                        if kb != ka:
                            blk = jnp.where(low, blk, pieces_ref[h, kb])
                        bias_ref[variant, h, qc * CHUNK:(qc + 1) * CHUNK,
                                 kp * LANES:(kp + 1) * LANES] = blk
            return carry

        lax.fori_loop(0, N_HEADS, per_head, 0)

    even_lanes = (lax.broadcasted_iota(jnp.int32, (TM, ATTN_WIDTH), 1) % LANES) < HEAD_DIM
    ones = jnp.ones((TM, ATTN_WIDTH), BF16)
    for src, rows in ((vp_ref, slice(0, TM)), (vc_ref, slice(TM, 2 * TM))):
        v = src[...]
        v_even_ref[rows, :] = jnp.where(even_lanes, v, ones)
        v_odd_ref[rows, :] = jnp.where(even_lanes, ones, v)

    low_half = lax.broadcasted_iota(jnp.int32, (Q_BLOCK, LANES), 1) < HEAD_DIM
    units = [(sb, p, par) for sb in range(n_qblocks) for p in range(N_HEADS // 2)
             for par in range(2)]

    def scores(unit):
        sb, p, par = unit
        r0, cs = sb * Q_BLOCK, slice(p * LANES, (p + 1) * LANES)
        qp = q_ref[r0:r0 + Q_BLOCK, cs]
        qm = jnp.where(low_half if par == 0 else jnp.logical_not(low_half), qp, jnp.zeros_like(qp))
        nt = (((1,), (1,)), ((), ()))
        s = jnp.concatenate(
            [lax.dot_general(qm, kp_ref[r0:TM, cs], nt, preferred_element_type=F32),
             lax.dot_general(qm, kc_ref[0:r0 + Q_BLOCK, cs], nt, preferred_element_type=F32)],
            axis=1)
        return s + bias_ref[jnp.where(is_first, 1 + sb, 0), 2 * p + par]

    pending = [scores(u) for u in units[:SCORE_LOOKAHEAD]]
    outs = []
    for n, (sb, p, par) in enumerate(units):
        r0, cs = sb * Q_BLOCK, slice(p * LANES, (p + 1) * LANES)
        s = pending.pop(0)
        e = jnp.exp2(s - jnp.max(s, axis=-1, keepdims=True))
        if n + SCORE_LOOKAHEAD < len(units):
            pending.append(scores(units[n + SCORE_LOOKAHEAD]))
        v_ref = v_even_ref if par == 0 else v_odd_ref
        o = jnp.dot(e.astype(BF16), v_ref[r0:r0 + KEY_BLOCK, cs], preferred_element_type=F32)
        outs.append(o / pltpu.roll(o, HEAD_DIM, axis=1))
        if par == 1:
            o_ref[r0:r0 + Q_BLOCK, cs] = jnp.where(low_half, outs[-2], outs[-1]).astype(BF16)


def _mix_ffn2_kernel(tiles_per_seq, x1_ref, up_ref, halo_ref, attn_ref, nm_ref, wg_ref,
                     bg_ref, poolw_ref, pscale_ref, wbp_ref, wba_ref, wout_ref,
                     n2_ref, wg2_ref, wu2_ref, wd_ref, nf_ref,
                     out_ref, hb_ref, act_ref, ext_ref, lvl_a_ref, lvl_b_ref, pm_ref, mg_ref, ga_ref):
    i = pl.program_id(0)
    tile_in_seq = i % tiles_per_seq
    x1 = x1_ref[...]
    hb_ref[...], rscale = _norm_split(x1, nm_ref[...])

    base = POOL_PAD + HALO
    keep_halo = jnp.where(tile_in_seq == 0, 0.0, 1.0).astype(F32)
    lvl_refs = (lvl_a_ref, lvl_b_ref)
    for ref in (ext_ref,) + lvl_refs:
        ref[0:POOL_PAD, :] = jnp.zeros((POOL_PAD, POOL_WIDTH), F32)
    ext_ref[POOL_PAD:base, :] = halo_ref[...] * keep_halo
    ext_ref[base:base + TM, :] = up_ref[...]
    pos_top = tile_in_seq * TM + lax.broadcasted_iota(jnp.int32, (HALO, 1), 0)

    def pool_group(g):
        w = POOL_WINDOWS[g]
        cs = slice(g * POOL_GROUP, (g + 1) * POOL_GROUP)
        src, shift, level = ext_ref, 1, 0
        while 2 * shift < w:
            lvl_ref = lvl_refs[level % 2]
            lvl_ref[POOL_PAD:base + TM, cs] = (src[POOL_PAD:base + TM, cs]
                                              + src[POOL_PAD - shift:base + TM - shift, cs])
            src, shift, level = lvl_ref, 2 * shift, level + 1
        sums = src[base:base + TM, cs] + src[base - shift:base + TM - shift, cs]
        inv_top = 1.0 / jnp.minimum(pos_top + 1, w).astype(F32)
        mean = jnp.concatenate([sums[:HALO] * inv_top, sums[HALO:] * (1.0 / w)], axis=0)
        mixed = (mean - ext_ref[base:base + TM, cs]).astype(BF16)
        y = jnp.dot(mixed, poolw_ref[g], preferred_element_type=F32)
        pm_ref[:, cs] = (y * pscale_ref[:, cs]).astype(BF16)

    half = D_MODEL // 2

    def gate_tanh(cols):
        z_half = jnp.dot(hb_ref[...], wg_ref[:, cols], preferred_element_type=F32)
        return jnp.tanh(_row_scale(z_half, rscale) + 0.5 * bg_ref[:, cols])

    for c in range(2):
        cs = slice(c * half, (c + 1) * half)
        y_half = jnp.dot(attn_ref[...], wba_ref[:, cs], preferred_element_type=F32)
        t1 = gate_tanh(slice(D_MODEL + c * half, D_MODEL + (c + 1) * half))
        ga_ref[:, cs] = y_half + t1 * y_half
    g0 = [gate_tanh(slice(0, half))]
    for g in (3, 0, 1, 2):
        pool_group(g)

    for c in range(2):
        cs = slice(c * half, (c + 1) * half)
        if c == len(g0):
            g0.append(gate_tanh(cs))
        y_half = jnp.dot(pm_ref[...], wbp_ref[:, cs], preferred_element_type=F32)
        mg_ref[:, cs] = (y_half + g0[c] * y_half + ga_ref[:, cs]).astype(BF16)
    x2 = x1 + jnp.dot(mg_ref[...], wout_ref[...], preferred_element_type=F32)

    hb_ref[...], rscale2 = _norm_split(x2, n2_ref[...])
    _swiglu_act(hb_ref, rscale2, wg2_ref, wu2_ref, act_ref)
    hm = TM // 2
    for rows in (slice(0, hm), slice(hm, TM)):
        y = jnp.dot(act_ref[rows, :], wd_ref[...], preferred_element_type=F32)
        out_ref[rows, :] = _rmsnorm(x2[rows] + y, nf_ref[...])


def _resident(shape):
    return pl.BlockSpec(shape, lambda i: (0,) * len(shape), pipeline_mode=pl.Buffered(1))


def _toeplitz(w):
    n_heads, width = w.shape
    flat = jnp.tile(jnp.pad(w, ((0, 0), (0, 1))), (1, CHUNK))
    rows = flat[:, :CHUNK * width].reshape(n_heads, CHUNK, width)
    return rows[:, :, CHUNK - 1:]


def _bias_pieces(rel_bias):
    assert MAX_REL == CHUNK
    rb = rel_bias.astype(F32) * LOG2E
    n_heads = rb.shape[0]
    far = rb[:, 2 * MAX_REL]
    t_own = _toeplitz(rb[:, 1:2 * MAX_REL][:, ::-1])
    t_prev = _toeplitz(jnp.concatenate(
        [jnp.broadcast_to(far[:, None], (n_heads, CHUNK)), rb[:, MAX_REL + 1:2 * MAX_REL][:, ::-1]],
        axis=1))
    full = (n_heads, CHUNK, LANES)
    return jnp.stack([jnp.full(full, MASK_VALUE, F32),
                      jnp.broadcast_to(far[:, None, None], full),
                      jnp.concatenate([t_prev, t_prev], axis=2),
                      jnp.concatenate([t_own, t_own], axis=2)], axis=1)


def kernel(x, ffn1_norm, ffn1_w_gate, ffn1_w_up, ffn1_w_down, mix_norm, w_in, pool_w, pool_scale,
           rel_bias, w_branch_pool, w_branch_attn, w_gate, b_gate, w_out, ffn2_norm, ffn2_w_gate,
           ffn2_w_up, ffn2_w_down, final_norm):
    b, s, d = x.shape
    assert d == D_MODEL and s % TM == 0 and ffn1_norm.shape[0] == 1
    n_tok = b * s
    n_tiles = n_tok // TM
    assert n_tiles == N_TILES
    tiles_per_seq = s // TM
    xf = x.reshape(n_tok, d)

    row = lambda v: v.reshape(1, -1).astype(F32)
    cparams = pltpu.CompilerParams(dimension_semantics=("arbitrary",),
                                   vmem_limit_bytes=VMEM_LIMIT)
    tile = lambda width: pl.BlockSpec((TM, width), lambda i: (i, 0))

    tok = lambda i: jnp.maximum(i - N_CAST, 0)
    tile1 = lambda width: pl.BlockSpec((TM, width), lambda i: (tok(i), 0))
    staged = lambda w: pl.BlockSpec((w.shape[0] // N_CAST, w.shape[1]),
                                    lambda i: (jnp.minimum(i, N_CAST - 1), 0))
    own_w = [ffn1_w_gate[0], ffn1_w_up[0], ffn1_w_down[0], w_in[0]]
    side_w = [w_gate[0], w_branch_pool[0], w_branch_attn[0], w_out[0], ffn2_w_gate[0],
              ffn2_w_up[0], ffn2_w_down[0], pool_w[0].reshape(POOL_WIDTH, POOL_GROUP)]
    def side_specs():
        specs = []
        for w, (rows, steps, _) in zip(side_w, SIDE_ROWS):
            assert w.shape[0] == rows
            specs.append(pl.BlockSpec(
                (rows // steps, w.shape[1]),
                functools.partial(lambda steps, i: (jnp.minimum(tok(i), steps - 1), 0), steps)))
        return specs

    res = pl.pallas_call(
        _ffn1_inproj_kernel,
        grid=(N_CAST + n_tiles,),
        in_specs=[tile1(D_MODEL), _resident((1, D_MODEL)), staged(own_w[0]), staged(own_w[1]),
                  staged(own_w[2]), _resident((1, D_MODEL)), staged(own_w[3])] + side_specs(),
        out_specs=[tile1(D_MODEL), tile1(POOL_WIDTH), tile1(3 * ATTN_WIDTH)] + side_specs(),
        out_shape=[jax.ShapeDtypeStruct((n_tok, D_MODEL), F32),
                   jax.ShapeDtypeStruct((n_tok, POOL_WIDTH), F32),
                   jax.ShapeDtypeStruct((n_tok, 3 * ATTN_WIDTH), BF16)]
                  + [jax.ShapeDtypeStruct(w.shape, BF16) for w in side_w],
        scratch_shapes=[pltpu.VMEM(w.shape, BF16) for w in own_w]
                       + [pltpu.VMEM((TM, D_MODEL), BF16), pltpu.VMEM((TM, D_FF), BF16)],
        compiler_params=cparams,
        name="ffn1_inproj",
    )(xf, row(ffn1_norm), own_w[0], own_w[1], own_w[2], row(mix_norm), own_w[3], *side_w)
    x1, up, qkv = res[:3]
    wgate_b, wbp_b, wba_b, wout_b, wg2_b, wu2_b, wd2_b, poolw_b = res[3:]

    prev = lambda i: jnp.maximum(i - 1, 0)
    attn = pl.pallas_call(
        functools.partial(_attn_kernel, tiles_per_seq),
        grid=(n_tiles,),
        in_specs=[pl.BlockSpec((TM, ATTN_WIDTH), lambda i: (i, 0)),
                  pl.BlockSpec((TM, ATTN_WIDTH), lambda i: (prev(i), 1)),
                  pl.BlockSpec((TM, ATTN_WIDTH), lambda i: (i, 1)),
                  pl.BlockSpec((TM, ATTN_WIDTH), lambda i: (prev(i), 2)),
                  pl.BlockSpec((TM, ATTN_WIDTH), lambda i: (i, 2)),
                  _resident((N_HEADS, 4, CHUNK, LANES))],
        out_specs=tile(ATTN_WIDTH),
        out_shape=jax.ShapeDtypeStruct((n_tok, ATTN_WIDTH), BF16),
        scratch_shapes=[pltpu.VMEM((1 + TM // Q_BLOCK, N_HEADS, Q_BLOCK, KEY_BLOCK), F32),
                        pltpu.VMEM((2 * TM, ATTN_WIDTH), BF16),
                        pltpu.VMEM((2 * TM, ATTN_WIDTH), BF16)],
        compiler_params=cparams,
        name="chunk_attn",
    )(qkv, qkv, qkv, qkv, qkv, _bias_pieces(rel_bias[0]))

    halo_blocks = TM // HALO
    out = pl.pallas_call(
        functools.partial(_mix_ffn2_kernel, tiles_per_seq),
        grid=(n_tiles,),
        in_specs=[tile(D_MODEL), tile(POOL_WIDTH),
                  pl.BlockSpec((HALO, POOL_WIDTH),
                               lambda i: (jnp.maximum(i * halo_blocks - 1, 0), 0)),
                  tile(ATTN_WIDTH), _resident((1, D_MODEL)),
                  _resident((D_MODEL, 2 * D_MODEL)), _resident((1, 2 * D_MODEL)),
                  _resident((len(POOL_WINDOWS), POOL_GROUP, POOL_GROUP)),
                  _resident((1, POOL_WIDTH)), _resident((POOL_WIDTH, D_MODEL)),
                  _resident((ATTN_WIDTH, D_MODEL)), _resident((D_MODEL, D_MODEL)),
                  _resident((1, D_MODEL)), _resident((D_MODEL, D_FF)),
                  _resident((D_MODEL, D_FF)), _resident((D_FF, D_MODEL)),
                  _resident((1, D_MODEL))],
        out_specs=tile(D_MODEL),
        out_shape=jax.ShapeDtypeStruct((n_tok, D_MODEL), F32),
        scratch_shapes=[pltpu.VMEM((TM, D_MODEL), BF16), pltpu.VMEM((TM, D_FF), BF16),
                        pltpu.VMEM((POOL_PAD + HALO + TM, POOL_WIDTH), F32),
                        pltpu.VMEM((POOL_PAD + HALO + TM, POOL_WIDTH), F32),
                        pltpu.VMEM((POOL_PAD + HALO + TM, POOL_WIDTH), F32),
                        pltpu.VMEM((TM, POOL_WIDTH), BF16), pltpu.VMEM((TM, D_MODEL), BF16),
                        pltpu.VMEM((TM, D_MODEL), F32)],
        compiler_params=cparams,
        name="mix_ffn2",
    )(x1, up, up, attn, row(mix_norm), wgate_b, row(b_gate),
      poolw_b.reshape(len(POOL_WINDOWS), POOL_GROUP, POOL_GROUP), row(pool_scale), wbp_b, wba_b,
      wout_b, row(ffn2_norm), wg2_b, wu2_b, wd2_b, row(final_norm))
    return out.reshape(b, s, d)
```

```python
import functools

import jax
import jax.numpy as jnp
from jax import lax
from jax.experimental import pallas as pl
from jax.experimental.pallas import tpu as pltpu

D_MODEL = 1024
D_FF = 2816
CHUNK = 64
LEFT_CHUNKS = 8
POOL_WIDTH = 512
POOL_WINDOWS = (2, 4, 8, 16)
POOL_GROUP = 128
N_HEADS = 8
HEAD_DIM = 64
ATTN_WIDTH = 512
MAX_REL = 64
IN_WIDTH = POOL_WIDTH + 3 * ATTN_WIDTH
EPS = 1e-6

LANES = 128
TM = 512
FF_CHUNK = 256
COL_CHUNK = 256
Q_BLOCK = 256
KEY_BLOCK = Q_BLOCK + LEFT_CHUNKS * CHUNK
SCORE_LOOKAHEAD = 2
HALO = 16
POOL_PAD = 8
N_CAST = 8
N_TILES = 32
SIDE_ROWS = ((D_MODEL, N_TILES, 0.5),
             (POOL_WIDTH, N_TILES, 0.5),
             (ATTN_WIDTH, N_TILES, 0.5),
             (D_MODEL, N_TILES, 1.0),
             (D_MODEL, N_TILES, 1.0),
             (D_MODEL, N_TILES, 1.0),
             (D_FF, N_TILES // 2, 0.5),
             (POOL_WIDTH, N_TILES, 1.0))
MASK_VALUE = -1e30
LOG2E = 1.4426950408889634
Q_SCALE = HEAD_DIM ** -0.5 * LOG2E
VMEM_LIMIT = 56 * 1024 * 1024

F32 = jnp.float32
BF16 = jnp.bfloat16


def _rmsnorm(x, g):
    ms = jnp.mean(x * x, axis=-1, keepdims=True)
    return x * lax.rsqrt(ms + EPS) * g


def _norm_split(x, g):
    rscale = lax.rsqrt(jnp.mean(x * x, axis=-1, keepdims=True) + EPS)
    return (x * g).astype(BF16), jnp.broadcast_to(rscale, (x.shape[0], LANES))


def _row_scale(y, rscale):
    return jnp.concatenate([y[:, lo:lo + LANES] * rscale for lo in range(0, y.shape[1], LANES)],
                           axis=1)


def _swiglu(hb_ref, rscale, wg_ref, wu_ref, wd_ref, act_ref):
    _swiglu_act(hb_ref, rscale, wg_ref, wu_ref, act_ref)
    return jnp.dot(act_ref[...], wd_ref[...], preferred_element_type=F32)


def _swiglu_act(hb_ref, rscale, wg_ref, wu_ref, act_ref):
    half_rscale = 0.5 * rscale
    for lo in range(0, D_FF, FF_CHUNK):
        cs = slice(lo, min(lo + FF_CHUNK, D_FF))
        hg = _row_scale(jnp.dot(hb_ref[...], wg_ref[:, cs], preferred_element_type=F32),
                        half_rscale)
        up = _row_scale(jnp.dot(hb_ref[...], wu_ref[:, cs], preferred_element_type=F32), rscale)
        act_ref[:, cs] = ((hg + hg * jnp.tanh(hg)) * up).astype(BF16)


def _ffn1_inproj_kernel(x_ref, n1_ref, wg_blk, wu_blk, wd_blk, nm_ref, win_blk, *rest):
    n_side = len(SIDE_ROWS)
    side_in = rest[:n_side]
    x1_ref, up_ref, qkv_ref = rest[n_side:n_side + 3]
    side_out = rest[n_side + 3:2 * n_side + 3]
    wg_ref, wu_ref, wd_ref, win_ref, hb_ref, act_ref = rest[2 * n_side + 3:]
    i = pl.program_id(0)

    @pl.when(i < N_CAST)
    def _stage_weights():
        for blk, dst, scale in ((wg_blk, wg_ref, 1.0), (wu_blk, wu_ref, 1.0),
                                (wd_blk, wd_ref, 0.5), (win_blk, win_ref, 1.0)):
            rows = blk.shape[0]
            w = blk[...] if scale == 1.0 else blk[...] * scale
            dst[pl.ds(pl.multiple_of(i * rows, rows), rows), :] = w.astype(BF16)

    @pl.when(i >= N_CAST)
    def _token_tile():
        hb_ref[...], rscale = _norm_split(x_ref[...], n1_ref[...])
        _swiglu_act(hb_ref, rscale, wg_ref, wu_ref, act_ref)
        for lo in range(0, D_MODEL, COL_CHUNK):
            cs = slice(lo, lo + COL_CHUNK)
            x1_ref[:, cs] = x_ref[:, cs] + jnp.dot(act_ref[...], wd_ref[:, cs],
                                                   preferred_element_type=F32)
        hb_ref[...], rscale = _norm_split(x1_ref[...], nm_ref[...])
        q_rscale = rscale * Q_SCALE
        for lo in range(0, IN_WIDTH, COL_CHUNK):
            y = jnp.dot(hb_ref[...], win_ref[:, lo:lo + COL_CHUNK], preferred_element_type=F32)
            if lo < POOL_WIDTH:
                up_ref[:, lo:lo + COL_CHUNK] = _row_scale(y, rscale)
            else:
                is_q = lo < POOL_WIDTH + ATTN_WIDTH
                qkv_ref[:, lo - POOL_WIDTH:lo - POOL_WIDTH + COL_CHUNK] = _row_scale(
                    y, q_rscale if is_q else rscale).astype(BF16)
        for src, dst, (_, steps, scale) in zip(side_in, side_out, SIDE_ROWS):
            @pl.when(i - N_CAST < steps)
            def _cast_block(src=src, dst=dst, scale=scale):
                w = src[...] if scale == 1.0 else src[...] * scale
                dst[...] = w.astype(BF16)


def _band_kind(dchunk):
    if dchunk < 0 or dchunk > LEFT_CHUNKS:
        return 0
    if dchunk < LEFT_CHUNKS - 1:
        return 1
    return 2 if dchunk == LEFT_CHUNKS - 1 else 3


def _attn_kernel(tiles_per_seq, q_ref, kp_ref, kc_ref, vp_ref, vc_ref, pieces_ref,
                 o_ref, bias_ref, v_even_ref, v_odd_ref):
    i = pl.program_id(0)
    is_first = (i % tiles_per_seq) == 0
    n_qblocks = TM // Q_BLOCK

    @pl.when(i == 0)
    def _build_bias_tables():
        low = lax.broadcasted_iota(jnp.int32, (CHUNK, LANES), 1) < CHUNK

        def per_head(h, carry):
            for variant in range(1 + n_qblocks):
                n_before = 0 if variant == 0 else (TM - (variant - 1) * Q_BLOCK) // LANES
                for qc in range(Q_BLOCK // CHUNK):
                    for kp in range(KEY_BLOCK // LANES):
                        ka, kb = _band_kind(2 * kp - qc), _band_kind(2 * kp + 1 - qc)
                        if kp < n_before:
                            ka = kb = 0
                        blk = pieces_ref[h, ka]
                        if kb != ka:
                            blk = jnp.where(low, blk, pieces_ref[h, kb])
                        bias_ref[variant, h, qc * CHUNK:(qc + 1) * CHUNK,
                                 kp * LANES:(kp + 1) * LANES] = blk
            return carry

        lax.fori_loop(0, N_HEADS, per_head, 0)

    even_lanes = (lax.broadcasted_iota(jnp.int32, (TM, ATTN_WIDTH), 1) % LANES) < HEAD_DIM
    ones = jnp.ones((TM, ATTN_WIDTH), BF16)
    for src, rows in ((vp_ref, slice(0, TM)), (vc_ref, slice(TM, 2 * TM))):
        v = src[...]
        v_even_ref[rows, :] = jnp.where(even_lanes, v, ones)
        v_odd_ref[rows, :] = jnp.where(even_lanes, ones, v)

    low_half = lax.broadcasted_iota(jnp.int32, (Q_BLOCK, LANES), 1) < HEAD_DIM
    units = [(sb, p, par) for sb in range(n_qblocks) for p in range(N_HEADS // 2)
             for par in range(2)]

    def scores(unit):
        sb, p, par = unit
        r0, cs = sb * Q_BLOCK, slice(p * LANES, (p + 1) * LANES)
        qp = q_ref[r0:r0 + Q_BLOCK, cs]
        qm = jnp.where(low_half if par == 0 else jnp.logical_not(low_half), qp, jnp.zeros_like(qp))
        nt = (((1,), (1,)), ((), ()))
        s = jnp.concatenate(
            [lax.dot_general(qm, kp_ref[r0:TM, cs], nt, preferred_element_type=F32),
             lax.dot_general(qm, kc_ref[0:r0 + Q_BLOCK, cs], nt, preferred_element_type=F32)],
            axis=1)
        return s + bias_ref[jnp.where(is_first, 1 + sb, 0), 2 * p + par]

    pending = [scores(u) for u in units[:SCORE_LOOKAHEAD]]
    outs = []
    for n, (sb, p, par) in enumerate(units):
        r0, cs = sb * Q_BLOCK, slice(p * LANES, (p + 1) * LANES)
        s = pending.pop(0)
        e = jnp.exp2(s - jnp.max(s, axis=-1, keepdims=True))
        if n + SCORE_LOOKAHEAD < len(units):
            pending.append(scores(units[n + SCORE_LOOKAHEAD]))
        v_ref = v_even_ref if par == 0 else v_odd_ref
        o = jnp.dot(e.astype(BF16), v_ref[r0:r0 + KEY_BLOCK, cs], preferred_element_type=F32)
        outs.append(o / pltpu.roll(o, HEAD_DIM, axis=1))
        if par == 1:
            o_ref[r0:r0 + Q_BLOCK, cs] = jnp.where(low_half, outs[-2], outs[-1]).astype(BF16)


def _mix_ffn2_kernel(tiles_per_seq, x1_ref, up_ref, halo_ref, attn_ref, nm_ref, wg_ref,
                     bg_ref, poolw_ref, pscale_ref, wbp_ref, wba_ref, wout_ref,
                     n2_ref, wg2_ref, wu2_ref, wd_ref, nf_ref,
                     out_ref, hb_ref, act_ref, ext_ref, lvl_a_ref, lvl_b_ref, pm_ref, mg_ref, ga_ref):
    i = pl.program_id(0)
    tile_in_seq = i % tiles_per_seq
    hb_ref[...], rscale = _norm_split(x1_ref[...], nm_ref[...])

    base = POOL_PAD + HALO
    keep_halo = jnp.where(tile_in_seq == 0, 0.0, 1.0).astype(F32)
    lvl_refs = (lvl_a_ref, lvl_b_ref)
    for ref in (ext_ref,) + lvl_refs:
        ref[0:POOL_PAD, :] = jnp.zeros((POOL_PAD, POOL_WIDTH), F32)
    ext_ref[POOL_PAD:base, :] = halo_ref[...] * keep_halo
    ext_ref[base:base + TM, :] = up_ref[...]
    pos_top = tile_in_seq * TM + lax.broadcasted_iota(jnp.int32, (HALO, 1), 0)

    def pool_group(g):
        w = POOL_WINDOWS[g]
        cs = slice(g * POOL_GROUP, (g + 1) * POOL_GROUP)
        src, shift, level = ext_ref, 1, 0
        while 2 * shift < w:
            lvl_ref = lvl_refs[level % 2]
            lvl_ref[POOL_PAD:base + TM, cs] = (src[POOL_PAD:base + TM, cs]
                                              + src[POOL_PAD - shift:base + TM - shift, cs])
            src, shift, level = lvl_ref, 2 * shift, level + 1
        sums = src[base:base + TM, cs] + src[base - shift:base + TM - shift, cs]
        inv_top = 1.0 / jnp.minimum(pos_top + 1, w).astype(F32)
        mean = jnp.concatenate([sums[:HALO] * inv_top, sums[HALO:] * (1.0 / w)], axis=0)
        mixed = (mean - ext_ref[base:base + TM, cs]).astype(BF16)
        y = jnp.dot(mixed, poolw_ref[g], preferred_element_type=F32)
        pm_ref[:, cs] = (y * pscale_ref[:, cs]).astype(BF16)

    col_chunks = [slice(lo, lo + COL_CHUNK) for lo in range(0, D_MODEL, COL_CHUNK)]

    def gate_tanh(cols):
        z_half = jnp.dot(hb_ref[...], wg_ref[:, cols], preferred_element_type=F32)
        return jnp.tanh(_row_scale(z_half, rscale) + 0.5 * bg_ref[:, cols])

    for cs in col_chunks:
        y_half = jnp.dot(attn_ref[...], wba_ref[:, cs], preferred_element_type=F32)
        t1 = gate_tanh(slice(D_MODEL + cs.start, D_MODEL + cs.stop))
        ga_ref[:, cs] = y_half + t1 * y_half
    g0 = [gate_tanh(cs) for cs in col_chunks[:2]]
    for g in (3, 0, 1, 2):
        pool_group(g)

    for c, cs in enumerate(col_chunks):
        if c == len(g0):
            g0.append(gate_tanh(cs))
        y_half = jnp.dot(pm_ref[...], wbp_ref[:, cs], preferred_element_type=F32)
        mg_ref[:, cs] = (y_half + g0[c] * y_half + ga_ref[:, cs]).astype(BF16)
    for cs in col_chunks:
        ga_ref[:, cs] = x1_ref[:, cs] + jnp.dot(mg_ref[...], wout_ref[:, cs],
                                                preferred_element_type=F32)

    hb_ref[...], rscale2 = _norm_split(ga_ref[...], n2_ref[...])
    _swiglu_act(hb_ref, rscale2, wg2_ref, wu2_ref, act_ref)
    hm = TM // 2
    for rows in (slice(0, hm), slice(hm, TM)):
        for cs in col_chunks:
            out_ref[rows, cs] = ga_ref[rows, cs] + jnp.dot(act_ref[rows, :], wd_ref[:, cs],
                                                           preferred_element_type=F32)
        out_ref[rows, :] = _rmsnorm(out_ref[rows, :], nf_ref[...])


def _resident(shape):
    return pl.BlockSpec(shape, lambda i: (0,) * len(shape), pipeline_mode=pl.Buffered(1))


def _toeplitz(w):
    n_heads, width = w.shape
    flat = jnp.tile(jnp.pad(w, ((0, 0), (0, 1))), (1, CHUNK))
    rows = flat[:, :CHUNK * width].reshape(n_heads, CHUNK, width)
    return rows[:, :, CHUNK - 1:]


def _bias_pieces(rel_bias):
    assert MAX_REL == CHUNK
    rb = rel_bias.astype(F32) * LOG2E
    n_heads = rb.shape[0]
    far = rb[:, 2 * MAX_REL]
    t_own = _toeplitz(rb[:, 1:2 * MAX_REL][:, ::-1])
    t_prev = _toeplitz(jnp.concatenate(
        [jnp.broadcast_to(far[:, None], (n_heads, CHUNK)), rb[:, MAX_REL + 1:2 * MAX_REL][:, ::-1]],
        axis=1))
    full = (n_heads, CHUNK, LANES)
    return jnp.stack([jnp.full(full, MASK_VALUE, F32),
                      jnp.broadcast_to(far[:, None, None], full),
                      jnp.concatenate([t_prev, t_prev], axis=2),
                      jnp.concatenate([t_own, t_own], axis=2)], axis=1)


def kernel(x, ffn1_norm, ffn1_w_gate, ffn1_w_up, ffn1_w_down, mix_norm, w_in, pool_w, pool_scale,
           rel_bias, w_branch_pool, w_branch_attn, w_gate, b_gate, w_out, ffn2_norm, ffn2_w_gate,
           ffn2_w_up, ffn2_w_down, final_norm):
    b, s, d = x.shape
    assert d == D_MODEL and s % TM == 0 and ffn1_norm.shape[0] == 1
    n_tok = b * s
    n_tiles = n_tok // TM
    assert n_tiles == N_TILES
    tiles_per_seq = s // TM
    xf = x.reshape(n_tok, d)

    row = lambda v: v.reshape(1, -1).astype(F32)
    cparams = pltpu.CompilerParams(dimension_semantics=("arbitrary",),
                                   vmem_limit_bytes=VMEM_LIMIT)
    tile = lambda width: pl.BlockSpec((TM, width), lambda i: (i, 0))

    tok = lambda i: jnp.maximum(i - N_CAST, 0)
    tile1 = lambda width: pl.BlockSpec((TM, width), lambda i: (tok(i), 0))
    staged = lambda w: pl.BlockSpec((w.shape[0] // N_CAST, w.shape[1]),
                                    lambda i: (jnp.minimum(i, N_CAST - 1), 0))
    own_w = [ffn1_w_gate[0], ffn1_w_up[0], ffn1_w_down[0], w_in[0]]
    side_w = [w_gate[0], w_branch_pool[0], w_branch_attn[0], w_out[0], ffn2_w_gate[0],
              ffn2_w_up[0], ffn2_w_down[0], pool_w[0].reshape(POOL_WIDTH, POOL_GROUP)]
    def side_specs():
        specs = []
        for w, (rows, steps, _) in zip(side_w, SIDE_ROWS):
            assert w.shape[0] == rows
            specs.append(pl.BlockSpec(
                (rows // steps, w.shape[1]),
                functools.partial(lambda steps, i: (jnp.minimum(tok(i), steps - 1), 0), steps)))
        return specs

    res = pl.pallas_call(
        _ffn1_inproj_kernel,
        grid=(N_CAST + n_tiles,),
        in_specs=[tile1(D_MODEL), _resident((1, D_MODEL)), staged(own_w[0]), staged(own_w[1]),
                  staged(own_w[2]), _resident((1, D_MODEL)), staged(own_w[3])] + side_specs(),
        out_specs=[tile1(D_MODEL), tile1(POOL_WIDTH), tile1(3 * ATTN_WIDTH)] + side_specs(),
        out_shape=[jax.ShapeDtypeStruct((n_tok, D_MODEL), F32),
                   jax.ShapeDtypeStruct((n_tok, POOL_WIDTH), F32),
                   jax.ShapeDtypeStruct((n_tok, 3 * ATTN_WIDTH), BF16)]
                  + [jax.ShapeDtypeStruct(w.shape, BF16) for w in side_w],
        scratch_shapes=[pltpu.VMEM(w.shape, BF16) for w in own_w]
                       + [pltpu.VMEM((TM, D_MODEL), BF16), pltpu.VMEM((TM, D_FF), BF16)],
        compiler_params=cparams,
        name="ffn1_inproj",
    )(xf, row(ffn1_norm), own_w[0], own_w[1], own_w[2], row(mix_norm), own_w[3], *side_w)
    x1, up, qkv = res[:3]
    wgate_b, wbp_b, wba_b, wout_b, wg2_b, wu2_b, wd2_b, poolw_b = res[3:]

    prev = lambda i: jnp.maximum(i - 1, 0)
    attn = pl.pallas_call(
        functools.partial(_attn_kernel, tiles_per_seq),
        grid=(n_tiles,),
        in_specs=[pl.BlockSpec((TM, ATTN_WIDTH), lambda i: (i, 0)),
                  pl.BlockSpec((TM, ATTN_WIDTH), lambda i: (prev(i), 1)),
                  pl.BlockSpec((TM, ATTN_WIDTH), lambda i: (i, 1)),
                  pl.BlockSpec((TM, ATTN_WIDTH), lambda i: (prev(i), 2)),
                  pl.BlockSpec((TM, ATTN_WIDTH), lambda i: (i, 2)),
                  _resident((N_HEADS, 4, CHUNK, LANES))],
        out_specs=tile(ATTN_WIDTH),
        out_shape=jax.ShapeDtypeStruct((n_tok, ATTN_WIDTH), BF16),
        scratch_shapes=[pltpu.VMEM((1 + TM // Q_BLOCK, N_HEADS, Q_BLOCK, KEY_BLOCK), F32),
                        pltpu.VMEM((2 * TM, ATTN_WIDTH), BF16),
                        pltpu.VMEM((2 * TM, ATTN_WIDTH), BF16)],
        compiler_params=cparams,
        name="chunk_attn",
    )(qkv, qkv, qkv, qkv, qkv, _bias_pieces(rel_bias[0]))

    halo_blocks = TM // HALO
    out = pl.pallas_call(
        functools.partial(_mix_ffn2_kernel, tiles_per_seq),
        grid=(n_tiles,),
        in_specs=[tile(D_MODEL), tile(POOL_WIDTH),
                  pl.BlockSpec((HALO, POOL_WIDTH),
                               lambda i: (jnp.maximum(i * halo_blocks - 1, 0), 0)),
                  tile(ATTN_WIDTH), _resident((1, D_MODEL)),
                  _resident((D_MODEL, 2 * D_MODEL)), _resident((1, 2 * D_MODEL)),
                  _resident((len(POOL_WINDOWS), POOL_GROUP, POOL_GROUP)),
                  _resident((1, POOL_WIDTH)), _resident((POOL_WIDTH, D_MODEL)),
                  _resident((ATTN_WIDTH, D_MODEL)), _resident((D_MODEL, D_MODEL)),
                  _resident((1, D_MODEL)), _resident((D_MODEL, D_FF)),
                  _resident((D_MODEL, D_FF)), _resident((D_FF, D_MODEL)),
                  _resident((1, D_MODEL))],
        out_specs=tile(D_MODEL),
        out_shape=jax.ShapeDtypeStruct((n_tok, D_MODEL), F32),
        scratch_shapes=[pltpu.VMEM((TM, D_MODEL), BF16), pltpu.VMEM((TM, D_FF), BF16),
                        pltpu.VMEM((POOL_PAD + HALO + TM, POOL_WIDTH), F32),
                        pltpu.VMEM((POOL_PAD + HALO + TM, POOL_WIDTH), F32),
                        pltpu.VMEM((POOL_PAD + HALO + TM, POOL_WIDTH), F32),
                        pltpu.VMEM((TM, POOL_WIDTH), BF16), pltpu.VMEM((TM, D_MODEL), BF16),
                        pltpu.VMEM((TM, D_MODEL), F32)],
        compiler_params=cparams,
        name="mix_ffn2",
    )(x1, up, up, attn, row(mix_norm), wgate_b, row(b_gate),
      poolw_b.reshape(len(POOL_WINDOWS), POOL_GROUP, POOL_GROUP), row(pool_scale), wbp_b, wba_b,
      wout_b, row(ffn2_norm), wg2_b, wu2_b, wd2_b, row(final_norm))
    return out.reshape(b, s, d)
```

```python
import functools

import jax
import jax.numpy as jnp
from jax import lax
from jax.experimental import pallas as pl
from jax.experimental.pallas import tpu as pltpu

D_MODEL = 1024
D_FF = 2816
CHUNK = 64
LEFT_CHUNKS = 8
POOL_WIDTH = 512
POOL_WINDOWS = (2, 4, 8, 16)
POOL_GROUP = 128
N_HEADS = 8
HEAD_DIM = 64
ATTN_WIDTH = 512
MAX_REL = 64
IN_WIDTH = POOL_WIDTH + 3 * ATTN_WIDTH
EPS = 1e-6

LANES = 128
TM = 512
FF_CHUNK = 256
COL_CHUNK = 256
Q_BLOCK = 256
ATTN_PREV = LEFT_CHUNKS * CHUNK
ATTN_TM = 1024
KEY_BLOCK = Q_BLOCK + ATTN_PREV
SCORE_LOOKAHEAD = 2
HALO = 16
POOL_PAD = 8
N_CAST = 8
N_TILES = 32
SIDE_ROWS = ((D_MODEL, N_TILES, 0.5),
             (POOL_WIDTH, N_TILES, 0.5),
             (ATTN_WIDTH, N_TILES, 0.5),
             (D_MODEL, N_TILES, 1.0),
             (D_FF, N_TILES // 2, 0.5),
             (POOL_WIDTH, N_TILES, 1.0))
MASK_VALUE = -1e30
LOG2E = 1.4426950408889634
Q_SCALE = HEAD_DIM ** -0.5 * LOG2E
VMEM_LIMIT = 56 * 1024 * 1024

F32 = jnp.float32
BF16 = jnp.bfloat16


def _rmsnorm(x, g):
    ms = jnp.mean(x * x, axis=-1, keepdims=True)
    return x * lax.rsqrt(ms + EPS) * g


def _norm_split(x, g):
    rscale = lax.rsqrt(jnp.mean(x * x, axis=-1, keepdims=True) + EPS)
    return (x * g).astype(BF16), jnp.broadcast_to(rscale, (x.shape[0], LANES))


def _row_scale(y, rscale):
    return jnp.concatenate([y[:, lo:lo + LANES] * rscale for lo in range(0, y.shape[1], LANES)],
                           axis=1)


def _interleave_gate_up(dst_ref, rows, gate_blk, up_blk):
    for c in range(D_FF // FF_CHUNK):
        src = slice(c * FF_CHUNK, (c + 1) * FF_CHUNK)
        dst_ref[rows, 2 * c * FF_CHUNK:(2 * c + 1) * FF_CHUNK] = gate_blk[:, src].astype(BF16)
        dst_ref[rows, (2 * c + 1) * FF_CHUNK:(2 * c + 2) * FF_CHUNK] = up_blk[:, src].astype(BF16)


def _swiglu_act(hb_ref, rscale, wgu_ref, act_ref):
    half_rscale = 0.5 * rscale
    for c in range(D_FF // FF_CHUNK):
        gu = jnp.dot(hb_ref[...], wgu_ref[:, 2 * c * FF_CHUNK:(2 * c + 2) * FF_CHUNK],
                     preferred_element_type=F32)
        hg = _row_scale(gu[:, :FF_CHUNK], half_rscale)
        up = _row_scale(gu[:, FF_CHUNK:], rscale)
        act_ref[:, c * FF_CHUNK:(c + 1) * FF_CHUNK] = ((hg + hg * jnp.tanh(hg)) * up).astype(BF16)


def _ffn1_inproj_kernel(x_ref, n1_ref, wg_blk, wu_blk, wd_blk, nm_ref, win_blk,
                        wg2_blk, wu2_blk, *rest):
    n_side = len(SIDE_ROWS)
    side_in = rest[:n_side]
    x1_ref, up_ref, qkv_ref, wgu2_out = rest[n_side:n_side + 4]
    side_out = rest[n_side + 4:2 * n_side + 4]
    wgu_ref, wd_ref, win_ref, hb_ref, act_ref = rest[2 * n_side + 4:]
    i = pl.program_id(0)

    @pl.when(i < N_CAST)
    def _stage_weights():
        def rows_of(blk):
            return pl.ds(pl.multiple_of(i * blk.shape[0], blk.shape[0]), blk.shape[0])

        _interleave_gate_up(wgu_ref, rows_of(wg_blk), wg_blk, wu_blk)
        wd_ref[rows_of(wd_blk), :] = (wd_blk[...] * 0.5).astype(BF16)
        win_ref[rows_of(win_blk), :] = win_blk[...].astype(BF16)

    @pl.when(i >= N_CAST)
    def _token_tile():
        hb_ref[...], rscale = _norm_split(x_ref[...], n1_ref[...])
        _swiglu_act(hb_ref, rscale, wgu_ref, act_ref)
        for lo in range(0, D_MODEL, COL_CHUNK):
            cs = slice(lo, lo + COL_CHUNK)
            x1_ref[:, cs] = x_ref[:, cs] + jnp.dot(act_ref[...], wd_ref[:, cs],
                                                   preferred_element_type=F32)
        hb_ref[...], rscale = _norm_split(x1_ref[...], nm_ref[...])
        q_rscale = rscale * Q_SCALE
        for lo in range(0, IN_WIDTH, COL_CHUNK):
            y = jnp.dot(hb_ref[...], win_ref[:, lo:lo + COL_CHUNK], preferred_element_type=F32)
            if lo < POOL_WIDTH:
                up_ref[:, lo:lo + COL_CHUNK] = _row_scale(y, rscale)
            else:
                is_q = lo < POOL_WIDTH + ATTN_WIDTH
                qkv_ref[:, lo - POOL_WIDTH:lo - POOL_WIDTH + COL_CHUNK] = _row_scale(
                    y, q_rscale if is_q else rscale).astype(BF16)
        _interleave_gate_up(wgu2_out, slice(None), wg2_blk, wu2_blk)
        for src, dst, (_, steps, scale) in zip(side_in, side_out, SIDE_ROWS):
            @pl.when(i - N_CAST < steps)
            def _cast_block(src=src, dst=dst, scale=scale):
                w = src[...] if scale == 1.0 else src[...] * scale
                dst[...] = w.astype(BF16)


def _band_kind(dchunk):
    if dchunk < 0 or dchunk > LEFT_CHUNKS:
        return 0
    if dchunk < LEFT_CHUNKS - 1:
        return 1
    return 2 if dchunk == LEFT_CHUNKS - 1 else 3


def _attn_kernel(tiles_per_seq, q_ref, kp_ref, kc_ref, vp_ref, vc_ref, pieces_ref,
                 o_ref, bias_ref, v_even_ref, v_odd_ref):
    i = pl.program_id(0)
    is_first = (i % tiles_per_seq) == 0
    n_first_blocks = ATTN_PREV // Q_BLOCK

    @pl.when(i == 0)
    def _build_bias_tables():
        low = lax.broadcasted_iota(jnp.int32, (CHUNK, LANES), 1) < CHUNK

        def per_head(h, carry):
            for variant in range(1 + n_first_blocks):
                n_before = 0 if variant == 0 else (ATTN_PREV - (variant - 1) * Q_BLOCK) // LANES
                for qc in range(Q_BLOCK // CHUNK):
                    for kp in range(KEY_BLOCK // LANES):
                        ka, kb = _band_kind(2 * kp - qc), _band_kind(2 * kp + 1 - qc)
                        if kp < n_before:
                            ka = kb = 0
                        blk = pieces_ref[h, ka]
                        if kb != ka:
                            blk = jnp.where(low, blk, pieces_ref[h, kb])
                        bias_ref[variant, h, qc * CHUNK:(qc + 1) * CHUNK,
                                 kp * LANES:(kp + 1) * LANES] = blk
            return carry

        lax.fori_loop(0, N_HEADS, per_head, 0)

    for src, row0 in ((vp_ref, 0), (vc_ref, ATTN_PREV)):
        v = src[...]
        even_lanes = (lax.broadcasted_iota(jnp.int32, v.shape, 1) % LANES) < HEAD_DIM
        ones = jnp.ones(v.shape, BF16)
        v_even_ref[row0:row0 + v.shape[0], :] = jnp.where(even_lanes, v, ones)
        v_odd_ref[row0:row0 + v.shape[0], :] = jnp.where(even_lanes, ones, v)

    low_half = lax.broadcasted_iota(jnp.int32, (Q_BLOCK, LANES), 1) < HEAD_DIM
    units = [(sb, p, par) for sb in range(ATTN_TM // Q_BLOCK) for p in range(N_HEADS // 2)
             for par in range(2)]

    def scores(unit):
        sb, p, par = unit
        r0, cs = sb * Q_BLOCK, slice(p * LANES, (p + 1) * LANES)
        qp = q_ref[r0:r0 + Q_BLOCK, cs]
        qm = jnp.where(low_half if par == 0 else jnp.logical_not(low_half), qp, jnp.zeros_like(qp))
        nt = (((1,), (1,)), ((), ()))
        parts = []
        if r0 < ATTN_PREV:
            parts.append(lax.dot_general(qm, kp_ref[r0:ATTN_PREV, cs], nt,
                                         preferred_element_type=F32))
        parts.append(lax.dot_general(
            qm, kc_ref[max(r0 - ATTN_PREV, 0):r0 + KEY_BLOCK - ATTN_PREV, cs], nt,
            preferred_element_type=F32))
        s = parts[0] if len(parts) == 1 else jnp.concatenate(parts, axis=1)
        variant = jnp.where(is_first, 1 + sb, 0) if sb < n_first_blocks else 0
        return s + bias_ref[variant, 2 * p + par]

    pending = [scores(u) for u in units[:SCORE_LOOKAHEAD]]
    outs = []
    for n, (sb, p, par) in enumerate(units):
        r0, cs = sb * Q_BLOCK, slice(p * LANES, (p + 1) * LANES)
        s = pending.pop(0)
        e = jnp.exp2(s - jnp.max(s, axis=-1, keepdims=True))
        if n + SCORE_LOOKAHEAD < len(units):
            pending.append(scores(units[n + SCORE_LOOKAHEAD]))
        v_ref = v_even_ref if par == 0 else v_odd_ref
        o = jnp.dot(e.astype(BF16), v_ref[r0:r0 + KEY_BLOCK, cs], preferred_element_type=F32)
        outs.append(o / pltpu.roll(o, HEAD_DIM, axis=1))
        if par == 1:
            o_ref[r0:r0 + Q_BLOCK, cs] = jnp.where(low_half, outs[-2], outs[-1]).astype(BF16)


def _mix_ffn2_kernel(tiles_per_seq, x1_ref, up_ref, halo_ref, attn_ref, nm_ref, wg_ref,
                     bg_ref, poolw_ref, pscale_ref, wbp_ref, wba_ref, wout_ref,
                     n2_ref, wgu2_ref, wd_ref, nf_ref,
                     out_ref, hb_ref, act_ref, ext_ref, lvl_a_ref, lvl_b_ref, pm_ref, mg_ref, ga_ref):
    i = pl.program_id(0)
    tile_in_seq = i % tiles_per_seq
    hb_ref[...], rscale = _norm_split(x1_ref[...], nm_ref[...])

    base = POOL_PAD + HALO
    keep_halo = jnp.where(tile_in_seq == 0, 0.0, 1.0).astype(F32)
    lvl_refs = (lvl_a_ref, lvl_b_ref)
    for ref in (ext_ref,) + lvl_refs:
        ref[0:POOL_PAD, :] = jnp.zeros((POOL_PAD, POOL_WIDTH), F32)
    ext_ref[POOL_PAD:base, :] = halo_ref[...] * keep_halo
    ext_ref[base:base + TM, :] = up_ref[...]
    pos_top = tile_in_seq * TM + lax.broadcasted_iota(jnp.int32, (HALO, 1), 0)

    def pool_group(g):
        w = POOL_WINDOWS[g]
        cs = slice(g * POOL_GROUP, (g + 1) * POOL_GROUP)
        src, shift, level = ext_ref, 1, 0
        while 2 * shift < w:
            lvl_ref = lvl_refs[level % 2]
            lvl_ref[POOL_PAD:base + TM, cs] = (src[POOL_PAD:base + TM, cs]
                                              + src[POOL_PAD - shift:base + TM - shift, cs])
            src, shift, level = lvl_ref, 2 * shift, level + 1
        sums = src[base:base + TM, cs] + src[base - shift:base + TM - shift, cs]
        inv_top = 1.0 / jnp.minimum(pos_top + 1, w).astype(F32)
        mean = jnp.concatenate([sums[:HALO] * inv_top, sums[HALO:] * (1.0 / w)], axis=0)
        mixed = (mean - ext_ref[base:base + TM, cs]).astype(BF16)
        y = jnp.dot(mixed, poolw_ref[g], preferred_element_type=F32)
        pm_ref[:, cs] = (y * pscale_ref[:, cs]).astype(BF16)

    col_chunks = [slice(lo, lo + COL_CHUNK) for lo in range(0, D_MODEL, COL_CHUNK)]

    def gate_tanh(cols):
        z_half = jnp.dot(hb_ref[...], wg_ref[:, cols], preferred_element_type=F32)
        return jnp.tanh(_row_scale(z_half, rscale) + 0.5 * bg_ref[:, cols])

    for cs in col_chunks:
        y_half = jnp.dot(attn_ref[...], wba_ref[:, cs], preferred_element_type=F32)
        t1 = gate_tanh(slice(D_MODEL + cs.start, D_MODEL + cs.stop))
        ga_ref[:, cs] = y_half + t1 * y_half
    g0 = [gate_tanh(cs) for cs in col_chunks[:2]]
    for g in (3, 0, 1, 2):
        pool_group(g)

    for c, cs in enumerate(col_chunks):
        if c == len(g0):
            g0.append(gate_tanh(cs))
        y_half = jnp.dot(pm_ref[...], wbp_ref[:, cs], preferred_element_type=F32)
        mg_ref[:, cs] = (y_half + g0[c] * y_half + ga_ref[:, cs]).astype(BF16)
    for cs in col_chunks:
        ga_ref[:, cs] = x1_ref[:, cs] + jnp.dot(mg_ref[...], wout_ref[:, cs],
                                                preferred_element_type=F32)

    hb_ref[...], rscale2 = _norm_split(ga_ref[...], n2_ref[...])
    _swiglu_act(hb_ref, rscale2, wgu2_ref, act_ref)
    hm = TM // 2
    for rows in (slice(0, hm), slice(hm, TM)):
        for cs in col_chunks:
            out_ref[rows, cs] = ga_ref[rows, cs] + jnp.dot(act_ref[rows, :], wd_ref[:, cs],
                                                           preferred_element_type=F32)
        out_ref[rows, :] = _rmsnorm(out_ref[rows, :], nf_ref[...])


def _resident(shape):
    return pl.BlockSpec(shape, lambda i: (0,) * len(shape), pipeline_mode=pl.Buffered(1))


def _toeplitz(w):
    n_heads, width = w.shape
    flat = jnp.tile(jnp.pad(w, ((0, 0), (0, 1))), (1, CHUNK))
    rows = flat[:, :CHUNK * width].reshape(n_heads, CHUNK, width)
    return rows[:, :, CHUNK - 1:]


def _bias_pieces(rel_bias):
    assert MAX_REL == CHUNK
    rb = rel_bias.astype(F32) * LOG2E
    n_heads = rb.shape[0]
    far = rb[:, 2 * MAX_REL]
    t_own = _toeplitz(rb[:, 1:2 * MAX_REL][:, ::-1])
    t_prev = _toeplitz(jnp.concatenate(
        [jnp.broadcast_to(far[:, None], (n_heads, CHUNK)), rb[:, MAX_REL + 1:2 * MAX_REL][:, ::-1]],
        axis=1))
    full = (n_heads, CHUNK, LANES)
    return jnp.stack([jnp.full(full, MASK_VALUE, F32),
                      jnp.broadcast_to(far[:, None, None], full),
                      jnp.concatenate([t_prev, t_prev], axis=2),
                      jnp.concatenate([t_own, t_own], axis=2)], axis=1)


def kernel(x, ffn1_norm, ffn1_w_gate, ffn1_w_up, ffn1_w_down, mix_norm, w_in, pool_w, pool_scale,
           rel_bias, w_branch_pool, w_branch_attn, w_gate, b_gate, w_out, ffn2_norm, ffn2_w_gate,
           ffn2_w_up, ffn2_w_down, final_norm):
    b, s, d = x.shape
    assert d == D_MODEL and s % TM == 0 and ffn1_norm.shape[0] == 1
    n_tok = b * s
    n_tiles = n_tok // TM
    assert n_tiles == N_TILES
    tiles_per_seq = s // TM
    xf = x.reshape(n_tok, d)

    row = lambda v: v.reshape(1, -1).astype(F32)
    cparams = pltpu.CompilerParams(dimension_semantics=("arbitrary",),
                                   vmem_limit_bytes=VMEM_LIMIT)
    tile = lambda width: pl.BlockSpec((TM, width), lambda i: (i, 0))

    tok = lambda i: jnp.maximum(i - N_CAST, 0)
    tile1 = lambda width: pl.BlockSpec((TM, width), lambda i: (tok(i), 0))
    staged = lambda w: pl.BlockSpec((w.shape[0] // N_CAST, w.shape[1]),
                                    lambda i: (jnp.minimum(i, N_CAST - 1), 0))
    own_w = [ffn1_w_gate[0], ffn1_w_up[0], ffn1_w_down[0], w_in[0]]
    side_w = [w_gate[0], w_branch_pool[0], w_branch_attn[0], w_out[0], ffn2_w_down[0],
              pool_w[0].reshape(POOL_WIDTH, POOL_GROUP)]
    gate_up_rows = D_MODEL // n_tiles
    gate_up_spec = lambda width: pl.BlockSpec((gate_up_rows, width), lambda i: (tok(i), 0))

    def side_specs():
        specs = []
        for w, (rows, steps, _) in zip(side_w, SIDE_ROWS):
            assert w.shape[0] == rows
            specs.append(pl.BlockSpec(
                (rows // steps, w.shape[1]),
                functools.partial(lambda steps, i: (jnp.minimum(tok(i), steps - 1), 0), steps)))
        return specs

    res = pl.pallas_call(
        _ffn1_inproj_kernel,
        grid=(N_CAST + n_tiles,),
        in_specs=[tile1(D_MODEL), _resident((1, D_MODEL)), staged(own_w[0]), staged(own_w[1]),
                  staged(own_w[2]), _resident((1, D_MODEL)), staged(own_w[3]),
                  gate_up_spec(D_FF), gate_up_spec(D_FF)] + side_specs(),
        out_specs=[tile1(D_MODEL), tile1(POOL_WIDTH), tile1(3 * ATTN_WIDTH),
                   gate_up_spec(2 * D_FF)] + side_specs(),
        out_shape=[jax.ShapeDtypeStruct((n_tok, D_MODEL), F32),
                   jax.ShapeDtypeStruct((n_tok, POOL_WIDTH), F32),
                   jax.ShapeDtypeStruct((n_tok, 3 * ATTN_WIDTH), BF16),
                   jax.ShapeDtypeStruct((D_MODEL, 2 * D_FF), BF16)]
                  + [jax.ShapeDtypeStruct(w.shape, BF16) for w in side_w],
        scratch_shapes=[pltpu.VMEM((D_MODEL, 2 * D_FF), BF16), pltpu.VMEM((D_FF, D_MODEL), BF16),
                        pltpu.VMEM((D_MODEL, IN_WIDTH), BF16),
                        pltpu.VMEM((TM, D_MODEL), BF16), pltpu.VMEM((TM, D_FF), BF16)],
        compiler_params=cparams,
        name="ffn1_inproj",
    )(xf, row(ffn1_norm), own_w[0], own_w[1], own_w[2], row(mix_norm), own_w[3],
      ffn2_w_gate[0], ffn2_w_up[0], *side_w)
    x1, up, qkv, wgu2_b = res[:4]
    wgate_b, wbp_b, wba_b, wout_b, wd2_b, poolw_b = res[4:]

    assert s % ATTN_TM == 0 and ATTN_TM % ATTN_PREV == 0
    prev = lambda i: jnp.maximum(i * (ATTN_TM // ATTN_PREV) - 1, 0)
    cur_spec = lambda col: pl.BlockSpec((ATTN_TM, ATTN_WIDTH), lambda i: (i, col))
    prev_spec = lambda col: pl.BlockSpec((ATTN_PREV, ATTN_WIDTH), lambda i: (prev(i), col))
    attn = pl.pallas_call(
        functools.partial(_attn_kernel, s // ATTN_TM),
        grid=(n_tok // ATTN_TM,),
        in_specs=[cur_spec(0), prev_spec(1), cur_spec(1), prev_spec(2), cur_spec(2),
                  _resident((N_HEADS, 4, CHUNK, LANES))],
        out_specs=cur_spec(0),
        out_shape=jax.ShapeDtypeStruct((n_tok, ATTN_WIDTH), BF16),
        scratch_shapes=[pltpu.VMEM((1 + ATTN_PREV // Q_BLOCK, N_HEADS, Q_BLOCK, KEY_BLOCK), F32),
                        pltpu.VMEM((ATTN_PREV + ATTN_TM, ATTN_WIDTH), BF16),
                        pltpu.VMEM((ATTN_PREV + ATTN_TM, ATTN_WIDTH), BF16)],
        compiler_params=cparams,
        name="chunk_attn",
    )(qkv, qkv, qkv, qkv, qkv, _bias_pieces(rel_bias[0]))

    halo_blocks = TM // HALO
    out = pl.pallas_call(
        functools.partial(_mix_ffn2_kernel, tiles_per_seq),
        grid=(n_tiles,),
        in_specs=[tile(D_MODEL), tile(POOL_WIDTH),
                  pl.BlockSpec((HALO, POOL_WIDTH),
                               lambda i: (jnp.maximum(i * halo_blocks - 1, 0), 0)),
                  tile(ATTN_WIDTH), _resident((1, D_MODEL)),
                  _resident((D_MODEL, 2 * D_MODEL)), _resident((1, 2 * D_MODEL)),
                  _resident((len(POOL_WINDOWS), POOL_GROUP, POOL_GROUP)),
                  _resident((1, POOL_WIDTH)), _resident((POOL_WIDTH, D_MODEL)),
                  _resident((ATTN_WIDTH, D_MODEL)), _resident((D_MODEL, D_MODEL)),
                  _resident((1, D_MODEL)), _resident((D_MODEL, 2 * D_FF)),
                  _resident((D_FF, D_MODEL)), _resident((1, D_MODEL))],
        out_specs=tile(D_MODEL),
        out_shape=jax.ShapeDtypeStruct((n_tok, D_MODEL), F32),
        scratch_shapes=[pltpu.VMEM((TM, D_MODEL), BF16), pltpu.VMEM((TM, D_FF), BF16),
                        pltpu.VMEM((POOL_PAD + HALO + TM, POOL_WIDTH), F32),
                        pltpu.VMEM((POOL_PAD + HALO + TM, POOL_WIDTH), F32),
                        pltpu.VMEM((POOL_PAD + HALO + TM, POOL_WIDTH), F32),
                        pltpu.VMEM((TM, POOL_WIDTH), BF16), pltpu.VMEM((TM, D_MODEL), BF16),
                        pltpu.VMEM((TM, D_MODEL), F32)],
        compiler_params=cparams,
        name="mix_ffn2",
    )(x1, up, up, attn, row(mix_norm), wgate_b, row(b_gate),
      poolw_b.reshape(len(POOL_WINDOWS), POOL_GROUP, POOL_GROUP), row(pool_scale), wbp_b, wba_b,
      wout_b, row(ffn2_norm), wgu2_b, wd2_b, row(final_norm))
    return out.reshape(b, s, d)
```

```python
import functools

import jax
import jax.numpy as jnp
from jax import lax
from jax.experimental import pallas as pl
from jax.experimental.pallas import tpu as pltpu

D_MODEL = 1024
D_FF = 2816
CHUNK = 64
LEFT_CHUNKS = 8
POOL_WIDTH = 512
POOL_WINDOWS = (2, 4, 8, 16)
POOL_GROUP = 128
N_HEADS = 8
HEAD_DIM = 64
ATTN_WIDTH = 512
MAX_REL = 64
IN_WIDTH = POOL_WIDTH + 3 * ATTN_WIDTH
EPS = 1e-6

LANES = 128
TM = 512
FF_CHUNK = 256
COL_CHUNK = 256
Q_BLOCK = 256
ATTN_PREV = LEFT_CHUNKS * CHUNK
ATTN_TM = 2048
KEY_BLOCK = Q_BLOCK + ATTN_PREV
SCORE_LOOKAHEAD = 2
HALO = 16
POOL_PAD = 8
N_CAST = 8
N_TILES = 32
SIDE_ROWS = ((D_MODEL, N_TILES, 0.5),
             (POOL_WIDTH, N_TILES, 0.5),
             (ATTN_WIDTH, N_TILES, 0.5),
             (D_MODEL, N_TILES, 1.0),
             (D_FF, N_TILES // 2, 0.5),
             (POOL_WIDTH, N_TILES, 1.0))
MASK_VALUE = -1e30
LOG2E = 1.4426950408889634
Q_SCALE = HEAD_DIM ** -0.5 * LOG2E
VMEM_LIMIT = 56 * 1024 * 1024

F32 = jnp.float32
BF16 = jnp.bfloat16


def _rmsnorm(x, g):
    ms = jnp.mean(x * x, axis=-1, keepdims=True)
    return x * lax.rsqrt(ms + EPS) * g


def _norm_split(x, g):
    rscale = lax.rsqrt(jnp.mean(x * x, axis=-1, keepdims=True) + EPS)
    return (x * g).astype(BF16), jnp.broadcast_to(rscale, (x.shape[0], LANES))


def _row_scale(y, rscale):
    return jnp.concatenate([y[:, lo:lo + LANES] * rscale for lo in range(0, y.shape[1], LANES)],
                           axis=1)


def _interleave_gate_up(dst_ref, rows, gate_blk, up_blk):
    for c in range(D_FF // FF_CHUNK):
        src = slice(c * FF_CHUNK, (c + 1) * FF_CHUNK)
        dst_ref[rows, 2 * c * FF_CHUNK:(2 * c + 1) * FF_CHUNK] = gate_blk[:, src].astype(BF16)
        dst_ref[rows, (2 * c + 1) * FF_CHUNK:(2 * c + 2) * FF_CHUNK] = up_blk[:, src].astype(BF16)


def _swiglu_act(hb_ref, rscale, wgu_ref, act_ref):
    half_rscale = 0.5 * rscale
    for c in range(D_FF // FF_CHUNK):
        gu = jnp.dot(hb_ref[...], wgu_ref[:, 2 * c * FF_CHUNK:(2 * c + 2) * FF_CHUNK],
                     preferred_element_type=F32)
        hg = _row_scale(gu[:, :FF_CHUNK], half_rscale)
        up = _row_scale(gu[:, FF_CHUNK:], rscale)
        act_ref[:, c * FF_CHUNK:(c + 1) * FF_CHUNK] = ((hg + hg * jnp.tanh(hg)) * up).astype(BF16)


def _ffn1_inproj_kernel(x_ref, n1_ref, wg_blk, wu_blk, wd_blk, nm_ref, win_blk,
                        wg2_blk, wu2_blk, *rest):
    n_side = len(SIDE_ROWS)
    side_in = rest[:n_side]
    x1_ref, up_ref, qkv_ref, wgu2_out = rest[n_side:n_side + 4]
    side_out = rest[n_side + 4:2 * n_side + 4]
    wgu_ref, wd_ref, win_ref, hb_ref, act_ref = rest[2 * n_side + 4:]
    i = pl.program_id(0)

    @pl.when(i < N_CAST)
    def _stage_weights():
        def rows_of(blk):
            return pl.ds(pl.multiple_of(i * blk.shape[0], blk.shape[0]), blk.shape[0])

        _interleave_gate_up(wgu_ref, rows_of(wg_blk), wg_blk, wu_blk)
        wd_ref[rows_of(wd_blk), :] = (wd_blk[...] * 0.5).astype(BF16)
        win_ref[rows_of(win_blk), :] = win_blk[...].astype(BF16)

    @pl.when(i >= N_CAST)
    def _token_tile():
        hb_ref[...], rscale = _norm_split(x_ref[...], n1_ref[...])
        _swiglu_act(hb_ref, rscale, wgu_ref, act_ref)
        for lo in range(0, D_MODEL, COL_CHUNK):
            cs = slice(lo, lo + COL_CHUNK)
            x1_ref[:, cs] = x_ref[:, cs] + jnp.dot(act_ref[...], wd_ref[:, cs],
                                                   preferred_element_type=F32)
        hb_ref[...], rscale = _norm_split(x1_ref[...], nm_ref[...])
        q_rscale = rscale * Q_SCALE
        for lo in range(0, IN_WIDTH, COL_CHUNK):
            y = jnp.dot(hb_ref[...], win_ref[:, lo:lo + COL_CHUNK], preferred_element_type=F32)
            if lo < POOL_WIDTH:
                up_ref[:, lo:lo + COL_CHUNK] = _row_scale(y, rscale)
            else:
                is_q = lo < POOL_WIDTH + ATTN_WIDTH
                qkv_ref[:, lo - POOL_WIDTH:lo - POOL_WIDTH + COL_CHUNK] = _row_scale(
                    y, q_rscale if is_q else rscale).astype(BF16)
        _interleave_gate_up(wgu2_out, slice(None), wg2_blk, wu2_blk)
        for src, dst, (_, steps, scale) in zip(side_in, side_out, SIDE_ROWS):
            @pl.when(i - N_CAST < steps)
            def _cast_block(src=src, dst=dst, scale=scale):
                w = src[...] if scale == 1.0 else src[...] * scale
                dst[...] = w.astype(BF16)


def _band_kind(dchunk):
    if dchunk < 0 or dchunk > LEFT_CHUNKS:
        return 0
    if dchunk < LEFT_CHUNKS - 1:
        return 1
    return 2 if dchunk == LEFT_CHUNKS - 1 else 3


def _attn_kernel(tiles_per_seq, q_ref, kp_ref, kc_ref, vp_ref, vc_ref, pieces_ref,
                 o_ref, bias_ref, v_even_ref, v_odd_ref):
    i = pl.program_id(0)
    is_first = (i % tiles_per_seq) == 0
    n_first_blocks = ATTN_PREV // Q_BLOCK

    @pl.when(i == 0)
    def _build_bias_tables():
        low = lax.broadcasted_iota(jnp.int32, (CHUNK, LANES), 1) < CHUNK

        def per_head(h, carry):
            for variant in range(1 + n_first_blocks):
                n_before = 0 if variant == 0 else (ATTN_PREV - (variant - 1) * Q_BLOCK) // LANES
                for qc in range(Q_BLOCK // CHUNK):
                    for kp in range(KEY_BLOCK // LANES):
                        ka, kb = _band_kind(2 * kp - qc), _band_kind(2 * kp + 1 - qc)
                        if kp < n_before:
                            ka = kb = 0
                        blk = pieces_ref[h, ka]
                        if kb != ka:
                            blk = jnp.where(low, blk, pieces_ref[h, kb])
                        bias_ref[variant, h, qc * CHUNK:(qc + 1) * CHUNK,
                                 kp * LANES:(kp + 1) * LANES] = blk
            return carry

        lax.fori_loop(0, N_HEADS, per_head, 0)

    for src, row0 in ((vp_ref, 0), (vc_ref, ATTN_PREV)):
        v = src[...]
        even_lanes = (lax.broadcasted_iota(jnp.int32, v.shape, 1) % LANES) < HEAD_DIM
        ones = jnp.ones(v.shape, BF16)
        v_even_ref[row0:row0 + v.shape[0], :] = jnp.where(even_lanes, v, ones)
        v_odd_ref[row0:row0 + v.shape[0], :] = jnp.where(even_lanes, ones, v)

    low_half = lax.broadcasted_iota(jnp.int32, (Q_BLOCK, LANES), 1) < HEAD_DIM
    units = [(sb, p, par) for sb in range(ATTN_TM // Q_BLOCK) for p in range(N_HEADS // 2)
             for par in range(2)]

    def scores(unit):
        sb, p, par = unit
        r0, cs = sb * Q_BLOCK, slice(p * LANES, (p + 1) * LANES)
        qp = q_ref[r0:r0 + Q_BLOCK, cs]
        qm = jnp.where(low_half if par == 0 else jnp.logical_not(low_half), qp, jnp.zeros_like(qp))
        nt = (((1,), (1,)), ((), ()))
        parts = []
        if r0 < ATTN_PREV:
            parts.append(lax.dot_general(qm, kp_ref[r0:ATTN_PREV, cs], nt,
                                         preferred_element_type=F32))
        parts.append(lax.dot_general(
            qm, kc_ref[max(r0 - ATTN_PREV, 0):r0 + KEY_BLOCK - ATTN_PREV, cs], nt,
            preferred_element_type=F32))
        s = parts[0] if len(parts) == 1 else jnp.concatenate(parts, axis=1)
        variant = jnp.where(is_first, 1 + sb, 0) if sb < n_first_blocks else 0
        return s + bias_ref[variant, 2 * p + par]

    pending = [scores(u) for u in units[:SCORE_LOOKAHEAD]]
    outs = []
    for n, (sb, p, par) in enumerate(units):
        r0, cs = sb * Q_BLOCK, slice(p * LANES, (p + 1) * LANES)
        s = pending.pop(0)
        e = jnp.exp2(s - jnp.max(s, axis=-1, keepdims=True))
        if n + SCORE_LOOKAHEAD < len(units):
            pending.append(scores(units[n + SCORE_LOOKAHEAD]))
        v_ref = v_even_ref if par == 0 else v_odd_ref
        o = jnp.dot(e.astype(BF16), v_ref[r0:r0 + KEY_BLOCK, cs], preferred_element_type=F32)
        outs.append(o / pltpu.roll(o, HEAD_DIM, axis=1))
        if par == 1:
            o_ref[r0:r0 + Q_BLOCK, cs] = jnp.where(low_half, outs[-2], outs[-1]).astype(BF16)


def _mix_ffn2_kernel(tiles_per_seq, x1_ref, up_ref, halo_ref, attn_ref, nm_ref, wg_ref,
                     bg_ref, poolw_ref, pscale_ref, wbp_ref, wba_ref, wout_ref,
                     n2_ref, wgu2_ref, wd_ref, nf_ref,
                     out_ref, hb_ref, act_ref, ext_ref, lvl_a_ref, lvl_b_ref, pm_ref, mg_ref, ga_ref):
    i = pl.program_id(0)
    tile_in_seq = i % tiles_per_seq
    hb_ref[...], rscale = _norm_split(x1_ref[...], nm_ref[...])

    base = POOL_PAD + HALO
    keep_halo = jnp.where(tile_in_seq == 0, 0.0, 1.0).astype(F32)
    lvl_refs = (lvl_a_ref, lvl_b_ref)
    for ref in (ext_ref,) + lvl_refs:
        ref[0:POOL_PAD, :] = jnp.zeros((POOL_PAD, POOL_WIDTH), F32)
    ext_ref[POOL_PAD:base, :] = halo_ref[...] * keep_halo
    ext_ref[base:base + TM, :] = up_ref[...]
    pos_top = tile_in_seq * TM + lax.broadcasted_iota(jnp.int32, (HALO, 1), 0)

    def pool_group(g):
        w = POOL_WINDOWS[g]
        cs = slice(g * POOL_GROUP, (g + 1) * POOL_GROUP)
        src, shift, level = ext_ref, 1, 0
        while 2 * shift < w:
            lvl_ref = lvl_refs[level % 2]
            lvl_ref[POOL_PAD:base + TM, cs] = (src[POOL_PAD:base + TM, cs]
                                              + src[POOL_PAD - shift:base + TM - shift, cs])
            src, shift, level = lvl_ref, 2 * shift, level + 1
        sums = src[base:base + TM, cs] + src[base - shift:base + TM - shift, cs]
        inv_top = 1.0 / jnp.minimum(pos_top + 1, w).astype(F32)
        mean = jnp.concatenate([sums[:HALO] * inv_top, sums[HALO:] * (1.0 / w)], axis=0)
        mixed = (mean - ext_ref[base:base + TM, cs]).astype(BF16)
        y = jnp.dot(mixed, poolw_ref[g], preferred_element_type=F32)
        pm_ref[:, cs] = (y * pscale_ref[:, cs]).astype(BF16)

    col_chunks = [slice(lo, lo + COL_CHUNK) for lo in range(0, D_MODEL, COL_CHUNK)]

    def gate_tanh(cols):
        z_half = jnp.dot(hb_ref[...], wg_ref[:, cols], preferred_element_type=F32)
        return jnp.tanh(_row_scale(z_half, rscale) + 0.5 * bg_ref[:, cols])

    for cs in col_chunks:
        y_half = jnp.dot(attn_ref[...], wba_ref[:, cs], preferred_element_type=F32)
        t1 = gate_tanh(slice(D_MODEL + cs.start, D_MODEL + cs.stop))
        ga_ref[:, cs] = y_half + t1 * y_half
    g0 = [gate_tanh(cs) for cs in col_chunks[:2]]
    for g in (3, 0, 1, 2):
        pool_group(g)

    for c, cs in enumerate(col_chunks):
        if c == len(g0):
            g0.append(gate_tanh(cs))
        y_half = jnp.dot(pm_ref[...], wbp_ref[:, cs], preferred_element_type=F32)
        mg_ref[:, cs] = (y_half + g0[c] * y_half + ga_ref[:, cs]).astype(BF16)
    for cs in col_chunks:
        ga_ref[:, cs] = x1_ref[:, cs] + jnp.dot(mg_ref[...], wout_ref[:, cs],
                                                preferred_element_type=F32)

    hb_ref[...], rscale2 = _norm_split(ga_ref[...], n2_ref[...])
    _swiglu_act(hb_ref, rscale2, wgu2_ref, act_ref)
    hm = TM // 2
    for rows in (slice(0, hm), slice(hm, TM)):
        for cs in col_chunks:
            out_ref[rows, cs] = ga_ref[rows, cs] + jnp.dot(act_ref[rows, :], wd_ref[:, cs],
                                                           preferred_element_type=F32)
        out_ref[rows, :] = _rmsnorm(out_ref[rows, :], nf_ref[...])


def _resident(shape):
    return pl.BlockSpec(shape, lambda i: (0,) * len(shape), pipeline_mode=pl.Buffered(1))


def _toeplitz(w):
    n_heads, width = w.shape
    flat = jnp.tile(jnp.pad(w, ((0, 0), (0, 1))), (1, CHUNK))
    rows = flat[:, :CHUNK * width].reshape(n_heads, CHUNK, width)
    return rows[:, :, CHUNK - 1:]


def _bias_pieces(rel_bias):
    assert MAX_REL == CHUNK
    rb = rel_bias.astype(F32) * LOG2E
    n_heads = rb.shape[0]
    far = rb[:, 2 * MAX_REL]
    t_own = _toeplitz(rb[:, 1:2 * MAX_REL][:, ::-1])
    t_prev = _toeplitz(jnp.concatenate(
        [jnp.broadcast_to(far[:, None], (n_heads, CHUNK)), rb[:, MAX_REL + 1:2 * MAX_REL][:, ::-1]],
        axis=1))
    full = (n_heads, CHUNK, LANES)
    return jnp.stack([jnp.full(full, MASK_VALUE, F32),
                      jnp.broadcast_to(far[:, None, None], full),
                      jnp.concatenate([t_prev, t_prev], axis=2),
                      jnp.concatenate([t_own, t_own], axis=2)], axis=1)


def kernel(x, ffn1_norm, ffn1_w_gate, ffn1_w_up, ffn1_w_down, mix_norm, w_in, pool_w, pool_scale,
           rel_bias, w_branch_pool, w_branch_attn, w_gate, b_gate, w_out, ffn2_norm, ffn2_w_gate,
           ffn2_w_up, ffn2_w_down, final_norm):
    b, s, d = x.shape
    assert d == D_MODEL and s % TM == 0 and ffn1_norm.shape[0] == 1
    n_tok = b * s
    n_tiles = n_tok // TM
    assert n_tiles == N_TILES
    tiles_per_seq = s // TM
    xf = x.reshape(n_tok, d)

    row = lambda v: v.reshape(1, -1).astype(F32)
    cparams = pltpu.CompilerParams(dimension_semantics=("arbitrary",),
                                   vmem_limit_bytes=VMEM_LIMIT)
    tile = lambda width: pl.BlockSpec((TM, width), lambda i: (i, 0))

    tok = lambda i: jnp.maximum(i - N_CAST, 0)
    tile1 = lambda width: pl.BlockSpec((TM, width), lambda i: (tok(i), 0))
    staged = lambda w: pl.BlockSpec((w.shape[0] // N_CAST, w.shape[1]),
                                    lambda i: (jnp.minimum(i, N_CAST - 1), 0))
    own_w = [ffn1_w_gate[0], ffn1_w_up[0], ffn1_w_down[0], w_in[0]]
    side_w = [w_gate[0], w_branch_pool[0], w_branch_attn[0], w_out[0], ffn2_w_down[0],
              pool_w[0].reshape(POOL_WIDTH, POOL_GROUP)]
    gate_up_rows = D_MODEL // n_tiles
    gate_up_spec = lambda width: pl.BlockSpec((gate_up_rows, width), lambda i: (tok(i), 0))

    def side_specs():
        specs = []
        for w, (rows, steps, _) in zip(side_w, SIDE_ROWS):
            assert w.shape[0] == rows
            specs.append(pl.BlockSpec(
                (rows // steps, w.shape[1]),
                functools.partial(lambda steps, i: (jnp.minimum(tok(i), steps - 1), 0), steps)))
        return specs

    res = pl.pallas_call(
        _ffn1_inproj_kernel,
        grid=(N_CAST + n_tiles,),
        in_specs=[tile1(D_MODEL), _resident((1, D_MODEL)), staged(own_w[0]), staged(own_w[1]),
                  staged(own_w[2]), _resident((1, D_MODEL)), staged(own_w[3]),
                  gate_up_spec(D_FF), gate_up_spec(D_FF)] + side_specs(),
        out_specs=[tile1(D_MODEL), tile1(POOL_WIDTH), tile1(3 * ATTN_WIDTH),
                   gate_up_spec(2 * D_FF)] + side_specs(),
        out_shape=[jax.ShapeDtypeStruct((n_tok, D_MODEL), F32),
                   jax.ShapeDtypeStruct((n_tok, POOL_WIDTH), F32),
                   jax.ShapeDtypeStruct((n_tok, 3 * ATTN_WIDTH), BF16),
                   jax.ShapeDtypeStruct((D_MODEL, 2 * D_FF), BF16)]
                  + [jax.ShapeDtypeStruct(w.shape, BF16) for w in side_w],
        scratch_shapes=[pltpu.VMEM((D_MODEL, 2 * D_FF), BF16), pltpu.VMEM((D_FF, D_MODEL), BF16),
                        pltpu.VMEM((D_MODEL, IN_WIDTH), BF16),
                        pltpu.VMEM((TM, D_MODEL), BF16), pltpu.VMEM((TM, D_FF), BF16)],
        compiler_params=cparams,
        name="ffn1_inproj",
    )(xf, row(ffn1_norm), own_w[0], own_w[1], own_w[2], row(mix_norm), own_w[3],
      ffn2_w_gate[0], ffn2_w_up[0], *side_w)
    x1, up, qkv, wgu2_b = res[:4]
    wgate_b, wbp_b, wba_b, wout_b, wd2_b, poolw_b = res[4:]

    assert s % ATTN_TM == 0 and ATTN_TM % ATTN_PREV == 0
    prev = lambda i: jnp.maximum(i * (ATTN_TM // ATTN_PREV) - 1, 0)
    cur_spec = lambda col: pl.BlockSpec((ATTN_TM, ATTN_WIDTH), lambda i: (i, col))
    prev_spec = lambda col: pl.BlockSpec((ATTN_PREV, ATTN_WIDTH), lambda i: (prev(i), col))
    attn = pl.pallas_call(
        functools.partial(_attn_kernel, s // ATTN_TM),
        grid=(n_tok // ATTN_TM,),
        in_specs=[cur_spec(0), prev_spec(1), cur_spec(1), prev_spec(2), cur_spec(2),
                  _resident((N_HEADS, 4, CHUNK, LANES))],
        out_specs=cur_spec(0),
        out_shape=jax.ShapeDtypeStruct((n_tok, ATTN_WIDTH), BF16),
        scratch_shapes=[pltpu.VMEM((1 + ATTN_PREV // Q_BLOCK, N_HEADS, Q_BLOCK, KEY_BLOCK), F32),
                        pltpu.VMEM((ATTN_PREV + ATTN_TM, ATTN_WIDTH), BF16),
                        pltpu.VMEM((ATTN_PREV + ATTN_TM, ATTN_WIDTH), BF16)],
        compiler_params=cparams,
        name="chunk_attn",
    )(qkv, qkv, qkv, qkv, qkv, _bias_pieces(rel_bias[0]))

    halo_blocks = TM // HALO
    out = pl.pallas_call(
        functools.partial(_mix_ffn2_kernel, tiles_per_seq),
        grid=(n_tiles,),
        in_specs=[tile(D_MODEL), tile(POOL_WIDTH),
                  pl.BlockSpec((HALO, POOL_WIDTH),
                               lambda i: (jnp.maximum(i * halo_blocks - 1, 0), 0)),
                  tile(ATTN_WIDTH), _resident((1, D_MODEL)),
                  _resident((D_MODEL, 2 * D_MODEL)), _resident((1, 2 * D_MODEL)),
                  _resident((len(POOL_WINDOWS), POOL_GROUP, POOL_GROUP)),
                  _resident((1, POOL_WIDTH)), _resident((POOL_WIDTH, D_MODEL)),
                  _resident((ATTN_WIDTH, D_MODEL)), _resident((D_MODEL, D_MODEL)),
                  _resident((1, D_MODEL)), _resident((D_MODEL, 2 * D_FF)),
                  _resident((D_FF, D_MODEL)), _resident((1, D_MODEL))],
        out_specs=tile(D_MODEL),
        out_shape=jax.ShapeDtypeStruct((n_tok, D_MODEL), F32),
        scratch_shapes=[pltpu.VMEM((TM, D_MODEL), BF16), pltpu.VMEM((TM, D_FF), BF16),
                        pltpu.VMEM((POOL_PAD + HALO + TM, POOL_WIDTH), F32),
                        pltpu.VMEM((POOL_PAD + HALO + TM, POOL_WIDTH), F32),
                        pltpu.VMEM((POOL_PAD + HALO + TM, POOL_WIDTH), F32),
                        pltpu.VMEM((TM, POOL_WIDTH), BF16), pltpu.VMEM((TM, D_MODEL), BF16),
                        pltpu.VMEM((TM, D_MODEL), F32)],
        compiler_params=cparams,
        name="mix_ffn2",
    )(x1, up, up, attn, row(mix_norm), wgate_b, row(b_gate),
      poolw_b.reshape(len(POOL_WINDOWS), POOL_GROUP, POOL_GROUP), row(pool_scale), wbp_b, wba_b,
      wout_b, row(ffn2_norm), wgu2_b, wd2_b, row(final_norm))
    return out.reshape(b, s, d)
```

```python
import functools

import jax
import jax.numpy as jnp
from jax import lax
from jax.experimental import pallas as pl
from jax.experimental.pallas import tpu as pltpu

D_MODEL = 1024
D_FF = 2816
CHUNK = 64
LEFT_CHUNKS = 8
POOL_WIDTH = 512
POOL_WINDOWS = (2, 4, 8, 16)
POOL_GROUP = 128
N_HEADS = 8
HEAD_DIM = 64
ATTN_WIDTH = 512
MAX_REL = 64
IN_WIDTH = POOL_WIDTH + 3 * ATTN_WIDTH
EPS = 1e-6

LANES = 128
TM = 512
FF_CHUNK = 256
COL_CHUNK = 256
Q_BLOCK = 256
ATTN_PREV = LEFT_CHUNKS * CHUNK
ATTN_TM = 1024
KEY_BLOCK = Q_BLOCK + ATTN_PREV
SCORE_LOOKAHEAD = 2
HALO = 16
POOL_PAD = 8
N_CAST = 8
N_TILES = 32
SIDE_ROWS = ((D_MODEL, N_TILES, 0.5),
             (POOL_WIDTH, N_TILES, 0.5),
             (ATTN_WIDTH, N_TILES, 0.5),
             (D_MODEL, N_TILES, 1.0),
             (D_FF, N_TILES // 2, 0.5),
             (POOL_WIDTH, N_TILES, 1.0))
MASK_VALUE = -1e30
LOG2E = 1.4426950408889634
Q_SCALE = HEAD_DIM ** -0.5 * LOG2E
VMEM_LIMIT = 56 * 1024 * 1024

F32 = jnp.float32
BF16 = jnp.bfloat16


def _rmsnorm(x, g):
    ms = jnp.mean(x * x, axis=-1, keepdims=True)
    return x * lax.rsqrt(ms + EPS) * g


def _norm_split(x, g):
    rscale = lax.rsqrt(jnp.mean(x * x, axis=-1, keepdims=True) + EPS)
    return (x * g).astype(BF16), jnp.broadcast_to(rscale, (x.shape[0], LANES))


def _row_scale(y, rscale):
    return jnp.concatenate([y[:, lo:lo + LANES] * rscale for lo in range(0, y.shape[1], LANES)],
                           axis=1)


def _interleave_gate_up(dst_ref, rows, gate_blk, up_blk):
    for c in range(D_FF // FF_CHUNK):
        src = slice(c * FF_CHUNK, (c + 1) * FF_CHUNK)
        dst_ref[rows, 2 * c * FF_CHUNK:(2 * c + 1) * FF_CHUNK] = gate_blk[:, src].astype(BF16)
        dst_ref[rows, (2 * c + 1) * FF_CHUNK:(2 * c + 2) * FF_CHUNK] = up_blk[:, src].astype(BF16)


def _swiglu_act(hb_ref, rscale, wgu_ref, act_ref):
    half_rscale = 0.5 * rscale
    for c in range(D_FF // FF_CHUNK):
        gu = jnp.dot(hb_ref[...], wgu_ref[:, 2 * c * FF_CHUNK:(2 * c + 2) * FF_CHUNK],
                     preferred_element_type=F32)
        hg = _row_scale(gu[:, :FF_CHUNK], half_rscale)
        up = _row_scale(gu[:, FF_CHUNK:], rscale)
        act_ref[:, c * FF_CHUNK:(c + 1) * FF_CHUNK] = ((hg + hg * jnp.tanh(hg)) * up).astype(BF16)


def _ffn1_inproj_kernel(x_ref, n1_ref, wg_blk, wu_blk, wd_blk, nm_ref, win_blk,
                        wg2_blk, wu2_blk, *rest):
    n_side = len(SIDE_ROWS)
    side_in = rest[:n_side]
    x1_ref, up_ref, qkv_ref, wgu2_out = rest[n_side:n_side + 4]
    side_out = rest[n_side + 4:2 * n_side + 4]
    wgu_ref, wd_ref, win_ref, hb_ref, act_ref = rest[2 * n_side + 4:]
    i = pl.program_id(0)

    @pl.when(i < N_CAST)
    def _stage_weights():
        def rows_of(blk):
            return pl.ds(pl.multiple_of(i * blk.shape[0], blk.shape[0]), blk.shape[0])

        _interleave_gate_up(wgu_ref, rows_of(wg_blk), wg_blk, wu_blk)
        wd_ref[rows_of(wd_blk), :] = (wd_blk[...] * 0.5).astype(BF16)
        win_ref[rows_of(win_blk), :] = win_blk[...].astype(BF16)

    @pl.when(i >= N_CAST)
    def _token_tile():
        hb_ref[...], rscale = _norm_split(x_ref[...], n1_ref[...])
        _swiglu_act(hb_ref, rscale, wgu_ref, act_ref)
        for lo in range(0, D_MODEL, COL_CHUNK):
            cs = slice(lo, lo + COL_CHUNK)
            x1_ref[:, cs] = x_ref[:, cs] + jnp.dot(act_ref[...], wd_ref[:, cs],
                                                   preferred_element_type=F32)
        hb_ref[...], rscale = _norm_split(x1_ref[...], nm_ref[...])
        q_rscale = rscale * Q_SCALE
        for lo in range(0, IN_WIDTH, COL_CHUNK):
            y = jnp.dot(hb_ref[...], win_ref[:, lo:lo + COL_CHUNK], preferred_element_type=F32)
            if lo < POOL_WIDTH:
                up_ref[:, lo:lo + COL_CHUNK] = _row_scale(y, rscale)
            else:
                is_q = lo < POOL_WIDTH + ATTN_WIDTH
                qkv_ref[:, lo - POOL_WIDTH:lo - POOL_WIDTH + COL_CHUNK] = _row_scale(
                    y, q_rscale if is_q else rscale).astype(BF16)
        _interleave_gate_up(wgu2_out, slice(None), wg2_blk, wu2_blk)
        for src, dst, (_, _, scale) in zip(side_in, side_out, SIDE_ROWS):
            w = src[...] if scale == 1.0 else src[...] * scale
            dst[...] = w.astype(BF16)


def _band_kind(dchunk):
    if dchunk < 0 or dchunk > LEFT_CHUNKS:
        return 0
    if dchunk < LEFT_CHUNKS - 1:
        return 1
    return 2 if dchunk == LEFT_CHUNKS - 1 else 3


def _attn_kernel(tiles_per_seq, q_ref, kp_ref, kc_ref, vp_ref, vc_ref, pieces_ref,
                 o_ref, bias_ref, v_even_ref, v_odd_ref):
    i = pl.program_id(0)
    is_first = (i % tiles_per_seq) == 0
    n_first_blocks = ATTN_PREV // Q_BLOCK

    @pl.when(i == 0)
    def _build_bias_tables():
        low = lax.broadcasted_iota(jnp.int32, (CHUNK, LANES), 1) < CHUNK

        def per_head(h, carry):
            for variant in range(1 + n_first_blocks):
                n_before = 0 if variant == 0 else (ATTN_PREV - (variant - 1) * Q_BLOCK) // LANES
                for qc in range(Q_BLOCK // CHUNK):
                    for kp in range(KEY_BLOCK // LANES):
                        ka, kb = _band_kind(2 * kp - qc), _band_kind(2 * kp + 1 - qc)
                        if kp < n_before:
                            ka = kb = 0
                        blk = pieces_ref[h, ka]
                        if kb != ka:
                            blk = jnp.where(low, blk, pieces_ref[h, kb])
                        bias_ref[variant, h, qc * CHUNK:(qc + 1) * CHUNK,
                                 kp * LANES:(kp + 1) * LANES] = blk
            return carry

        lax.fori_loop(0, N_HEADS, per_head, 0)

    for src, row0 in ((vp_ref, 0), (vc_ref, ATTN_PREV)):
        v = src[...]
        even_lanes = (lax.broadcasted_iota(jnp.int32, v.shape, 1) % LANES) < HEAD_DIM
        ones = jnp.ones(v.shape, BF16)
        v_even_ref[row0:row0 + v.shape[0], :] = jnp.where(even_lanes, v, ones)
        v_odd_ref[row0:row0 + v.shape[0], :] = jnp.where(even_lanes, ones, v)

    low_half = lax.broadcasted_iota(jnp.int32, (Q_BLOCK, LANES), 1) < HEAD_DIM
    units = [(sb, p, par) for sb in range(ATTN_TM // Q_BLOCK) for p in range(N_HEADS // 2)
             for par in range(2)]

    def scores(unit):
        sb, p, par = unit
        r0, cs = sb * Q_BLOCK, slice(p * LANES, (p + 1) * LANES)
        qp = q_ref[r0:r0 + Q_BLOCK, cs]
        qm = jnp.where(low_half if par == 0 else jnp.logical_not(low_half), qp, jnp.zeros_like(qp))
        nt = (((1,), (1,)), ((), ()))
        parts = []
        if r0 < ATTN_PREV:
            parts.append(lax.dot_general(qm, kp_ref[r0:ATTN_PREV, cs], nt,
                                         preferred_element_type=F32))
        parts.append(lax.dot_general(
            qm, kc_ref[max(r0 - ATTN_PREV, 0):r0 + KEY_BLOCK - ATTN_PREV, cs], nt,
            preferred_element_type=F32))
        s = parts[0] if len(parts) == 1 else jnp.concatenate(parts, axis=1)
        variant = jnp.where(is_first, 1 + sb, 0) if sb < n_first_blocks else 0
        return s + bias_ref[variant, 2 * p + par]

    pending = [scores(u) for u in units[:SCORE_LOOKAHEAD]]
    outs = []
    for n, (sb, p, par) in enumerate(units):
        r0, cs = sb * Q_BLOCK, slice(p * LANES, (p + 1) * LANES)
        s = pending.pop(0)
        e = jnp.exp2(s - jnp.max(s, axis=-1, keepdims=True))
        if n + SCORE_LOOKAHEAD < len(units):
            pending.append(scores(units[n + SCORE_LOOKAHEAD]))
        v_ref = v_even_ref if par == 0 else v_odd_ref
        o = jnp.dot(e.astype(BF16), v_ref[r0:r0 + KEY_BLOCK, cs], preferred_element_type=F32)
        outs.append(o / pltpu.roll(o, HEAD_DIM, axis=1))
        if par == 1:
            o_ref[r0:r0 + Q_BLOCK, cs] = jnp.where(low_half, outs[-2], outs[-1]).astype(BF16)


def _mix_ffn2_kernel(tiles_per_seq, x1_ref, up_ref, halo_ref, attn_ref, nm_ref, wg_ref,
                     bg_ref, poolw_ref, pscale_ref, wbp_ref, wba_ref, wout_ref,
                     n2_ref, wgu2_ref, wd_ref, nf_ref,
                     out_ref, hb_ref, act_ref, ext_ref, lvl_a_ref, lvl_b_ref, pm_ref, mg_ref, ga_ref):
    i = pl.program_id(0)
    tile_in_seq = i % tiles_per_seq
    hb_ref[...], rscale = _norm_split(x1_ref[...], nm_ref[...])

    base = POOL_PAD + HALO
    keep_halo = jnp.where(tile_in_seq == 0, 0.0, 1.0).astype(F32)
    lvl_refs = (lvl_a_ref, lvl_b_ref)
    for ref in (ext_ref,) + lvl_refs:
        ref[0:POOL_PAD, :] = jnp.zeros((POOL_PAD, POOL_WIDTH), F32)
    ext_ref[POOL_PAD:base, :] = halo_ref[...] * keep_halo
    ext_ref[base:base + TM, :] = up_ref[...]
    pos_top = tile_in_seq * TM + lax.broadcasted_iota(jnp.int32, (HALO, 1), 0)

    def pool_group(g):
        w = POOL_WINDOWS[g]
        cs = slice(g * POOL_GROUP, (g + 1) * POOL_GROUP)
        src, shift, level = ext_ref, 1, 0
        while 2 * shift < w:
            lvl_ref = lvl_refs[level % 2]
            lvl_ref[POOL_PAD:base + TM, cs] = (src[POOL_PAD:base + TM, cs]
                                              + src[POOL_PAD - shift:base + TM - shift, cs])
            src, shift, level = lvl_ref, 2 * shift, level + 1
        sums = src[base:base + TM, cs] + src[base - shift:base + TM - shift, cs]
        inv_top = 1.0 / jnp.minimum(pos_top + 1, w).astype(F32)
        mean = jnp.concatenate([sums[:HALO] * inv_top, sums[HALO:] * (1.0 / w)], axis=0)
        mixed = (mean - ext_ref[base:base + TM, cs]).astype(BF16)
        y = jnp.dot(mixed, poolw_ref[g], preferred_element_type=F32)
        pm_ref[:, cs] = (y * pscale_ref[:, cs]).astype(BF16)

    col_chunks = [slice(lo, lo + COL_CHUNK) for lo in range(0, D_MODEL, COL_CHUNK)]

    def gate_tanh(cols):
        z_half = jnp.dot(hb_ref[...], wg_ref[:, cols], preferred_element_type=F32)
        return jnp.tanh(_row_scale(z_half, rscale) + 0.5 * bg_ref[:, cols])

    for cs in col_chunks:
        y_half = jnp.dot(attn_ref[...], wba_ref[:, cs], preferred_element_type=F32)
        t1 = gate_tanh(slice(D_MODEL + cs.start, D_MODEL + cs.stop))
        ga_ref[:, cs] = y_half + t1 * y_half
    g0 = [gate_tanh(cs) for cs in col_chunks[:2]]
    for g in (3, 0, 1, 2):
        pool_group(g)

    for c, cs in enumerate(col_chunks):
        if c == len(g0):
            g0.append(gate_tanh(cs))
        y_half = jnp.dot(pm_ref[...], wbp_ref[:, cs], preferred_element_type=F32)
        mg_ref[:, cs] = (y_half + g0[c] * y_half + ga_ref[:, cs]).astype(BF16)
    for cs in col_chunks:
        ga_ref[:, cs] = x1_ref[:, cs] + jnp.dot(mg_ref[...], wout_ref[:, cs],
                                                preferred_element_type=F32)

    hb_ref[...], rscale2 = _norm_split(ga_ref[...], n2_ref[...])
    _swiglu_act(hb_ref, rscale2, wgu2_ref, act_ref)
    hm = TM // 2
    for rows in (slice(0, hm), slice(hm, TM)):
        for cs in col_chunks:
            out_ref[rows, cs] = ga_ref[rows, cs] + jnp.dot(act_ref[rows, :], wd_ref[:, cs],
                                                           preferred_element_type=F32)
        out_ref[rows, :] = _rmsnorm(out_ref[rows, :], nf_ref[...])


def _resident(shape):
    return pl.BlockSpec(shape, lambda i: (0,) * len(shape), pipeline_mode=pl.Buffered(1))


def _toeplitz(w):
    n_heads, width = w.shape
    flat = jnp.tile(jnp.pad(w, ((0, 0), (0, 1))), (1, CHUNK))
    rows = flat[:, :CHUNK * width].reshape(n_heads, CHUNK, width)
    return rows[:, :, CHUNK - 1:]


def _bias_pieces(rel_bias):
    assert MAX_REL == CHUNK
    rb = rel_bias.astype(F32) * LOG2E
    n_heads = rb.shape[0]
    far = rb[:, 2 * MAX_REL]
    t_own = _toeplitz(rb[:, 1:2 * MAX_REL][:, ::-1])
    t_prev = _toeplitz(jnp.concatenate(
        [jnp.broadcast_to(far[:, None], (n_heads, CHUNK)), rb[:, MAX_REL + 1:2 * MAX_REL][:, ::-1]],
        axis=1))
    full = (n_heads, CHUNK, LANES)
    return jnp.stack([jnp.full(full, MASK_VALUE, F32),
                      jnp.broadcast_to(far[:, None, None], full),
                      jnp.concatenate([t_prev, t_prev], axis=2),
                      jnp.concatenate([t_own, t_own], axis=2)], axis=1)


def kernel(x, ffn1_norm, ffn1_w_gate, ffn1_w_up, ffn1_w_down, mix_norm, w_in, pool_w, pool_scale,
           rel_bias, w_branch_pool, w_branch_attn, w_gate, b_gate, w_out, ffn2_norm, ffn2_w_gate,
           ffn2_w_up, ffn2_w_down, final_norm):
    b, s, d = x.shape
    assert d == D_MODEL and s % TM == 0 and ffn1_norm.shape[0] == 1
    n_tok = b * s
    n_tiles = n_tok // TM
    assert n_tiles == N_TILES
    tiles_per_seq = s // TM
    xf = x.reshape(n_tok, d)

    row = lambda v: v.reshape(1, -1).astype(F32)
    cparams = pltpu.CompilerParams(dimension_semantics=("arbitrary",),
                                   vmem_limit_bytes=VMEM_LIMIT)
    tile = lambda width: pl.BlockSpec((TM, width), lambda i: (i, 0))

    tok = lambda i: jnp.maximum(i - N_CAST, 0)
    tile1 = lambda width: pl.BlockSpec((TM, width), lambda i: (tok(i), 0))
    staged = lambda w: pl.BlockSpec((w.shape[0] // N_CAST, w.shape[1]),
                                    lambda i: (jnp.minimum(i, N_CAST - 1), 0))
    own_w = [ffn1_w_gate[0], ffn1_w_up[0], ffn1_w_down[0], w_in[0]]
    side_w = [w_gate[0], w_branch_pool[0], w_branch_attn[0], w_out[0], ffn2_w_down[0],
              pool_w[0].reshape(POOL_WIDTH, POOL_GROUP)]
    gate_up_rows = D_MODEL // n_tiles
    gate_up_spec = lambda width: pl.BlockSpec((gate_up_rows, width), lambda i: (tok(i), 0))

    def side_specs():
        specs = []
        for w, (rows, steps, _) in zip(side_w, SIDE_ROWS):
            assert w.shape[0] == rows
            specs.append(pl.BlockSpec(
                (rows // steps, w.shape[1]),
                functools.partial(lambda steps, i: (jnp.minimum(tok(i), steps - 1), 0), steps)))
        return specs

    res = pl.pallas_call(
        _ffn1_inproj_kernel,
        grid=(N_CAST + n_tiles,),
        in_specs=[tile1(D_MODEL), _resident((1, D_MODEL)), staged(own_w[0]), staged(own_w[1]),
                  staged(own_w[2]), _resident((1, D_MODEL)), staged(own_w[3]),
                  gate_up_spec(D_FF), gate_up_spec(D_FF)] + side_specs(),
        out_specs=[tile1(D_MODEL), tile1(POOL_WIDTH), tile1(3 * ATTN_WIDTH),
                   gate_up_spec(2 * D_FF)] + side_specs(),
        out_shape=[jax.ShapeDtypeStruct((n_tok, D_MODEL), F32),
                   jax.ShapeDtypeStruct((n_tok, POOL_WIDTH), F32),
                   jax.ShapeDtypeStruct((n_tok, 3 * ATTN_WIDTH), BF16),
                   jax.ShapeDtypeStruct((D_MODEL, 2 * D_FF), BF16)]
                  + [jax.ShapeDtypeStruct(w.shape, BF16) for w in side_w],
        scratch_shapes=[pltpu.VMEM((D_MODEL, 2 * D_FF), BF16), pltpu.VMEM((D_FF, D_MODEL), BF16),
                        pltpu.VMEM((D_MODEL, IN_WIDTH), BF16),
                        pltpu.VMEM((TM, D_MODEL), BF16), pltpu.VMEM((TM, D_FF), BF16)],
        compiler_params=cparams,
        name="ffn1_inproj",
    )(xf, row(ffn1_norm), own_w[0], own_w[1], own_w[2], row(mix_norm), own_w[3],
      ffn2_w_gate[0], ffn2_w_up[0], *side_w)
    x1, up, qkv, wgu2_b = res[:4]
    wgate_b, wbp_b, wba_b, wout_b, wd2_b, poolw_b = res[4:]

    assert s % ATTN_TM == 0 and ATTN_TM % ATTN_PREV == 0
    prev = lambda i: jnp.maximum(i * (ATTN_TM // ATTN_PREV) - 1, 0)
    cur_spec = lambda col: pl.BlockSpec((ATTN_TM, ATTN_WIDTH), lambda i: (i, col))
    prev_spec = lambda col: pl.BlockSpec((ATTN_PREV, ATTN_WIDTH), lambda i: (prev(i), col))
    attn = pl.pallas_call(
        functools.partial(_attn_kernel, s // ATTN_TM),
        grid=(n_tok // ATTN_TM,),
        in_specs=[cur_spec(0), prev_spec(1), cur_spec(1), prev_spec(2), cur_spec(2),
                  _resident((N_HEADS, 4, CHUNK, LANES))],
        out_specs=cur_spec(0),
        out_shape=jax.ShapeDtypeStruct((n_tok, ATTN_WIDTH), BF16),
        scratch_shapes=[pltpu.VMEM((1 + ATTN_PREV // Q_BLOCK, N_HEADS, Q_BLOCK, KEY_BLOCK), F32),
                        pltpu.VMEM((ATTN_PREV + ATTN_TM, ATTN_WIDTH), BF16),
                        pltpu.VMEM((ATTN_PREV + ATTN_TM, ATTN_WIDTH), BF16)],
        compiler_params=cparams,
        name="chunk_attn",
    )(qkv, qkv, qkv, qkv, qkv, _bias_pieces(rel_bias[0]))

    halo_blocks = TM // HALO
    out = pl.pallas_call(
        functools.partial(_mix_ffn2_kernel, tiles_per_seq),
        grid=(n_tiles,),
        in_specs=[tile(D_MODEL), tile(POOL_WIDTH),
                  pl.BlockSpec((HALO, POOL_WIDTH),
                               lambda i: (jnp.maximum(i * halo_blocks - 1, 0), 0)),
                  tile(ATTN_WIDTH), _resident((1, D_MODEL)),
                  _resident((D_MODEL, 2 * D_MODEL)), _resident((1, 2 * D_MODEL)),
                  _resident((len(POOL_WINDOWS), POOL_GROUP, POOL_GROUP)),
                  _resident((1, POOL_WIDTH)), _resident((POOL_WIDTH, D_MODEL)),
                  _resident((ATTN_WIDTH, D_MODEL)), _resident((D_MODEL, D_MODEL)),
                  _resident((1, D_MODEL)), _resident((D_MODEL, 2 * D_FF)),
                  _resident((D_FF, D_MODEL)), _resident((1, D_MODEL))],
        out_specs=tile(D_MODEL),
        out_shape=jax.ShapeDtypeStruct((n_tok, D_MODEL), F32),
        scratch_shapes=[pltpu.VMEM((TM, D_MODEL), BF16), pltpu.VMEM((TM, D_FF), BF16),
                        pltpu.VMEM((POOL_PAD + HALO + TM, POOL_WIDTH), F32),
                        pltpu.VMEM((POOL_PAD + HALO + TM, POOL_WIDTH), F32),
                        pltpu.VMEM((POOL_PAD + HALO + TM, POOL_WIDTH), F32),
                        pltpu.VMEM((TM, POOL_WIDTH), BF16), pltpu.VMEM((TM, D_MODEL), BF16),
                        pltpu.VMEM((TM, D_MODEL), F32)],
        compiler_params=cparams,
        name="mix_ffn2",
    )(x1, up, up, attn, row(mix_norm), wgate_b, row(b_gate),
      poolw_b.reshape(len(POOL_WINDOWS), POOL_GROUP, POOL_GROUP), row(pool_scale), wbp_b, wba_b,
      wout_b, row(ffn2_norm), wgu2_b, wd2_b, row(final_norm))
    return out.reshape(b, s, d)
```

```python
import functools

import jax
import jax.numpy as jnp
from jax import lax
from jax.experimental import pallas as pl
from jax.experimental.pallas import tpu as pltpu

D_MODEL = 1024
D_FF = 2816
CHUNK = 64
LEFT_CHUNKS = 8
POOL_WIDTH = 512
POOL_WINDOWS = (2, 4, 8, 16)
POOL_GROUP = 128
N_HEADS = 8
HEAD_DIM = 64
ATTN_WIDTH = 512
MAX_REL = 64
IN_WIDTH = POOL_WIDTH + 3 * ATTN_WIDTH
EPS = 1e-6

LANES = 128
TM = 512
FF_CHUNK = 256
COL_CHUNK = 256
Q_BLOCK = 256
ATTN_PREV = LEFT_CHUNKS * CHUNK
ATTN_TM = 1024
KEY_BLOCK = Q_BLOCK + ATTN_PREV
SCORE_LOOKAHEAD = 2
HALO = 16
POOL_PAD = 8
N_CAST = 8
N_TILES = 32
SIDE_ROWS = ((D_MODEL, N_TILES, 0.5),
             (POOL_WIDTH, N_TILES, 0.5),
             (ATTN_WIDTH, N_TILES, 0.5),
             (D_MODEL, N_TILES, 1.0),
             (D_FF, N_TILES // 2, 0.5),
             (POOL_WIDTH, N_TILES, 1.0))
MASK_VALUE = -2.0 ** 126
LOG2E = 1.4426950408889634
Q_SCALE = HEAD_DIM ** -0.5 * LOG2E
VMEM_LIMIT = 56 * 1024 * 1024

F32 = jnp.float32
BF16 = jnp.bfloat16


def _rmsnorm(x, g):
    ms = jnp.mean(x * x, axis=-1, keepdims=True)
    return x * lax.rsqrt(ms + EPS) * g


def _norm_split(x, g):
    rscale = lax.rsqrt(jnp.mean(x * x, axis=-1, keepdims=True) + EPS)
    return (x * g).astype(BF16), jnp.broadcast_to(rscale, (x.shape[0], LANES))


def _row_scale(y, rscale):
    return jnp.concatenate([y[:, lo:lo + LANES] * rscale for lo in range(0, y.shape[1], LANES)],
                           axis=1)


def _interleave_gate_up(dst_ref, rows, gate_blk, up_blk):
    for c in range(D_FF // FF_CHUNK):
        src = slice(c * FF_CHUNK, (c + 1) * FF_CHUNK)
        dst_ref[rows, 2 * c * FF_CHUNK:(2 * c + 1) * FF_CHUNK] = gate_blk[:, src].astype(BF16)
        dst_ref[rows, (2 * c + 1) * FF_CHUNK:(2 * c + 2) * FF_CHUNK] = up_blk[:, src].astype(BF16)


def _swiglu_act(hb_ref, rscale, wgu_ref, act_ref):
    half_rscale = 0.5 * rscale
    for c in range(D_FF // FF_CHUNK):
        gu = jnp.dot(hb_ref[...], wgu_ref[:, 2 * c * FF_CHUNK:(2 * c + 2) * FF_CHUNK],
                     preferred_element_type=F32)
        hg = _row_scale(gu[:, :FF_CHUNK], half_rscale)
        up = _row_scale(gu[:, FF_CHUNK:], rscale)
        act_ref[:, c * FF_CHUNK:(c + 1) * FF_CHUNK] = ((hg + hg * jnp.tanh(hg)) * up).astype(BF16)


def _ffn1_inproj_kernel(x_ref, n1_ref, wg_blk, wu_blk, wd_blk, nm_ref, win_blk,
                        wg2_blk, wu2_blk, *rest):
    n_side = len(SIDE_ROWS)
    side_in = rest[:n_side]
    x1_ref, up_ref, qkv_ref, wgu2_out = rest[n_side:n_side + 4]
    side_out = rest[n_side + 4:2 * n_side + 4]
    wgu_ref, wd_ref, win_ref, hb_ref, act_ref = rest[2 * n_side + 4:]
    i = pl.program_id(0)

    @pl.when(i < N_CAST)
    def _stage_weights():
        def rows_of(blk):
            return pl.ds(pl.multiple_of(i * blk.shape[0], blk.shape[0]), blk.shape[0])

        _interleave_gate_up(wgu_ref, rows_of(wg_blk), wg_blk, wu_blk)
        wd_ref[rows_of(wd_blk), :] = (wd_blk[...] * 0.5).astype(BF16)
        win_ref[rows_of(win_blk), :] = win_blk[...].astype(BF16)

    @pl.when(i >= N_CAST)
    def _token_tile():
        hb_ref[...], rscale = _norm_split(x_ref[...], n1_ref[...])
        _swiglu_act(hb_ref, rscale, wgu_ref, act_ref)
        for lo in range(0, D_MODEL, COL_CHUNK):
            cs = slice(lo, lo + COL_CHUNK)
            x1_ref[:, cs] = x_ref[:, cs] + jnp.dot(act_ref[...], wd_ref[:, cs],
                                                   preferred_element_type=F32)
        hb_ref[...], rscale = _norm_split(x1_ref[...], nm_ref[...])
        q_rscale = rscale * Q_SCALE
        for lo in range(0, IN_WIDTH, COL_CHUNK):
            y = jnp.dot(hb_ref[...], win_ref[:, lo:lo + COL_CHUNK], preferred_element_type=F32)
            if lo < POOL_WIDTH:
                up_ref[:, lo:lo + COL_CHUNK] = _row_scale(y, rscale)
            else:
                is_q = lo < POOL_WIDTH + ATTN_WIDTH
                qkv_ref[:, lo - POOL_WIDTH:lo - POOL_WIDTH + COL_CHUNK] = _row_scale(
                    y, q_rscale if is_q else rscale).astype(BF16)
        _interleave_gate_up(wgu2_out, slice(None), wg2_blk, wu2_blk)
        for src, dst, (_, _, scale) in zip(side_in, side_out, SIDE_ROWS):
            w = src[...] if scale == 1.0 else src[...] * scale
            dst[...] = w.astype(BF16)


def _band_kind(dchunk):
    if dchunk < 0 or dchunk > LEFT_CHUNKS:
        return 0
    if dchunk < LEFT_CHUNKS - 1:
        return 1
    return 2 if dchunk == LEFT_CHUNKS - 1 else 3


def _attn_kernel(tiles_per_seq, q_ref, kp_ref, kc_ref, vp_ref, vc_ref, pieces_ref,
                 o_ref, bias_ref, v_even_ref, v_odd_ref):
    i = pl.program_id(0)
    is_first = (i % tiles_per_seq) == 0
    n_first_blocks = ATTN_PREV // Q_BLOCK

    @pl.when(i == 0)
    def _build_bias_tables():
        low = lax.broadcasted_iota(jnp.int32, (CHUNK, LANES), 1) < CHUNK

        def per_head(h, carry):
            for variant in range(1 + n_first_blocks):
                n_before = 0 if variant == 0 else (ATTN_PREV - (variant - 1) * Q_BLOCK) // LANES
                for qc in range(Q_BLOCK // CHUNK):
                    for kp in range(KEY_BLOCK // LANES):
                        ka, kb = _band_kind(2 * kp - qc), _band_kind(2 * kp + 1 - qc)
                        if kp < n_before:
                            ka = kb = 0
                        blk = pieces_ref[h, ka]
                        if kb != ka:
                            blk = jnp.where(low, blk, pieces_ref[h, kb])
                        bias_ref[variant, h, qc * CHUNK:(qc + 1) * CHUNK,
                                 kp * LANES:(kp + 1) * LANES] = blk
            return carry

        lax.fori_loop(0, N_HEADS, per_head, 0)

    for src, row0 in ((vp_ref, 0), (vc_ref, ATTN_PREV)):
        v = src[...]
        even_lanes = (lax.broadcasted_iota(jnp.int32, v.shape, 1) % LANES) < HEAD_DIM
        ones = jnp.ones(v.shape, BF16)
        v_even_ref[row0:row0 + v.shape[0], :] = jnp.where(even_lanes, v, ones)
        v_odd_ref[row0:row0 + v.shape[0], :] = jnp.where(even_lanes, ones, v)

    low_half = lax.broadcasted_iota(jnp.int32, (Q_BLOCK, LANES), 1) < HEAD_DIM
    units = [(sb, p, par) for sb in range(ATTN_TM // Q_BLOCK) for p in range(N_HEADS // 2)
             for par in range(2)]

    def scores(unit):
        sb, p, par = unit
        r0, cs = sb * Q_BLOCK, slice(p * LANES, (p + 1) * LANES)
        qp = q_ref[r0:r0 + Q_BLOCK, cs]
        qm = jnp.where(low_half if par == 0 else jnp.logical_not(low_half), qp, jnp.zeros_like(qp))
        nt = (((1,), (1,)), ((), ()))
        parts = []
        if r0 < ATTN_PREV:
            parts.append(lax.dot_general(qm, kp_ref[r0:ATTN_PREV, cs], nt,
                                         preferred_element_type=F32))
        parts.append(lax.dot_general(
            qm, kc_ref[max(r0 - ATTN_PREV, 0):r0 + KEY_BLOCK - ATTN_PREV, cs], nt,
            preferred_element_type=F32))
        s = parts[0] if len(parts) == 1 else jnp.concatenate(parts, axis=1)
        variant = jnp.where(is_first, 1 + sb, 0) if sb < n_first_blocks else 0
        return s + bias_ref[variant, 2 * p + par]

    pending = [scores(u) for u in units[:SCORE_LOOKAHEAD]]
    outs = []
    for n, (sb, p, par) in enumerate(units):
        r0, cs = sb * Q_BLOCK, slice(p * LANES, (p + 1) * LANES)
        s = pending.pop(0)
        e = jnp.exp2(s - jnp.max(s, axis=-1, keepdims=True))
        if n + SCORE_LOOKAHEAD < len(units):
            pending.append(scores(units[n + SCORE_LOOKAHEAD]))
        v_ref = v_even_ref if par == 0 else v_odd_ref
        o = jnp.dot(e.astype(BF16), v_ref[r0:r0 + KEY_BLOCK, cs], preferred_element_type=F32)
        outs.append(o / pltpu.roll(o, HEAD_DIM, axis=1))
        if par == 1:
            o_ref[r0:r0 + Q_BLOCK, cs] = jnp.where(low_half, outs[-2], outs[-1]).astype(BF16)


def _mix_ffn2_kernel(tiles_per_seq, x1_ref, up_ref, halo_ref, attn_ref, nm_ref, wg_ref,
                     bg_ref, poolw_ref, pscale_ref, wbp_ref, wba_ref, wout_ref,
                     n2_ref, wgu2_ref, wd_ref, nf_ref,
                     out_ref, hb_ref, act_ref, ext_ref, lvl_a_ref, lvl_b_ref, pm_ref, mg_ref, ga_ref):
    i = pl.program_id(0)
    tile_in_seq = i % tiles_per_seq
    hb_ref[...], rscale = _norm_split(x1_ref[...], nm_ref[...])

    base = POOL_PAD + HALO
    keep_halo = jnp.where(tile_in_seq == 0, 0.0, 1.0).astype(F32)
    lvl_refs = (lvl_a_ref, lvl_b_ref)
    for ref in (ext_ref,) + lvl_refs:
        ref[0:POOL_PAD, :] = jnp.zeros((POOL_PAD, POOL_WIDTH), F32)
    ext_ref[POOL_PAD:base, :] = halo_ref[...] * keep_halo
    ext_ref[base:base + TM, :] = up_ref[...]
    pos_top = tile_in_seq * TM + lax.broadcasted_iota(jnp.int32, (HALO, 1), 0)

    def pool_group(g):
        w = POOL_WINDOWS[g]
        cs = slice(g * POOL_GROUP, (g + 1) * POOL_GROUP)
        src, shift, level = ext_ref, 1, 0
        while 2 * shift < w:
            lvl_ref = lvl_refs[level % 2]
            lvl_ref[POOL_PAD:base + TM, cs] = (src[POOL_PAD:base + TM, cs]
                                              + src[POOL_PAD - shift:base + TM - shift, cs])
            src, shift, level = lvl_ref, 2 * shift, level + 1
        sums = src[base:base + TM, cs] + src[base - shift:base + TM - shift, cs]
        inv_top = 1.0 / jnp.minimum(pos_top + 1, w).astype(F32)
        mean = jnp.concatenate([sums[:HALO] * inv_top, sums[HALO:] * (1.0 / w)], axis=0)
        mixed = (mean - ext_ref[base:base + TM, cs]).astype(BF16)
        y = jnp.dot(mixed, poolw_ref[g], preferred_element_type=F32)
        pm_ref[:, cs] = (y * pscale_ref[:, cs]).astype(BF16)

    col_chunks = [slice(lo, lo + COL_CHUNK) for lo in range(0, D_MODEL, COL_CHUNK)]

    def gate_tanh(cols):
        z_half = jnp.dot(hb_ref[...], wg_ref[:, cols], preferred_element_type=F32)
        return jnp.tanh(_row_scale(z_half, rscale) + 0.5 * bg_ref[:, cols])

    for cs in col_chunks:
        y_half = jnp.dot(attn_ref[...], wba_ref[:, cs], preferred_element_type=F32)
        t1 = gate_tanh(slice(D_MODEL + cs.start, D_MODEL + cs.stop))
        ga_ref[:, cs] = y_half + t1 * y_half
    g0 = [gate_tanh(cs) for cs in col_chunks[:2]]
    for g in (3, 0, 1, 2):
        pool_group(g)

    for c, cs in enumerate(col_chunks):
        if c == len(g0):
            g0.append(gate_tanh(cs))
        y_half = jnp.dot(pm_ref[...], wbp_ref[:, cs], preferred_element_type=F32)
        mg_ref[:, cs] = (y_half + g0[c] * y_half + ga_ref[:, cs]).astype(BF16)
    for cs in col_chunks:
        ga_ref[:, cs] = x1_ref[:, cs] + jnp.dot(mg_ref[...], wout_ref[:, cs],
                                                preferred_element_type=F32)

    hb_ref[...], rscale2 = _norm_split(ga_ref[...], n2_ref[...])
    _swiglu_act(hb_ref, rscale2, wgu2_ref, act_ref)
    hm = TM // 2
    for rows in (slice(0, hm), slice(hm, TM)):
        for cs in col_chunks:
            out_ref[rows, cs] = ga_ref[rows, cs] + jnp.dot(act_ref[rows, :], wd_ref[:, cs],
                                                           preferred_element_type=F32)
        out_ref[rows, :] = _rmsnorm(out_ref[rows, :], nf_ref[...])


def _resident(shape):
    return pl.BlockSpec(shape, lambda i: (0,) * len(shape), pipeline_mode=pl.Buffered(1))


def _toeplitz(w):
    n_heads, width = w.shape
    flat = jnp.tile(jnp.pad(w, ((0, 0), (0, 1))), (1, CHUNK))
    rows = flat[:, :CHUNK * width].reshape(n_heads, CHUNK, width)
    return rows[:, :, CHUNK - 1:]


def _bias_pieces(rel_bias):
    assert MAX_REL == CHUNK
    rb = rel_bias.astype(F32) * LOG2E
    n_heads = rb.shape[0]
    far = rb[:, 2 * MAX_REL]
    t_own = _toeplitz(rb[:, 1:2 * MAX_REL][:, ::-1])
    t_prev = _toeplitz(jnp.concatenate(
        [jnp.broadcast_to(far[:, None], (n_heads, CHUNK)), rb[:, MAX_REL + 1:2 * MAX_REL][:, ::-1]],
        axis=1))
    full = (n_heads, CHUNK, LANES)
    return jnp.stack([jnp.full(full, MASK_VALUE, F32),
                      jnp.broadcast_to(far[:, None, None], full),
                      jnp.concatenate([t_prev, t_prev], axis=2),
                      jnp.concatenate([t_own, t_own], axis=2)], axis=1)


def kernel(x, ffn1_norm, ffn1_w_gate, ffn1_w_up, ffn1_w_down, mix_norm, w_in, pool_w, pool_scale,
           rel_bias, w_branch_pool, w_branch_attn, w_gate, b_gate, w_out, ffn2_norm, ffn2_w_gate,
           ffn2_w_up, ffn2_w_down, final_norm):
    b, s, d = x.shape
    assert d == D_MODEL and s % TM == 0 and ffn1_norm.shape[0] == 1
    n_tok = b * s
    n_tiles = n_tok // TM
    assert n_tiles == N_TILES
    tiles_per_seq = s // TM
    xf = x.reshape(n_tok, d)

    row = lambda v: v.reshape(1, -1).astype(F32)
    cparams = pltpu.CompilerParams(dimension_semantics=("arbitrary",),
                                   vmem_limit_bytes=VMEM_LIMIT)
    tile = lambda width: pl.BlockSpec((TM, width), lambda i: (i, 0))

    tok = lambda i: jnp.maximum(i - N_CAST, 0)
    tile1 = lambda width: pl.BlockSpec((TM, width), lambda i: (tok(i), 0))
    staged = lambda w: pl.BlockSpec((w.shape[0] // N_CAST, w.shape[1]),
                                    lambda i: (jnp.minimum(i, N_CAST - 1), 0))
    own_w = [ffn1_w_gate[0], ffn1_w_up[0], ffn1_w_down[0], w_in[0]]
    side_w = [w_gate[0], w_branch_pool[0], w_branch_attn[0], w_out[0], ffn2_w_down[0],
              pool_w[0].reshape(POOL_WIDTH, POOL_GROUP)]
    gate_up_rows = D_MODEL // n_tiles
    gate_up_spec = lambda width: pl.BlockSpec((gate_up_rows, width), lambda i: (tok(i), 0))

    def side_specs():
        specs = []
        for w, (rows, steps, _) in zip(side_w, SIDE_ROWS):
            assert w.shape[0] == rows
            specs.append(pl.BlockSpec(
                (rows // steps, w.shape[1]),
                functools.partial(lambda steps, i: (jnp.minimum(tok(i), steps - 1), 0), steps)))
        return specs

    res = pl.pallas_call(
        _ffn1_inproj_kernel,
        grid=(N_CAST + n_tiles,),
        in_specs=[tile1(D_MODEL), _resident((1, D_MODEL)), staged(own_w[0]), staged(own_w[1]),
                  staged(own_w[2]), _resident((1, D_MODEL)), staged(own_w[3]),
                  gate_up_spec(D_FF), gate_up_spec(D_FF)] + side_specs(),
        out_specs=[tile1(D_MODEL), tile1(POOL_WIDTH), tile1(3 * ATTN_WIDTH),
                   gate_up_spec(2 * D_FF)] + side_specs(),
        out_shape=[jax.ShapeDtypeStruct((n_tok, D_MODEL), F32),
                   jax.ShapeDtypeStruct((n_tok, POOL_WIDTH), F32),
                   jax.ShapeDtypeStruct((n_tok, 3 * ATTN_WIDTH), BF16),
                   jax.ShapeDtypeStruct((D_MODEL, 2 * D_FF), BF16)]
                  + [jax.ShapeDtypeStruct(w.shape, BF16) for w in side_w],
        scratch_shapes=[pltpu.VMEM((D_MODEL, 2 * D_FF), BF16), pltpu.VMEM((D_FF, D_MODEL), BF16),
                        pltpu.VMEM((D_MODEL, IN_WIDTH), BF16),
                        pltpu.VMEM((TM, D_MODEL), BF16), pltpu.VMEM((TM, D_FF), BF16)],
        compiler_params=cparams,
        name="ffn1_inproj",
    )(xf, row(ffn1_norm), own_w[0], own_w[1], own_w[2], row(mix_norm), own_w[3],
      ffn2_w_gate[0], ffn2_w_up[0], *side_w)
    x1, up, qkv, wgu2_b = res[:4]
    wgate_b, wbp_b, wba_b, wout_b, wd2_b, poolw_b = res[4:]

    assert s % ATTN_TM == 0 and ATTN_TM % ATTN_PREV == 0
    prev = lambda i: jnp.maximum(i * (ATTN_TM // ATTN_PREV) - 1, 0)
    cur_spec = lambda col: pl.BlockSpec((ATTN_TM, ATTN_WIDTH), lambda i: (i, col))
    prev_spec = lambda col: pl.BlockSpec((ATTN_PREV, ATTN_WIDTH), lambda i: (prev(i), col))
    attn = pl.pallas_call(
        functools.partial(_attn_kernel, s // ATTN_TM),
        grid=(n_tok // ATTN_TM,),
        in_specs=[cur_spec(0), prev_spec(1), cur_spec(1), prev_spec(2), cur_spec(2),
                  _resident((N_HEADS, 4, CHUNK, LANES))],
        out_specs=cur_spec(0),
        out_shape=jax.ShapeDtypeStruct((n_tok, ATTN_WIDTH), BF16),
        scratch_shapes=[pltpu.VMEM((1 + ATTN_PREV // Q_BLOCK, N_HEADS, Q_BLOCK, KEY_BLOCK), F32),
                        pltpu.VMEM((ATTN_PREV + ATTN_TM, ATTN_WIDTH), BF16),
                        pltpu.VMEM((ATTN_PREV + ATTN_TM, ATTN_WIDTH), BF16)],
        compiler_params=cparams,
        name="chunk_attn",
    )(qkv, qkv, qkv, qkv, qkv, _bias_pieces(rel_bias[0]))

    halo_blocks = TM // HALO
    out = pl.pallas_call(
        functools.partial(_mix_ffn2_kernel, tiles_per_seq),
        grid=(n_tiles,),
        in_specs=[tile(D_MODEL), tile(POOL_WIDTH),
                  pl.BlockSpec((HALO, POOL_WIDTH),
                               lambda i: (jnp.maximum(i * halo_blocks - 1, 0), 0)),
                  tile(ATTN_WIDTH), _resident((1, D_MODEL)),
                  _resident((D_MODEL, 2 * D_MODEL)), _resident((1, 2 * D_MODEL)),
                  _resident((len(POOL_WINDOWS), POOL_GROUP, POOL_GROUP)),
                  _resident((1, POOL_WIDTH)), _resident((POOL_WIDTH, D_MODEL)),
                  _resident((ATTN_WIDTH, D_MODEL)), _resident((D_MODEL, D_MODEL)),
                  _resident((1, D_MODEL)), _resident((D_MODEL, 2 * D_FF)),
                  _resident((D_FF, D_MODEL)), _resident((1, D_MODEL))],
        out_specs=tile(D_MODEL),
        out_shape=jax.ShapeDtypeStruct((n_tok, D_MODEL), F32),
        scratch_shapes=[pltpu.VMEM((TM, D_MODEL), BF16), pltpu.VMEM((TM, D_FF), BF16),
                        pltpu.VMEM((POOL_PAD + HALO + TM, POOL_WIDTH), F32),
                        pltpu.VMEM((POOL_PAD + HALO + TM, POOL_WIDTH), F32),
                        pltpu.VMEM((POOL_PAD + HALO + TM, POOL_WIDTH), F32),
                        pltpu.VMEM((TM, POOL_WIDTH), BF16), pltpu.VMEM((TM, D_MODEL), BF16),
                        pltpu.VMEM((TM, D_MODEL), F32)],
        compiler_params=cparams,
        name="mix_ffn2",
    )(x1, up, up, attn, row(mix_norm), wgate_b, row(b_gate),
      poolw_b.reshape(len(POOL_WINDOWS), POOL_GROUP, POOL_GROUP), row(pool_scale), wbp_b, wba_b,
      wout_b, row(ffn2_norm), wgu2_b, wd2_b, row(final_norm))
    return out.reshape(b, s, d)
```

```python
import functools

import jax
import jax.numpy as jnp
from jax import lax
from jax.experimental import pallas as pl
from jax.experimental.pallas import tpu as pltpu

D_MODEL = 1024
D_FF = 2816
CHUNK = 64
LEFT_CHUNKS = 8
POOL_WIDTH = 512
POOL_WINDOWS = (2, 4, 8, 16)
POOL_GROUP = 128
N_HEADS = 8
HEAD_DIM = 64
ATTN_WIDTH = 512
MAX_REL = 64
IN_WIDTH = POOL_WIDTH + 3 * ATTN_WIDTH
EPS = 1e-6

LANES = 128
TM = 512
FF_CHUNK = 256
COL_CHUNK = 256
Q_BLOCK = 256
ATTN_PREV = LEFT_CHUNKS * CHUNK
ATTN_TM = 1024
KEY_BLOCK = Q_BLOCK + ATTN_PREV
SCORE_LOOKAHEAD = 1
HALO = 16
POOL_PAD = 8
N_CAST = 8
N_TILES = 32
SIDE_ROWS = ((D_MODEL, N_TILES, 0.5),
             (POOL_WIDTH, N_TILES, 0.5),
             (ATTN_WIDTH, N_TILES, 0.5),
             (D_MODEL, N_TILES, 1.0),
             (D_FF, N_TILES // 2, 0.5),
             (POOL_WIDTH, N_TILES, 1.0))
MASK_VALUE = -2.0 ** 126
LOG2E = 1.4426950408889634
Q_SCALE = HEAD_DIM ** -0.5 * LOG2E
VMEM_LIMIT = 56 * 1024 * 1024

F32 = jnp.float32
BF16 = jnp.bfloat16


def _rmsnorm(x, g):
    ms = jnp.mean(x * x, axis=-1, keepdims=True)
    return x * lax.rsqrt(ms + EPS) * g


def _norm_split(x, g):
    rscale = lax.rsqrt(jnp.mean(x * x, axis=-1, keepdims=True) + EPS)
    return (x * g).astype(BF16), jnp.broadcast_to(rscale, (x.shape[0], LANES))


def _row_scale(y, rscale):
    return jnp.concatenate([y[:, lo:lo + LANES] * rscale for lo in range(0, y.shape[1], LANES)],
                           axis=1)


def _interleave_gate_up(dst_ref, rows, gate_blk, up_blk):
    for c in range(D_FF // FF_CHUNK):
        src = slice(c * FF_CHUNK, (c + 1) * FF_CHUNK)
        dst_ref[rows, 2 * c * FF_CHUNK:(2 * c + 1) * FF_CHUNK] = gate_blk[:, src].astype(BF16)
        dst_ref[rows, (2 * c + 1) * FF_CHUNK:(2 * c + 2) * FF_CHUNK] = up_blk[:, src].astype(BF16)


def _swiglu_act(hb_ref, rscale, wgu_ref, act_ref):
    half_rscale = 0.5 * rscale
    for c in range(D_FF // FF_CHUNK):
        gu = jnp.dot(hb_ref[...], wgu_ref[:, 2 * c * FF_CHUNK:(2 * c + 2) * FF_CHUNK],
                     preferred_element_type=F32)
        hg = _row_scale(gu[:, :FF_CHUNK], half_rscale)
        up = _row_scale(gu[:, FF_CHUNK:], rscale)
        act_ref[:, c * FF_CHUNK:(c + 1) * FF_CHUNK] = ((hg + hg * jnp.tanh(hg)) * up).astype(BF16)


def _ffn1_inproj_kernel(x_ref, n1_ref, wg_blk, wu_blk, wd_blk, nm_ref, win_blk,
                        wg2_blk, wu2_blk, *rest):
    n_side = len(SIDE_ROWS)
    side_in = rest[:n_side]
    x1_ref, up_ref, qkv_ref, wgu2_out = rest[n_side:n_side + 4]
    side_out = rest[n_side + 4:2 * n_side + 4]
    wgu_ref, wd_ref, win_ref, hb_ref, act_ref = rest[2 * n_side + 4:]
    i = pl.program_id(0)

    @pl.when(i < N_CAST)
    def _stage_weights():
        def rows_of(blk):
            return pl.ds(pl.multiple_of(i * blk.shape[0], blk.shape[0]), blk.shape[0])

        _interleave_gate_up(wgu_ref, rows_of(wg_blk), wg_blk, wu_blk)
        wd_ref[rows_of(wd_blk), :] = (wd_blk[...] * 0.5).astype(BF16)
        win_ref[rows_of(win_blk), :] = win_blk[...].astype(BF16)

    @pl.when(i >= N_CAST)
    def _token_tile():
        hb_ref[...], rscale = _norm_split(x_ref[...], n1_ref[...])
        _swiglu_act(hb_ref, rscale, wgu_ref, act_ref)
        for lo in range(0, D_MODEL, COL_CHUNK):
            cs = slice(lo, lo + COL_CHUNK)
            x1_ref[:, cs] = x_ref[:, cs] + jnp.dot(act_ref[...], wd_ref[:, cs],
                                                   preferred_element_type=F32)
        hb_ref[...], rscale = _norm_split(x1_ref[...], nm_ref[...])
        q_rscale = rscale * Q_SCALE
        for lo in range(0, IN_WIDTH, COL_CHUNK):
            y = jnp.dot(hb_ref[...], win_ref[:, lo:lo + COL_CHUNK], preferred_element_type=F32)
            if lo < POOL_WIDTH:
                up_ref[:, lo:lo + COL_CHUNK] = _row_scale(y, rscale)
            else:
                is_q = lo < POOL_WIDTH + ATTN_WIDTH
                qkv_ref[:, lo - POOL_WIDTH:lo - POOL_WIDTH + COL_CHUNK] = _row_scale(
                    y, q_rscale if is_q else rscale).astype(BF16)
        _interleave_gate_up(wgu2_out, slice(None), wg2_blk, wu2_blk)
        for src, dst, (_, _, scale) in zip(side_in, side_out, SIDE_ROWS):
            w = src[...] if scale == 1.0 else src[...] * scale
            dst[...] = w.astype(BF16)


def _band_kind(dchunk):
    if dchunk < 0 or dchunk > LEFT_CHUNKS:
        return 0
    if dchunk < LEFT_CHUNKS - 1:
        return 1
    return 2 if dchunk == LEFT_CHUNKS - 1 else 3


def _attn_kernel(tiles_per_seq, q_ref, kp_ref, kc_ref, vp_ref, vc_ref, pieces_ref,
                 o_ref, bias_ref, v_even_ref, v_odd_ref):
    i = pl.program_id(0)
    is_first = (i % tiles_per_seq) == 0
    n_first_blocks = ATTN_PREV // Q_BLOCK

    @pl.when(i == 0)
    def _build_bias_tables():
        low = lax.broadcasted_iota(jnp.int32, (CHUNK, LANES), 1) < CHUNK

        def per_head(h, carry):
            for variant in range(1 + n_first_blocks):
                n_before = 0 if variant == 0 else (ATTN_PREV - (variant - 1) * Q_BLOCK) // LANES
                for qc in range(Q_BLOCK // CHUNK):
                    for kp in range(KEY_BLOCK // LANES):
                        ka, kb = _band_kind(2 * kp - qc), _band_kind(2 * kp + 1 - qc)
                        if kp < n_before:
                            ka = kb = 0
                        blk = pieces_ref[h, ka]
                        if kb != ka:
                            blk = jnp.where(low, blk, pieces_ref[h, kb])
                        bias_ref[variant, h, qc * CHUNK:(qc + 1) * CHUNK,
                                 kp * LANES:(kp + 1) * LANES] = blk
            return carry

        lax.fori_loop(0, N_HEADS, per_head, 0)

    for src, row0 in ((vp_ref, 0), (vc_ref, ATTN_PREV)):
        v = src[...]
        even_lanes = (lax.broadcasted_iota(jnp.int32, v.shape, 1) % LANES) < HEAD_DIM
        ones = jnp.ones(v.shape, BF16)
        v_even_ref[row0:row0 + v.shape[0], :] = jnp.where(even_lanes, v, ones)
        v_odd_ref[row0:row0 + v.shape[0], :] = jnp.where(even_lanes, ones, v)

    low_half = lax.broadcasted_iota(jnp.int32, (Q_BLOCK, LANES), 1) < HEAD_DIM
    units = [(sb, p, par) for sb in range(ATTN_TM // Q_BLOCK) for p in range(N_HEADS // 2)
             for par in range(2)]

    def scores(unit):
        sb, p, par = unit
        r0, cs = sb * Q_BLOCK, slice(p * LANES, (p + 1) * LANES)
        qp = q_ref[r0:r0 + Q_BLOCK, cs]
        qm = jnp.where(low_half if par == 0 else jnp.logical_not(low_half), qp, jnp.zeros_like(qp))
        nt = (((1,), (1,)), ((), ()))
        parts = []
        if r0 < ATTN_PREV:
            parts.append(lax.dot_general(qm, kp_ref[r0:ATTN_PREV, cs], nt,
                                         preferred_element_type=F32))
        parts.append(lax.dot_general(
            qm, kc_ref[max(r0 - ATTN_PREV, 0):r0 + KEY_BLOCK - ATTN_PREV, cs], nt,
            preferred_element_type=F32))
        s = parts[0] if len(parts) == 1 else jnp.concatenate(parts, axis=1)
        variant = jnp.where(is_first, 1 + sb, 0) if sb < n_first_blocks else 0
        return s + bias_ref[variant, 2 * p + par]

    pending = [scores(u) for u in units[:SCORE_LOOKAHEAD]]
    outs = []
    for n, (sb, p, par) in enumerate(units):
        r0, cs = sb * Q_BLOCK, slice(p * LANES, (p + 1) * LANES)
        s = pending.pop(0)
        e = jnp.exp2(s - jnp.max(s, axis=-1, keepdims=True))
        if n + SCORE_LOOKAHEAD < len(units):
            pending.append(scores(units[n + SCORE_LOOKAHEAD]))
        v_ref = v_even_ref if par == 0 else v_odd_ref
        o = jnp.dot(e.astype(BF16), v_ref[r0:r0 + KEY_BLOCK, cs], preferred_element_type=F32)
        outs.append(o / pltpu.roll(o, HEAD_DIM, axis=1))
        if par == 1:
            o_ref[r0:r0 + Q_BLOCK, cs] = jnp.where(low_half, outs[-2], outs[-1]).astype(BF16)


def _mix_ffn2_kernel(tiles_per_seq, x1_ref, up_ref, halo_ref, attn_ref, nm_ref, wg_ref,
                     bg_ref, poolw_ref, pscale_ref, wbp_ref, wba_ref, wout_ref,
                     n2_ref, wgu2_ref, wd_ref, nf_ref,
                     out_ref, hb_ref, act_ref, ext_ref, lvl_a_ref, lvl_b_ref, pm_ref, mg_ref, ga_ref):
    i = pl.program_id(0)
    tile_in_seq = i % tiles_per_seq
    hb_ref[...], rscale = _norm_split(x1_ref[...], nm_ref[...])

    base = POOL_PAD + HALO
    keep_halo = jnp.where(tile_in_seq == 0, 0.0, 1.0).astype(F32)
    lvl_refs = (lvl_a_ref, lvl_b_ref)
    for ref in (ext_ref,) + lvl_refs:
        ref[0:POOL_PAD, :] = jnp.zeros((POOL_PAD, POOL_WIDTH), F32)
    ext_ref[POOL_PAD:base, :] = halo_ref[...] * keep_halo
    ext_ref[base:base + TM, :] = up_ref[...]
    pos_top = tile_in_seq * TM + lax.broadcasted_iota(jnp.int32, (HALO, 1), 0)

    def pool_group(g):
        w = POOL_WINDOWS[g]
        cs = slice(g * POOL_GROUP, (g + 1) * POOL_GROUP)
        src, shift, level = ext_ref, 1, 0
        while 2 * shift < w:
            lvl_ref = lvl_refs[level % 2]
            lvl_ref[POOL_PAD:base + TM, cs] = (src[POOL_PAD:base + TM, cs]
                                              + src[POOL_PAD - shift:base + TM - shift, cs])
            src, shift, level = lvl_ref, 2 * shift, level + 1
        sums = src[base:base + TM, cs] + src[base - shift:base + TM - shift, cs]
        inv_top = 1.0 / jnp.minimum(pos_top + 1, w).astype(F32)
        mean = jnp.concatenate([sums[:HALO] * inv_top, sums[HALO:] * (1.0 / w)], axis=0)
        mixed = (mean - ext_ref[base:base + TM, cs]).astype(BF16)
        y = jnp.dot(mixed, poolw_ref[g], preferred_element_type=F32)
        pm_ref[:, cs] = (y * pscale_ref[:, cs]).astype(BF16)

    col_chunks = [slice(lo, lo + COL_CHUNK) for lo in range(0, D_MODEL, COL_CHUNK)]

    def gate_tanh(cols):
        z_half = jnp.dot(hb_ref[...], wg_ref[:, cols], preferred_element_type=F32)
        return jnp.tanh(_row_scale(z_half, rscale) + 0.5 * bg_ref[:, cols])

    for cs in col_chunks:
        y_half = jnp.dot(attn_ref[...], wba_ref[:, cs], preferred_element_type=F32)
        t1 = gate_tanh(slice(D_MODEL + cs.start, D_MODEL + cs.stop))
        ga_ref[:, cs] = y_half + t1 * y_half
    g0 = [gate_tanh(cs) for cs in col_chunks[:2]]
    for g in (3, 0, 1, 2):
        pool_group(g)

    for c, cs in enumerate(col_chunks):
        if c == len(g0):
            g0.append(gate_tanh(cs))
        y_half = jnp.dot(pm_ref[...], wbp_ref[:, cs], preferred_element_type=F32)
        mg_ref[:, cs] = (y_half + g0[c] * y_half + ga_ref[:, cs]).astype(BF16)
    for cs in col_chunks:
        ga_ref[:, cs] = x1_ref[:, cs] + jnp.dot(mg_ref[...], wout_ref[:, cs],
                                                preferred_element_type=F32)

    hb_ref[...], rscale2 = _norm_split(ga_ref[...], n2_ref[...])
    _swiglu_act(hb_ref, rscale2, wgu2_ref, act_ref)
    hm = TM // 2
    for rows in (slice(0, hm), slice(hm, TM)):
        for cs in col_chunks:
            out_ref[rows, cs] = ga_ref[rows, cs] + jnp.dot(act_ref[rows, :], wd_ref[:, cs],
                                                           preferred_element_type=F32)
        out_ref[rows, :] = _rmsnorm(out_ref[rows, :], nf_ref[...])


def _resident(shape):
    return pl.BlockSpec(shape, lambda i: (0,) * len(shape), pipeline_mode=pl.Buffered(1))


def _toeplitz(w):
    n_heads, width = w.shape
    flat = jnp.tile(jnp.pad(w, ((0, 0), (0, 1))), (1, CHUNK))
    rows = flat[:, :CHUNK * width].reshape(n_heads, CHUNK, width)
    return rows[:, :, CHUNK - 1:]


def _bias_pieces(rel_bias):
    assert MAX_REL == CHUNK
    rb = rel_bias.astype(F32) * LOG2E
    n_heads = rb.shape[0]
    far = rb[:, 2 * MAX_REL]
    t_own = _toeplitz(rb[:, 1:2 * MAX_REL][:, ::-1])
    t_prev = _toeplitz(jnp.concatenate(
        [jnp.broadcast_to(far[:, None], (n_heads, CHUNK)), rb[:, MAX_REL + 1:2 * MAX_REL][:, ::-1]],
        axis=1))
    full = (n_heads, CHUNK, LANES)
    return jnp.stack([jnp.full(full, MASK_VALUE, F32),
                      jnp.broadcast_to(far[:, None, None], full),
                      jnp.concatenate([t_prev, t_prev], axis=2),
                      jnp.concatenate([t_own, t_own], axis=2)], axis=1)


def kernel(x, ffn1_norm, ffn1_w_gate, ffn1_w_up, ffn1_w_down, mix_norm, w_in, pool_w, pool_scale,
           rel_bias, w_branch_pool, w_branch_attn, w_gate, b_gate, w_out, ffn2_norm, ffn2_w_gate,
           ffn2_w_up, ffn2_w_down, final_norm):
    b, s, d = x.shape
    assert d == D_MODEL and s % TM == 0 and ffn1_norm.shape[0] == 1
    n_tok = b * s
    n_tiles = n_tok // TM
    assert n_tiles == N_TILES
    tiles_per_seq = s // TM
    xf = x.reshape(n_tok, d)

    row = lambda v: v.reshape(1, -1).astype(F32)
    cparams = pltpu.CompilerParams(dimension_semantics=("arbitrary",),
                                   vmem_limit_bytes=VMEM_LIMIT)
    tile = lambda width: pl.BlockSpec((TM, width), lambda i: (i, 0))

    tok = lambda i: jnp.maximum(i - N_CAST, 0)
    tile1 = lambda width: pl.BlockSpec((TM, width), lambda i: (tok(i), 0))
    staged = lambda w: pl.BlockSpec((w.shape[0] // N_CAST, w.shape[1]),
                                    lambda i: (jnp.minimum(i, N_CAST - 1), 0))
    own_w = [ffn1_w_gate[0], ffn1_w_up[0], ffn1_w_down[0], w_in[0]]
    side_w = [w_gate[0], w_branch_pool[0], w_branch_attn[0], w_out[0], ffn2_w_down[0],
              pool_w[0].reshape(POOL_WIDTH, POOL_GROUP)]
    gate_up_rows = D_MODEL // n_tiles
    gate_up_spec = lambda width: pl.BlockSpec((gate_up_rows, width), lambda i: (tok(i), 0))

    def side_specs():
        specs = []
        for w, (rows, steps, _) in zip(side_w, SIDE_ROWS):
            assert w.shape[0] == rows
            specs.append(pl.BlockSpec(
                (rows // steps, w.shape[1]),
                functools.partial(lambda steps, i: (jnp.minimum(tok(i), steps - 1), 0), steps)))
        return specs

    res = pl.pallas_call(
        _ffn1_inproj_kernel,
        grid=(N_CAST + n_tiles,),
        in_specs=[tile1(D_MODEL), _resident((1, D_MODEL)), staged(own_w[0]), staged(own_w[1]),
                  staged(own_w[2]), _resident((1, D_MODEL)), staged(own_w[3]),
                  gate_up_spec(D_FF), gate_up_spec(D_FF)] + side_specs(),
        out_specs=[tile1(D_MODEL), tile1(POOL_WIDTH), tile1(3 * ATTN_WIDTH),
                   gate_up_spec(2 * D_FF)] + side_specs(),
        out_shape=[jax.ShapeDtypeStruct((n_tok, D_MODEL), F32),
                   jax.ShapeDtypeStruct((n_tok, POOL_WIDTH), F32),
                   jax.ShapeDtypeStruct((n_tok, 3 * ATTN_WIDTH), BF16),
                   jax.ShapeDtypeStruct((D_MODEL, 2 * D_FF), BF16)]
                  + [jax.ShapeDtypeStruct(w.shape, BF16) for w in side_w],
        scratch_shapes=[pltpu.VMEM((D_MODEL, 2 * D_FF), BF16), pltpu.VMEM((D_FF, D_MODEL), BF16),
                        pltpu.VMEM((D_MODEL, IN_WIDTH), BF16),
                        pltpu.VMEM((TM, D_MODEL), BF16), pltpu.VMEM((TM, D_FF), BF16)],
        compiler_params=cparams,
        name="ffn1_inproj",
    )(xf, row(ffn1_norm), own_w[0], own_w[1], own_w[2], row(mix_norm), own_w[3],
      ffn2_w_gate[0], ffn2_w_up[0], *side_w)
    x1, up, qkv, wgu2_b = res[:4]
    wgate_b, wbp_b, wba_b, wout_b, wd2_b, poolw_b = res[4:]

    assert s % ATTN_TM == 0 and ATTN_TM % ATTN_PREV == 0
    prev = lambda i: jnp.maximum(i * (ATTN_TM // ATTN_PREV) - 1, 0)
    cur_spec = lambda col: pl.BlockSpec((ATTN_TM, ATTN_WIDTH), lambda i: (i, col))
    prev_spec = lambda col: pl.BlockSpec((ATTN_PREV, ATTN_WIDTH), lambda i: (prev(i), col))
    attn = pl.pallas_call(
        functools.partial(_attn_kernel, s // ATTN_TM),
        grid=(n_tok // ATTN_TM,),
        in_specs=[cur_spec(0), prev_spec(1), cur_spec(1), prev_spec(2), cur_spec(2),
                  _resident((N_HEADS, 4, CHUNK, LANES))],
        out_specs=cur_spec(0),
        out_shape=jax.ShapeDtypeStruct((n_tok, ATTN_WIDTH), BF16),
        scratch_shapes=[pltpu.VMEM((1 + ATTN_PREV // Q_BLOCK, N_HEADS, Q_BLOCK, KEY_BLOCK), F32),
                        pltpu.VMEM((ATTN_PREV + ATTN_TM, ATTN_WIDTH), BF16),
                        pltpu.VMEM((ATTN_PREV + ATTN_TM, ATTN_WIDTH), BF16)],
        compiler_params=cparams,
        name="chunk_attn",
    )(qkv, qkv, qkv, qkv, qkv, _bias_pieces(rel_bias[0]))

    halo_blocks = TM // HALO
    out = pl.pallas_call(
        functools.partial(_mix_ffn2_kernel, tiles_per_seq),
        grid=(n_tiles,),
        in_specs=[tile(D_MODEL), tile(POOL_WIDTH),
                  pl.BlockSpec((HALO, POOL_WIDTH),
                               lambda i: (jnp.maximum(i * halo_blocks - 1, 0), 0)),
                  tile(ATTN_WIDTH), _resident((1, D_MODEL)),
                  _resident((D_MODEL, 2 * D_MODEL)), _resident((1, 2 * D_MODEL)),
                  _resident((len(POOL_WINDOWS), POOL_GROUP, POOL_GROUP)),
                  _resident((1, POOL_WIDTH)), _resident((POOL_WIDTH, D_MODEL)),
                  _resident((ATTN_WIDTH, D_MODEL)), _resident((D_MODEL, D_MODEL)),
                  _resident((1, D_MODEL)), _resident((D_MODEL, 2 * D_FF)),
                  _resident((D_FF, D_MODEL)), _resident((1, D_MODEL))],
        out_specs=tile(D_MODEL),
        out_shape=jax.ShapeDtypeStruct((n_tok, D_MODEL), F32),
        scratch_shapes=[pltpu.VMEM((TM, D_MODEL), BF16), pltpu.VMEM((TM, D_FF), BF16),
                        pltpu.VMEM((POOL_PAD + HALO + TM, POOL_WIDTH), F32),
                        pltpu.VMEM((POOL_PAD + HALO + TM, POOL_WIDTH), F32),
                        pltpu.VMEM((POOL_PAD + HALO + TM, POOL_WIDTH), F32),
                        pltpu.VMEM((TM, POOL_WIDTH), BF16), pltpu.VMEM((TM, D_MODEL), BF16),
                        pltpu.VMEM((TM, D_MODEL), F32)],
        compiler_params=cparams,
        name="mix_ffn2",
    )(x1, up, up, attn, row(mix_norm), wgate_b, row(b_gate),
      poolw_b.reshape(len(POOL_WINDOWS), POOL_GROUP, POOL_GROUP), row(pool_scale), wbp_b, wba_b,
      wout_b, row(ffn2_norm), wgu2_b, wd2_b, row(final_norm))
    return out.reshape(b, s, d)
```

```python
import functools

import jax
import jax.numpy as jnp
from jax import lax
from jax.experimental import pallas as pl
from jax.experimental.pallas import tpu as pltpu

D_MODEL = 1024
D_FF = 2816
CHUNK = 64
LEFT_CHUNKS = 8
POOL_WIDTH = 512
POOL_WINDOWS = (2, 4, 8, 16)
POOL_GROUP = 128
N_HEADS = 8
HEAD_DIM = 64
ATTN_WIDTH = 512
MAX_REL = 64
IN_WIDTH = POOL_WIDTH + 3 * ATTN_WIDTH
EPS = 1e-6

LANES = 128
TM = 512
FF_CHUNK = 256
COL_CHUNK = 256
Q_BLOCK = 256
ATTN_PREV = LEFT_CHUNKS * CHUNK
ATTN_TM = 1024
KEY_BLOCK = Q_BLOCK + ATTN_PREV
SCORE_LOOKAHEAD = 2
HALO = 16
POOL_PAD = 8
N_CAST = 8
N_TILES = 32
SIDE_ROWS = ((D_MODEL, N_TILES, 0.5),
             (POOL_WIDTH, N_TILES, 0.5),
             (ATTN_WIDTH, N_TILES, 0.5),
             (D_MODEL, N_TILES, 1.0),
             (D_FF, N_TILES // 2, 0.5),
             (POOL_WIDTH, N_TILES, 1.0))
MASK_VALUE = -2.0 ** 126
LOG2E = 1.4426950408889634
Q_SCALE = HEAD_DIM ** -0.5 * LOG2E
VMEM_LIMIT = 56 * 1024 * 1024

F32 = jnp.float32
BF16 = jnp.bfloat16


def _rmsnorm(x, g):
    ms = jnp.mean(x * x, axis=-1, keepdims=True)
    return x * lax.rsqrt(ms + EPS) * g


def _norm_split(x, g):
    rscale = lax.rsqrt(jnp.mean(x * x, axis=-1, keepdims=True) + EPS)
    return (x * g).astype(BF16), jnp.broadcast_to(rscale, (x.shape[0], LANES))


def _row_scale(y, rscale):
    return jnp.concatenate([y[:, lo:lo + LANES] * rscale for lo in range(0, y.shape[1], LANES)],
                           axis=1)


def _interleave_gate_up(dst_ref, rows, gate_blk, up_blk):
    for c in range(D_FF // FF_CHUNK):
        src = slice(c * FF_CHUNK, (c + 1) * FF_CHUNK)
        dst_ref[rows, 2 * c * FF_CHUNK:(2 * c + 1) * FF_CHUNK] = gate_blk[:, src].astype(BF16)
        dst_ref[rows, (2 * c + 1) * FF_CHUNK:(2 * c + 2) * FF_CHUNK] = up_blk[:, src].astype(BF16)


def _swiglu_act(hb_ref, rscale, wgu_ref, act_ref):
    half_rscale = 0.5 * rscale
    for c in range(D_FF // FF_CHUNK):
        gu = jnp.dot(hb_ref[...], wgu_ref[:, 2 * c * FF_CHUNK:(2 * c + 2) * FF_CHUNK],
                     preferred_element_type=F32)
        hg = _row_scale(gu[:, :FF_CHUNK], half_rscale)
        up = _row_scale(gu[:, FF_CHUNK:], rscale)
        act_ref[:, c * FF_CHUNK:(c + 1) * FF_CHUNK] = ((hg + hg * jnp.tanh(hg)) * up).astype(BF16)


def _ffn1_inproj_kernel(x_ref, n1_ref, wg_blk, wu_blk, wd_blk, nm_ref, win_blk,
                        wg2_blk, wu2_blk, *rest):
    n_side = len(SIDE_ROWS)
    side_in = rest[:n_side]
    x1_ref, up_ref, qkv_ref, wgu2_out = rest[n_side:n_side + 4]
    side_out = rest[n_side + 4:2 * n_side + 4]
    wgu_ref, wd_ref, win_ref, hb_ref, act_ref = rest[2 * n_side + 4:]
    i = pl.program_id(0)

    @pl.when(i < N_CAST)
    def _stage_weights():
        def rows_of(blk):
            return pl.ds(pl.multiple_of(i * blk.shape[0], blk.shape[0]), blk.shape[0])

        _interleave_gate_up(wgu_ref, rows_of(wg_blk), wg_blk, wu_blk)
        wd_ref[rows_of(wd_blk), :] = (wd_blk[...] * 0.5).astype(BF16)
        win_ref[rows_of(win_blk), :] = win_blk[...].astype(BF16)

    @pl.when(i >= N_CAST)
    def _token_tile():
        hb_ref[...], rscale = _norm_split(x_ref[...], n1_ref[...])
        _swiglu_act(hb_ref, rscale, wgu_ref, act_ref)
        for lo in range(0, D_MODEL, COL_CHUNK):
            cs = slice(lo, lo + COL_CHUNK)
            x1_ref[:, cs] = x_ref[:, cs] + jnp.dot(act_ref[...], wd_ref[:, cs],
                                                   preferred_element_type=F32)
        hb_ref[...], rscale = _norm_split(x1_ref[...], nm_ref[...])
        q_rscale = rscale * Q_SCALE
        for lo in range(0, IN_WIDTH, COL_CHUNK):
            y = jnp.dot(hb_ref[...], win_ref[:, lo:lo + COL_CHUNK], preferred_element_type=F32)
            if lo < POOL_WIDTH:
                up_ref[:, lo:lo + COL_CHUNK] = _row_scale(y, rscale)
            else:
                is_q = lo < POOL_WIDTH + ATTN_WIDTH
                qkv_ref[:, lo - POOL_WIDTH:lo - POOL_WIDTH + COL_CHUNK] = _row_scale(
                    y, q_rscale if is_q else rscale).astype(BF16)
        _interleave_gate_up(wgu2_out, slice(None), wg2_blk, wu2_blk)
        for src, dst, (_, _, scale) in zip(side_in, side_out, SIDE_ROWS):
            w = src[...] if scale == 1.0 else src[...] * scale
            dst[...] = w.astype(BF16)


def _band_kind(dchunk):
    if dchunk < 0 or dchunk > LEFT_CHUNKS:
        return 0
    if dchunk < LEFT_CHUNKS - 1:
        return 1
    return 2 if dchunk == LEFT_CHUNKS - 1 else 3


def _attn_kernel(tiles_per_seq, q_ref, kp_ref, kc_ref, vp_ref, vc_ref, pieces_ref,
                 o_ref, bias_ref, v_even_ref, v_odd_ref):
    i = pl.program_id(0)
    is_first = (i % tiles_per_seq) == 0
    n_first_blocks = ATTN_PREV // Q_BLOCK

    @pl.when(i == 0)
    def _build_bias_tables():
        low = lax.broadcasted_iota(jnp.int32, (CHUNK, LANES), 1) < CHUNK

        def per_head(h, carry):
            for variant in range(1 + n_first_blocks):
                n_before = 0 if variant == 0 else (ATTN_PREV - (variant - 1) * Q_BLOCK) // LANES
                for qc in range(Q_BLOCK // CHUNK):
                    for kp in range(KEY_BLOCK // LANES):
                        ka, kb = _band_kind(2 * kp - qc), _band_kind(2 * kp + 1 - qc)
                        if kp < n_before:
                            ka = kb = 0
                        blk = pieces_ref[h, ka]
                        if kb != ka:
                            blk = jnp.where(low, blk, pieces_ref[h, kb])
                        bias_ref[variant, h, qc * CHUNK:(qc + 1) * CHUNK,
                                 kp * LANES:(kp + 1) * LANES] = blk
            return carry

        lax.fori_loop(0, N_HEADS, per_head, 0)

    for src, row0 in ((vp_ref, 0), (vc_ref, ATTN_PREV)):
        v = src[...]
        even_lanes = (lax.broadcasted_iota(jnp.int32, v.shape, 1) % LANES) < HEAD_DIM
        ones = jnp.ones(v.shape, BF16)
        v_even_ref[row0:row0 + v.shape[0], :] = jnp.where(even_lanes, v, ones)
        v_odd_ref[row0:row0 + v.shape[0], :] = jnp.where(even_lanes, ones, v)

    low_half = lax.broadcasted_iota(jnp.int32, (Q_BLOCK, LANES), 1) < HEAD_DIM
    units = [(sb, p, par) for sb in range(ATTN_TM // Q_BLOCK) for p in range(N_HEADS // 2)
             for par in range(2)]

    def scores(unit):
        sb, p, par = unit
        r0, cs = sb * Q_BLOCK, slice(p * LANES, (p + 1) * LANES)
        qp = q_ref[r0:r0 + Q_BLOCK, cs]
        qm = jnp.where(low_half if par == 0 else jnp.logical_not(low_half), qp, jnp.zeros_like(qp))
        nt = (((1,), (1,)), ((), ()))
        parts = []
        if r0 < ATTN_PREV:
            parts.append(lax.dot_general(qm, kp_ref[r0:ATTN_PREV, cs], nt,
                                         preferred_element_type=F32))
        parts.append(lax.dot_general(
            qm, kc_ref[max(r0 - ATTN_PREV, 0):r0 + KEY_BLOCK - ATTN_PREV, cs], nt,
            preferred_element_type=F32))
        s = parts[0] if len(parts) == 1 else jnp.concatenate(parts, axis=1)
        variant = jnp.where(is_first, 1 + sb, 0) if sb < n_first_blocks else 0
        return s + bias_ref[variant, 2 * p + par]

    pending = [scores(u) for u in units[:SCORE_LOOKAHEAD]]
    outs = []
    for n, (sb, p, par) in enumerate(units):
        r0, cs = sb * Q_BLOCK, slice(p * LANES, (p + 1) * LANES)
        s = pending.pop(0)
        e = jnp.exp2(s - jnp.max(s, axis=-1, keepdims=True))
        if n + SCORE_LOOKAHEAD < len(units):
            pending.append(scores(units[n + SCORE_LOOKAHEAD]))
        v_ref = v_even_ref if par == 0 else v_odd_ref
        o = jnp.dot(e.astype(BF16), v_ref[r0:r0 + KEY_BLOCK, cs], preferred_element_type=F32)
        outs.append(o / pltpu.roll(o, HEAD_DIM, axis=1))
        if par == 1:
            o_ref[r0:r0 + Q_BLOCK, cs] = jnp.where(low_half, outs[-2], outs[-1]).astype(BF16)


def _mix_ffn2_kernel(tiles_per_seq, x1_ref, up_ref, halo_ref, attn_ref, nm_ref, wg_ref,
                     bg_ref, poolw_ref, pscale_ref, wbp_ref, wba_ref, wout_ref,
                     n2_ref, wgu2_ref, wd_ref, nf_ref,
                     out_ref, hb_ref, act_ref, ext_ref, lvl_a_ref, lvl_b_ref, pm_ref, mg_ref, ga_ref):
    i = pl.program_id(0)
    tile_in_seq = i % tiles_per_seq
    hb_ref[...], rscale = _norm_split(x1_ref[...], nm_ref[...])

    base = POOL_PAD + HALO
    keep_halo = jnp.where(tile_in_seq == 0, 0.0, 1.0).astype(F32)
    lvl_refs = (lvl_a_ref, lvl_b_ref)
    for ref in (ext_ref,) + lvl_refs:
        ref[0:POOL_PAD, :] = jnp.zeros((POOL_PAD, POOL_WIDTH), F32)
    ext_ref[POOL_PAD:base, :] = halo_ref[...] * keep_halo
    ext_ref[base:base + TM, :] = up_ref[...]
    pos_top = tile_in_seq * TM + lax.broadcasted_iota(jnp.int32, (HALO, 1), 0)

    def pool_group(g):
        w = POOL_WINDOWS[g]
        cs = slice(g * POOL_GROUP, (g + 1) * POOL_GROUP)
        src, shift, level = ext_ref, 1, 0
        while 2 * shift < w:
            lvl_ref = lvl_refs[level % 2]
            lvl_ref[POOL_PAD:base + TM, cs] = (src[POOL_PAD:base + TM, cs]
                                              + src[POOL_PAD - shift:base + TM - shift, cs])
            src, shift, level = lvl_ref, 2 * shift, level + 1
        sums = src[base:base + TM, cs] + src[base - shift:base + TM - shift, cs]
        inv_top = 1.0 / jnp.minimum(pos_top + 1, w).astype(F32)
        mean = jnp.concatenate([sums[:HALO] * inv_top, sums[HALO:] * (1.0 / w)], axis=0)
        mixed = (mean - ext_ref[base:base + TM, cs]).astype(BF16)
        y = jnp.dot(mixed, poolw_ref[g], preferred_element_type=F32)
        pm_ref[:, cs] = (y * pscale_ref[:, cs]).astype(BF16)

    col_chunks = [slice(lo, lo + COL_CHUNK) for lo in range(0, D_MODEL, COL_CHUNK)]

    def gate_tanh(cols):
        z_half = jnp.dot(hb_ref[...], wg_ref[:, cols], preferred_element_type=F32)
        return jnp.tanh(_row_scale(z_half, rscale) + 0.5 * bg_ref[:, cols])

    for cs in col_chunks:
        y_half = jnp.dot(attn_ref[...], wba_ref[:, cs], preferred_element_type=F32)
        t1 = gate_tanh(slice(D_MODEL + cs.start, D_MODEL + cs.stop))
        ga_ref[:, cs] = y_half + t1 * y_half
    g0 = [gate_tanh(cs) for cs in col_chunks[:3]]
    for g in (3, 0, 1, 2):
        pool_group(g)

    for c, cs in enumerate(col_chunks):
        if c == len(g0):
            g0.append(gate_tanh(cs))
        y_half = jnp.dot(pm_ref[...], wbp_ref[:, cs], preferred_element_type=F32)
        mg_ref[:, cs] = (y_half + g0[c] * y_half + ga_ref[:, cs]).astype(BF16)
    for cs in col_chunks:
        ga_ref[:, cs] = x1_ref[:, cs] + jnp.dot(mg_ref[...], wout_ref[:, cs],
                                                preferred_element_type=F32)

    hb_ref[...], rscale2 = _norm_split(ga_ref[...], n2_ref[...])
    _swiglu_act(hb_ref, rscale2, wgu2_ref, act_ref)
    hm = TM // 2
    for rows in (slice(0, hm), slice(hm, TM)):
        for cs in col_chunks:
            out_ref[rows, cs] = ga_ref[rows, cs] + jnp.dot(act_ref[rows, :], wd_ref[:, cs],
                                                           preferred_element_type=F32)
        out_ref[rows, :] = _rmsnorm(out_ref[rows, :], nf_ref[...])


def _resident(shape):
    return pl.BlockSpec(shape, lambda i: (0,) * len(shape), pipeline_mode=pl.Buffered(1))


def _toeplitz(w):
    n_heads, width = w.shape
    flat = jnp.tile(jnp.pad(w, ((0, 0), (0, 1))), (1, CHUNK))
    rows = flat[:, :CHUNK * width].reshape(n_heads, CHUNK, width)
    return rows[:, :, CHUNK - 1:]


def _bias_pieces(rel_bias):
    assert MAX_REL == CHUNK
    rb = rel_bias.astype(F32) * LOG2E
    n_heads = rb.shape[0]
    far = rb[:, 2 * MAX_REL]
    t_own = _toeplitz(rb[:, 1:2 * MAX_REL][:, ::-1])
    t_prev = _toeplitz(jnp.concatenate(
        [jnp.broadcast_to(far[:, None], (n_heads, CHUNK)), rb[:, MAX_REL + 1:2 * MAX_REL][:, ::-1]],
        axis=1))
    full = (n_heads, CHUNK, LANES)
    return jnp.stack([jnp.full(full, MASK_VALUE, F32),
                      jnp.broadcast_to(far[:, None, None], full),
                      jnp.concatenate([t_prev, t_prev], axis=2),
                      jnp.concatenate([t_own, t_own], axis=2)], axis=1)


def kernel(x, ffn1_norm, ffn1_w_gate, ffn1_w_up, ffn1_w_down, mix_norm, w_in, pool_w, pool_scale,
           rel_bias, w_branch_pool, w_branch_attn, w_gate, b_gate, w_out, ffn2_norm, ffn2_w_gate,
           ffn2_w_up, ffn2_w_down, final_norm):
    b, s, d = x.shape
    assert d == D_MODEL and s % TM == 0 and ffn1_norm.shape[0] == 1
    n_tok = b * s
    n_tiles = n_tok // TM
    assert n_tiles == N_TILES
    tiles_per_seq = s // TM
    xf = x.reshape(n_tok, d)

    row = lambda v: v.reshape(1, -1).astype(F32)
    cparams = pltpu.CompilerParams(dimension_semantics=("arbitrary",),
                                   vmem_limit_bytes=VMEM_LIMIT)
    tile = lambda width: pl.BlockSpec((TM, width), lambda i: (i, 0))

    tok = lambda i: jnp.maximum(i - N_CAST, 0)
    tile1 = lambda width: pl.BlockSpec((TM, width), lambda i: (tok(i), 0))
    staged = lambda w: pl.BlockSpec((w.shape[0] // N_CAST, w.shape[1]),
                                    lambda i: (jnp.minimum(i, N_CAST - 1), 0))
    own_w = [ffn1_w_gate[0], ffn1_w_up[0], ffn1_w_down[0], w_in[0]]
    side_w = [w_gate[0], w_branch_pool[0], w_branch_attn[0], w_out[0], ffn2_w_down[0],
              pool_w[0].reshape(POOL_WIDTH, POOL_GROUP)]
    gate_up_rows = D_MODEL // n_tiles
    gate_up_spec = lambda width: pl.BlockSpec((gate_up_rows, width), lambda i: (tok(i), 0))

    def side_specs():
        specs = []
        for w, (rows, steps, _) in zip(side_w, SIDE_ROWS):
            assert w.shape[0] == rows
            specs.append(pl.BlockSpec(
                (rows // steps, w.shape[1]),
                functools.partial(lambda steps, i: (jnp.minimum(tok(i), steps - 1), 0), steps)))
        return specs

    res = pl.pallas_call(
        _ffn1_inproj_kernel,
        grid=(N_CAST + n_tiles,),
        in_specs=[tile1(D_MODEL), _resident((1, D_MODEL)), staged(own_w[0]), staged(own_w[1]),
                  staged(own_w[2]), _resident((1, D_MODEL)), staged(own_w[3]),
                  gate_up_spec(D_FF), gate_up_spec(D_FF)] + side_specs(),
        out_specs=[tile1(D_MODEL), tile1(POOL_WIDTH), tile1(3 * ATTN_WIDTH),
                   gate_up_spec(2 * D_FF)] + side_specs(),
        out_shape=[jax.ShapeDtypeStruct((n_tok, D_MODEL), F32),
                   jax.ShapeDtypeStruct((n_tok, POOL_WIDTH), F32),
                   jax.ShapeDtypeStruct((n_tok, 3 * ATTN_WIDTH), BF16),
                   jax.ShapeDtypeStruct((D_MODEL, 2 * D_FF), BF16)]
                  + [jax.ShapeDtypeStruct(w.shape, BF16) for w in side_w],
        scratch_shapes=[pltpu.VMEM((D_MODEL, 2 * D_FF), BF16), pltpu.VMEM((D_FF, D_MODEL), BF16),
                        pltpu.VMEM((D_MODEL, IN_WIDTH), BF16),
                        pltpu.VMEM((TM, D_MODEL), BF16), pltpu.VMEM((TM, D_FF), BF16)],
        compiler_params=cparams,
        name="ffn1_inproj",
    )(xf, row(ffn1_norm), own_w[0], own_w[1], own_w[2], row(mix_norm), own_w[3],
      ffn2_w_gate[0], ffn2_w_up[0], *side_w)
    x1, up, qkv, wgu2_b = res[:4]
    wgate_b, wbp_b, wba_b, wout_b, wd2_b, poolw_b = res[4:]

    assert s % ATTN_TM == 0 and ATTN_TM % ATTN_PREV == 0
    prev = lambda i: jnp.maximum(i * (ATTN_TM // ATTN_PREV) - 1, 0)
    cur_spec = lambda col: pl.BlockSpec((ATTN_TM, ATTN_WIDTH), lambda i: (i, col))
    prev_spec = lambda col: pl.BlockSpec((ATTN_PREV, ATTN_WIDTH), lambda i: (prev(i), col))
    attn = pl.pallas_call(
        functools.partial(_attn_kernel, s // ATTN_TM),
        grid=(n_tok // ATTN_TM,),
        in_specs=[cur_spec(0), prev_spec(1), cur_spec(1), prev_spec(2), cur_spec(2),
                  _resident((N_HEADS, 4, CHUNK, LANES))],
        out_specs=cur_spec(0),
        out_shape=jax.ShapeDtypeStruct((n_tok, ATTN_WIDTH), BF16),
        scratch_shapes=[pltpu.VMEM((1 + ATTN_PREV // Q_BLOCK, N_HEADS, Q_BLOCK, KEY_BLOCK), F32),
                        pltpu.VMEM((ATTN_PREV + ATTN_TM, ATTN_WIDTH), BF16),
                        pltpu.VMEM((ATTN_PREV + ATTN_TM, ATTN_WIDTH), BF16)],
        compiler_params=cparams,
        name="chunk_attn",
    )(qkv, qkv, qkv, qkv, qkv, _bias_pieces(rel_bias[0]))

    halo_blocks = TM // HALO
    out = pl.pallas_call(
        functools.partial(_mix_ffn2_kernel, tiles_per_seq),
        grid=(n_tiles,),
        in_specs=[tile(D_MODEL), tile(POOL_WIDTH),
                  pl.BlockSpec((HALO, POOL_WIDTH),
                               lambda i: (jnp.maximum(i * halo_blocks - 1, 0), 0)),
                  tile(ATTN_WIDTH), _resident((1, D_MODEL)),
                  _resident((D_MODEL, 2 * D_MODEL)), _resident((1, 2 * D_MODEL)),
                  _resident((len(POOL_WINDOWS), POOL_GROUP, POOL_GROUP)),
                  _resident((1, POOL_WIDTH)), _resident((POOL_WIDTH, D_MODEL)),
                  _resident((ATTN_WIDTH, D_MODEL)), _resident((D_MODEL, D_MODEL)),
                  _resident((1, D_MODEL)), _resident((D_MODEL, 2 * D_FF)),
                  _resident((D_FF, D_MODEL)), _resident((1, D_MODEL))],
        out_specs=tile(D_MODEL),
        out_shape=jax.ShapeDtypeStruct((n_tok, D_MODEL), F32),
        scratch_shapes=[pltpu.VMEM((TM, D_MODEL), BF16), pltpu.VMEM((TM, D_FF), BF16),
                        pltpu.VMEM((POOL_PAD + HALO + TM, POOL_WIDTH), F32),
                        pltpu.VMEM((POOL_PAD + HALO + TM, POOL_WIDTH), F32),
                        pltpu.VMEM((POOL_PAD + HALO + TM, POOL_WIDTH), F32),
                        pltpu.VMEM((TM, POOL_WIDTH), BF16), pltpu.VMEM((TM, D_MODEL), BF16),
                        pltpu.VMEM((TM, D_MODEL), F32)],
        compiler_params=cparams,
        name="mix_ffn2",
    )(x1, up, up, attn, row(mix_norm), wgate_b, row(b_gate),
      poolw_b.reshape(len(POOL_WINDOWS), POOL_GROUP, POOL_GROUP), row(pool_scale), wbp_b, wba_b,
      wout_b, row(ffn2_norm), wgu2_b, wd2_b, row(final_norm))
    return out.reshape(b, s, d)
```

```python
import functools

import jax
import jax.numpy as jnp
from jax import lax
from jax.experimental import pallas as pl
from jax.experimental.pallas import tpu as pltpu

D_MODEL = 1024
D_FF = 2816
CHUNK = 64
LEFT_CHUNKS = 8
POOL_WIDTH = 512
POOL_WINDOWS = (2, 4, 8, 16)
POOL_GROUP = 128
N_HEADS = 8
HEAD_DIM = 64
ATTN_WIDTH = 512
MAX_REL = 64
IN_WIDTH = POOL_WIDTH + 3 * ATTN_WIDTH
EPS = 1e-6

LANES = 128
TM = 512
FF_CHUNK = 256
COL_CHUNK = 256
Q_BLOCK = 256
ATTN_PREV = LEFT_CHUNKS * CHUNK
ATTN_TM = 1024
KEY_BLOCK = Q_BLOCK + ATTN_PREV
SCORE_LOOKAHEAD = 2
HALO = 16
POOL_PAD = 8
N_CAST = 8
N_TILES = 32
SIDE_ROWS = ((D_MODEL, N_TILES, 0.5),
             (POOL_WIDTH, N_TILES, 0.5),
             (ATTN_WIDTH, N_TILES, 0.5),
             (D_MODEL, N_TILES, 1.0),
             (D_FF, N_TILES // 2, 0.5),
             (POOL_WIDTH, N_TILES, 1.0))
MASK_VALUE = -2.0 ** 126
LOG2E = 1.4426950408889634
Q_SCALE = HEAD_DIM ** -0.5 * LOG2E
VMEM_LIMIT = 56 * 1024 * 1024

F32 = jnp.float32
BF16 = jnp.bfloat16


def _rmsnorm(x, g):
    ms = jnp.mean(x * x, axis=-1, keepdims=True)
    return x * lax.rsqrt(ms + EPS) * g


def _norm_split(x, g):
    rscale = lax.rsqrt(jnp.mean(x * x, axis=-1, keepdims=True) + EPS)
    return (x * g).astype(BF16), jnp.broadcast_to(rscale, (x.shape[0], LANES))


def _row_scale(y, rscale):
    return jnp.concatenate([y[:, lo:lo + LANES] * rscale for lo in range(0, y.shape[1], LANES)],
                           axis=1)


def _interleave_gate_up(dst_ref, rows, gate_blk, up_blk):
    for c in range(D_FF // FF_CHUNK):
        src = slice(c * FF_CHUNK, (c + 1) * FF_CHUNK)
        dst_ref[rows, 2 * c * FF_CHUNK:(2 * c + 1) * FF_CHUNK] = gate_blk[:, src].astype(BF16)
        dst_ref[rows, (2 * c + 1) * FF_CHUNK:(2 * c + 2) * FF_CHUNK] = up_blk[:, src].astype(BF16)


def _swiglu_act(hb_ref, rscale, wgu_ref, act_ref):
    half_rscale = 0.5 * rscale
    for c in range(D_FF // FF_CHUNK):
        gu = jnp.dot(hb_ref[...], wgu_ref[:, 2 * c * FF_CHUNK:(2 * c + 2) * FF_CHUNK],
                     preferred_element_type=F32)
        hg = _row_scale(gu[:, :FF_CHUNK], half_rscale)
        up = _row_scale(gu[:, FF_CHUNK:], rscale)
        act_ref[:, c * FF_CHUNK:(c + 1) * FF_CHUNK] = ((hg + hg * jnp.tanh(hg)) * up).astype(BF16)


def _ffn1_inproj_kernel(x_ref, n1_ref, wg_blk, wu_blk, wd_blk, nm_ref, win_blk,
                        wg2_blk, wu2_blk, *rest):
    n_side = len(SIDE_ROWS)
    side_in = rest[:n_side]
    x1_ref, up_ref, qkv_ref, wgu2_out = rest[n_side:n_side + 4]
    side_out = rest[n_side + 4:2 * n_side + 4]
    wgu_ref, wd_ref, win_ref, hb_ref, act_ref = rest[2 * n_side + 4:]
    i = pl.program_id(0)

    @pl.when(i < N_CAST)
    def _stage_weights():
        def rows_of(blk):
            return pl.ds(pl.multiple_of(i * blk.shape[0], blk.shape[0]), blk.shape[0])

        _interleave_gate_up(wgu_ref, rows_of(wg_blk), wg_blk, wu_blk)
        wd_ref[rows_of(wd_blk), :] = (wd_blk[...] * 0.5).astype(BF16)
        win_ref[rows_of(win_blk), :] = win_blk[...].astype(BF16)

    @pl.when(i >= N_CAST)
    def _token_tile():
        hb_ref[...], rscale = _norm_split(x_ref[...], n1_ref[...])
        _swiglu_act(hb_ref, rscale, wgu_ref, act_ref)
        for lo in range(0, D_MODEL, COL_CHUNK):
            cs = slice(lo, lo + COL_CHUNK)
            x1_ref[:, cs] = x_ref[:, cs] + jnp.dot(act_ref[...], wd_ref[:, cs],
                                                   preferred_element_type=F32)
        hb_ref[...], rscale = _norm_split(x1_ref[...], nm_ref[...])
        q_rscale = rscale * Q_SCALE
        for lo in range(0, IN_WIDTH, COL_CHUNK):
            y = jnp.dot(hb_ref[...], win_ref[:, lo:lo + COL_CHUNK], preferred_element_type=F32)
            if lo < POOL_WIDTH:
                up_ref[:, lo:lo + COL_CHUNK] = _row_scale(y, rscale)
            else:
                is_q = lo < POOL_WIDTH + ATTN_WIDTH
                qkv_ref[:, lo - POOL_WIDTH:lo - POOL_WIDTH + COL_CHUNK] = _row_scale(
                    y, q_rscale if is_q else rscale).astype(BF16)
        _interleave_gate_up(wgu2_out, slice(None), wg2_blk, wu2_blk)
        for src, dst, (_, _, scale) in zip(side_in, side_out, SIDE_ROWS):
            w = src[...] if scale == 1.0 else src[...] * scale
            dst[...] = w.astype(BF16)


def _band_kind(dchunk):
    if dchunk < 0 or dchunk > LEFT_CHUNKS:
        return 0
    if dchunk < LEFT_CHUNKS - 1:
        return 1
    return 2 if dchunk == LEFT_CHUNKS - 1 else 3


def _attn_kernel(tiles_per_seq, q_ref, kp_ref, kc_ref, vp_ref, vc_ref, pieces_ref,
                 o_ref, bias_ref, v_even_ref, v_odd_ref):
    i = pl.program_id(0)
    is_first = (i % tiles_per_seq) == 0
    n_first_blocks = ATTN_PREV // Q_BLOCK

    @pl.when(i == 0)
    def _build_bias_tables():
        low = lax.broadcasted_iota(jnp.int32, (CHUNK, LANES), 1) < CHUNK

        def per_head(h, carry):
            for variant in range(1 + n_first_blocks):
                n_before = 0 if variant == 0 else (ATTN_PREV - (variant - 1) * Q_BLOCK) // LANES
                for qc in range(Q_BLOCK // CHUNK):
                    for kp in range(KEY_BLOCK // LANES):
                        ka, kb = _band_kind(2 * kp - qc), _band_kind(2 * kp + 1 - qc)
                        if kp < n_before:
                            ka = kb = 0
                        blk = pieces_ref[h, ka]
                        if kb != ka:
                            blk = jnp.where(low, blk, pieces_ref[h, kb])
                        bias_ref[variant, h, qc * CHUNK:(qc + 1) * CHUNK,
                                 kp * LANES:(kp + 1) * LANES] = blk
            return carry

        lax.fori_loop(0, N_HEADS, per_head, 0)

    for src, row0 in ((vp_ref, 0), (vc_ref, ATTN_PREV)):
        v = src[...]
        even_lanes = (lax.broadcasted_iota(jnp.int32, v.shape, 1) % LANES) < HEAD_DIM
        ones = jnp.ones(v.shape, BF16)
        v_even_ref[row0:row0 + v.shape[0], :] = jnp.where(even_lanes, v, ones)
        v_odd_ref[row0:row0 + v.shape[0], :] = jnp.where(even_lanes, ones, v)

    low_half = lax.broadcasted_iota(jnp.int32, (Q_BLOCK, LANES), 1) < HEAD_DIM
    units = [(sb, p, par) for sb in range(ATTN_TM // Q_BLOCK) for p in range(N_HEADS // 2)
             for par in range(2)]

    def scores(unit):
        sb, p, par = unit
        r0, cs = sb * Q_BLOCK, slice(p * LANES, (p + 1) * LANES)
        qp = q_ref[r0:r0 + Q_BLOCK, cs]
        qm = jnp.where(low_half if par == 0 else jnp.logical_not(low_half), qp, jnp.zeros_like(qp))
        nt = (((1,), (1,)), ((), ()))
        parts = []
        if r0 < ATTN_PREV:
            parts.append(lax.dot_general(qm, kp_ref[r0:ATTN_PREV, cs], nt,
                                         preferred_element_type=F32))
        parts.append(lax.dot_general(
            qm, kc_ref[max(r0 - ATTN_PREV, 0):r0 + KEY_BLOCK - ATTN_PREV, cs], nt,
            preferred_element_type=F32))
        s = parts[0] if len(parts) == 1 else jnp.concatenate(parts, axis=1)
        variant = jnp.where(is_first, 1 + sb, 0) if sb < n_first_blocks else 0
        return s + bias_ref[variant, 2 * p + par]

    pending = [scores(u) for u in units[:SCORE_LOOKAHEAD]]
    outs = []
    for n, (sb, p, par) in enumerate(units):
        r0, cs = sb * Q_BLOCK, slice(p * LANES, (p + 1) * LANES)
        s = pending.pop(0)
        e = jnp.exp2(s - jnp.max(s, axis=-1, keepdims=True))
        if n + SCORE_LOOKAHEAD < len(units):
            pending.append(scores(units[n + SCORE_LOOKAHEAD]))
        v_ref = v_even_ref if par == 0 else v_odd_ref
        o = jnp.dot(e.astype(BF16), v_ref[r0:r0 + KEY_BLOCK, cs], preferred_element_type=F32)
        outs.append(o)
        if par == 1:
            num = jnp.where(low_half, outs[-2], outs[-1])
            den = pltpu.roll(jnp.where(low_half, outs[-1], outs[-2]), HEAD_DIM, axis=1)
            o_ref[r0:r0 + Q_BLOCK, cs] = (num / den).astype(BF16)


def _mix_ffn2_kernel(tiles_per_seq, x1_ref, up_ref, halo_ref, attn_ref, nm_ref, wg_ref,
                     bg_ref, poolw_ref, pscale_ref, wbp_ref, wba_ref, wout_ref,
                     n2_ref, wgu2_ref, wd_ref, nf_ref,
                     out_ref, hb_ref, act_ref, ext_ref, lvl_a_ref, lvl_b_ref, pm_ref, mg_ref, ga_ref):
    i = pl.program_id(0)
    tile_in_seq = i % tiles_per_seq
    hb_ref[...], rscale = _norm_split(x1_ref[...], nm_ref[...])

    base = POOL_PAD + HALO
    keep_halo = jnp.where(tile_in_seq == 0, 0.0, 1.0).astype(F32)
    lvl_refs = (lvl_a_ref, lvl_b_ref)
    for ref in (ext_ref,) + lvl_refs:
        ref[0:POOL_PAD, :] = jnp.zeros((POOL_PAD, POOL_WIDTH), F32)
    ext_ref[POOL_PAD:base, :] = halo_ref[...] * keep_halo
    ext_ref[base:base + TM, :] = up_ref[...]
    pos_top = tile_in_seq * TM + lax.broadcasted_iota(jnp.int32, (HALO, 1), 0)

    def pool_group(g):
        w = POOL_WINDOWS[g]
        cs = slice(g * POOL_GROUP, (g + 1) * POOL_GROUP)
        src, shift, level = ext_ref, 1, 0
        while 2 * shift < w:
            lvl_ref = lvl_refs[level % 2]
            lvl_ref[POOL_PAD:base + TM, cs] = (src[POOL_PAD:base + TM, cs]
                                              + src[POOL_PAD - shift:base + TM - shift, cs])
            src, shift, level = lvl_ref, 2 * shift, level + 1
        sums = src[base:base + TM, cs] + src[base - shift:base + TM - shift, cs]
        inv_top = 1.0 / jnp.minimum(pos_top + 1, w).astype(F32)
        mean = jnp.concatenate([sums[:HALO] * inv_top, sums[HALO:] * (1.0 / w)], axis=0)
        mixed = (mean - ext_ref[base:base + TM, cs]).astype(BF16)
        y = jnp.dot(mixed, poolw_ref[g], preferred_element_type=F32)
        pm_ref[:, cs] = (y * pscale_ref[:, cs]).astype(BF16)

    col_chunks = [slice(lo, lo + COL_CHUNK) for lo in range(0, D_MODEL, COL_CHUNK)]

    def gate_tanh(cols):
        z_half = jnp.dot(hb_ref[...], wg_ref[:, cols], preferred_element_type=F32)
        return jnp.tanh(_row_scale(z_half, rscale) + 0.5 * bg_ref[:, cols])

    for cs in col_chunks:
        y_half = jnp.dot(attn_ref[...], wba_ref[:, cs], preferred_element_type=F32)
        t1 = gate_tanh(slice(D_MODEL + cs.start, D_MODEL + cs.stop))
        ga_ref[:, cs] = y_half + t1 * y_half
    g0 = [gate_tanh(cs) for cs in col_chunks[:2]]
    for g in (3, 0, 1, 2):
        pool_group(g)

    for c, cs in enumerate(col_chunks):
        if c == len(g0):
            g0.append(gate_tanh(cs))
        y_half = jnp.dot(pm_ref[...], wbp_ref[:, cs], preferred_element_type=F32)
        mg_ref[:, cs] = (y_half + g0[c] * y_half + ga_ref[:, cs]).astype(BF16)
    for cs in col_chunks:
        ga_ref[:, cs] = x1_ref[:, cs] + jnp.dot(mg_ref[...], wout_ref[:, cs],
                                                preferred_element_type=F32)

    hb_ref[...], rscale2 = _norm_split(ga_ref[...], n2_ref[...])
    _swiglu_act(hb_ref, rscale2, wgu2_ref, act_ref)
    hm = TM // 2
    for rows in (slice(0, hm), slice(hm, TM)):
        for cs in col_chunks:
            out_ref[rows, cs] = ga_ref[rows, cs] + jnp.dot(act_ref[rows, :], wd_ref[:, cs],
                                                           preferred_element_type=F32)
        out_ref[rows, :] = _rmsnorm(out_ref[rows, :], nf_ref[...])


def _resident(shape):
    return pl.BlockSpec(shape, lambda i: (0,) * len(shape), pipeline_mode=pl.Buffered(1))


def _toeplitz(w):
    n_heads, width = w.shape
    flat = jnp.tile(jnp.pad(w, ((0, 0), (0, 1))), (1, CHUNK))
    rows = flat[:, :CHUNK * width].reshape(n_heads, CHUNK, width)
    return rows[:, :, CHUNK - 1:]


def _bias_pieces(rel_bias):
    assert MAX_REL == CHUNK
    rb = rel_bias.astype(F32) * LOG2E
    n_heads = rb.shape[0]
    far = rb[:, 2 * MAX_REL]
    t_own = _toeplitz(rb[:, 1:2 * MAX_REL][:, ::-1])
    t_prev = _toeplitz(jnp.concatenate(
        [jnp.broadcast_to(far[:, None], (n_heads, CHUNK)), rb[:, MAX_REL + 1:2 * MAX_REL][:, ::-1]],
        axis=1))
    full = (n_heads, CHUNK, LANES)
    return jnp.stack([jnp.full(full, MASK_VALUE, F32),
                      jnp.broadcast_to(far[:, None, None], full),
                      jnp.concatenate([t_prev, t_prev], axis=2),
                      jnp.concatenate([t_own, t_own], axis=2)], axis=1)


def kernel(x, ffn1_norm, ffn1_w_gate, ffn1_w_up, ffn1_w_down, mix_norm, w_in, pool_w, pool_scale,
           rel_bias, w_branch_pool, w_branch_attn, w_gate, b_gate, w_out, ffn2_norm, ffn2_w_gate,
           ffn2_w_up, ffn2_w_down, final_norm):
    b, s, d = x.shape
    assert d == D_MODEL and s % TM == 0 and ffn1_norm.shape[0] == 1
    n_tok = b * s
    n_tiles = n_tok // TM
    assert n_tiles == N_TILES
    tiles_per_seq = s // TM
    xf = x.reshape(n_tok, d)

    row = lambda v: v.reshape(1, -1).astype(F32)
    cparams = pltpu.CompilerParams(dimension_semantics=("arbitrary",),
                                   vmem_limit_bytes=VMEM_LIMIT)
    tile = lambda width: pl.BlockSpec((TM, width), lambda i: (i, 0))

    tok = lambda i: jnp.maximum(i - N_CAST, 0)
    tile1 = lambda width: pl.BlockSpec((TM, width), lambda i: (tok(i), 0))
    staged = lambda w: pl.BlockSpec((w.shape[0] // N_CAST, w.shape[1]),
                                    lambda i: (jnp.minimum(i, N_CAST - 1), 0))
    own_w = [ffn1_w_gate[0], ffn1_w_up[0], ffn1_w_down[0], w_in[0]]
    side_w = [w_gate[0], w_branch_pool[0], w_branch_attn[0], w_out[0], ffn2_w_down[0],
              pool_w[0].reshape(POOL_WIDTH, POOL_GROUP)]
    gate_up_rows = D_MODEL // n_tiles
    gate_up_spec = lambda width: pl.BlockSpec((gate_up_rows, width), lambda i: (tok(i), 0))

    def side_specs():
        specs = []
        for w, (rows, steps, _) in zip(side_w, SIDE_ROWS):
            assert w.shape[0] == rows
            specs.append(pl.BlockSpec(
                (rows // steps, w.shape[1]),
                functools.partial(lambda steps, i: (jnp.minimum(tok(i), steps - 1), 0), steps)))
        return specs

    res = pl.pallas_call(
        _ffn1_inproj_kernel,
        grid=(N_CAST + n_tiles,),
        in_specs=[tile1(D_MODEL), _resident((1, D_MODEL)), staged(own_w[0]), staged(own_w[1]),
                  staged(own_w[2]), _resident((1, D_MODEL)), staged(own_w[3]),
                  gate_up_spec(D_FF), gate_up_spec(D_FF)] + side_specs(),
        out_specs=[tile1(D_MODEL), tile1(POOL_WIDTH), tile1(3 * ATTN_WIDTH),
                   gate_up_spec(2 * D_FF)] + side_specs(),
        out_shape=[jax.ShapeDtypeStruct((n_tok, D_MODEL), F32),
                   jax.ShapeDtypeStruct((n_tok, POOL_WIDTH), F32),
                   jax.ShapeDtypeStruct((n_tok, 3 * ATTN_WIDTH), BF16),
                   jax.ShapeDtypeStruct((D_MODEL, 2 * D_FF), BF16)]
                  + [jax.ShapeDtypeStruct(w.shape, BF16) for w in side_w],
        scratch_shapes=[pltpu.VMEM((D_MODEL, 2 * D_FF), BF16), pltpu.VMEM((D_FF, D_MODEL), BF16),
                        pltpu.VMEM((D_MODEL, IN_WIDTH), BF16),
                        pltpu.VMEM((TM, D_MODEL), BF16), pltpu.VMEM((TM, D_FF), BF16)],
        compiler_params=cparams,
        name="ffn1_inproj",
    )(xf, row(ffn1_norm), own_w[0], own_w[1], own_w[2], row(mix_norm), own_w[3],
      ffn2_w_gate[0], ffn2_w_up[0], *side_w)
    x1, up, qkv, wgu2_b = res[:4]
    wgate_b, wbp_b, wba_b, wout_b, wd2_b, poolw_b = res[4:]

    assert s % ATTN_TM == 0 and ATTN_TM % ATTN_PREV == 0
    prev = lambda i: jnp.maximum(i * (ATTN_TM // ATTN_PREV) - 1, 0)
    cur_spec = lambda col: pl.BlockSpec((ATTN_TM, ATTN_WIDTH), lambda i: (i, col))
    prev_spec = lambda col: pl.BlockSpec((ATTN_PREV, ATTN_WIDTH), lambda i: (prev(i), col))
    attn = pl.pallas_call(
        functools.partial(_attn_kernel, s // ATTN_TM),
        grid=(n_tok // ATTN_TM,),
        in_specs=[cur_spec(0), prev_spec(1), cur_spec(1), prev_spec(2), cur_spec(2),
                  _resident((N_HEADS, 4, CHUNK, LANES))],
        out_specs=cur_spec(0),
        out_shape=jax.ShapeDtypeStruct((n_tok, ATTN_WIDTH), BF16),
        scratch_shapes=[pltpu.VMEM((1 + ATTN_PREV // Q_BLOCK, N_HEADS, Q_BLOCK, KEY_BLOCK), F32),
                        pltpu.VMEM((ATTN_PREV + ATTN_TM, ATTN_WIDTH), BF16),
                        pltpu.VMEM((ATTN_PREV + ATTN_TM, ATTN_WIDTH), BF16)],
        compiler_params=cparams,
        name="chunk_attn",
    )(qkv, qkv, qkv, qkv, qkv, _bias_pieces(rel_bias[0]))

    halo_blocks = TM // HALO
    out = pl.pallas_call(
        functools.partial(_mix_ffn2_kernel, tiles_per_seq),
        grid=(n_tiles,),
        in_specs=[tile(D_MODEL), tile(POOL_WIDTH),
                  pl.BlockSpec((HALO, POOL_WIDTH),
                               lambda i: (jnp.maximum(i * halo_blocks - 1, 0), 0)),
                  tile(ATTN_WIDTH), _resident((1, D_MODEL)),
                  _resident((D_MODEL, 2 * D_MODEL)), _resident((1, 2 * D_MODEL)),
                  _resident((len(POOL_WINDOWS), POOL_GROUP, POOL_GROUP)),
                  _resident((1, POOL_WIDTH)), _resident((POOL_WIDTH, D_MODEL)),
                  _resident((ATTN_WIDTH, D_MODEL)), _resident((D_MODEL, D_MODEL)),
                  _resident((1, D_MODEL)), _resident((D_MODEL, 2 * D_FF)),
                  _resident((D_FF, D_MODEL)), _resident((1, D_MODEL))],
        out_specs=tile(D_MODEL),
        out_shape=jax.ShapeDtypeStruct((n_tok, D_MODEL), F32),
        scratch_shapes=[pltpu.VMEM((TM, D_MODEL), BF16), pltpu.VMEM((TM, D_FF), BF16),
                        pltpu.VMEM((POOL_PAD + HALO + TM, POOL_WIDTH), F32),
                        pltpu.VMEM((POOL_PAD + HALO + TM, POOL_WIDTH), F32),
                        pltpu.VMEM((POOL_PAD + HALO + TM, POOL_WIDTH), F32),
                        pltpu.VMEM((TM, POOL_WIDTH), BF16), pltpu.VMEM((TM, D_MODEL), BF16),
                        pltpu.VMEM((TM, D_MODEL), F32)],
        compiler_params=cparams,
        name="mix_ffn2",
    )(x1, up, up, attn, row(mix_norm), wgate_b, row(b_gate),
      poolw_b.reshape(len(POOL_WINDOWS), POOL_GROUP, POOL_GROUP), row(pool_scale), wbp_b, wba_b,
      wout_b, row(ffn2_norm), wgu2_b, wd2_b, row(final_norm))
    return out.reshape(b, s, d)
```

```python
import functools

import jax
import jax.numpy as jnp
from jax import lax
from jax.experimental import pallas as pl
from jax.experimental.pallas import tpu as pltpu

D_MODEL = 1024
D_FF = 2816
CHUNK = 64
LEFT_CHUNKS = 8
POOL_WIDTH = 512
POOL_WINDOWS = (2, 4, 8, 16)
POOL_GROUP = 128
N_HEADS = 8
HEAD_DIM = 64
ATTN_WIDTH = 512
MAX_REL = 64
IN_WIDTH = POOL_WIDTH + 3 * ATTN_WIDTH
EPS = 1e-6

LANES = 128
TM = 512
FF_CHUNK = 256
COL_CHUNK = 256
Q_BLOCK = 256
ATTN_PREV = LEFT_CHUNKS * CHUNK
ATTN_TM = 1024
KEY_BLOCK = Q_BLOCK + ATTN_PREV
SCORE_LOOKAHEAD = 2
HALO = 16
POOL_PAD = 8
N_CAST = 8
N_TILES = 32
SIDE_ROWS = ((D_MODEL, N_TILES, 0.5),
             (POOL_WIDTH, N_TILES, 0.5),
             (ATTN_WIDTH, N_TILES, 0.5),
             (D_MODEL, N_TILES, 1.0),
             (D_FF, N_TILES // 2, 0.5),
             (POOL_WIDTH, N_TILES, 1.0))
MASK_VALUE = -2.0 ** 126
LOG2E = 1.4426950408889634
Q_SCALE = HEAD_DIM ** -0.5 * LOG2E
VMEM_LIMIT = 56 * 1024 * 1024

F32 = jnp.float32
BF16 = jnp.bfloat16


def _rmsnorm(x, g):
    ms = jnp.mean(x * x, axis=-1, keepdims=True)
    return x * lax.rsqrt(ms + EPS) * g


def _norm_split(x, g):
    rscale = lax.rsqrt(jnp.mean(x * x, axis=-1, keepdims=True) + EPS)
    return (x * g).astype(BF16), jnp.broadcast_to(rscale, (x.shape[0], LANES))


def _row_scale(y, rscale):
    return jnp.concatenate([y[:, lo:lo + LANES] * rscale for lo in range(0, y.shape[1], LANES)],
                           axis=1)


def _interleave_gate_up(dst_ref, rows, gate_blk, up_blk):
    for c in range(D_FF // FF_CHUNK):
        src = slice(c * FF_CHUNK, (c + 1) * FF_CHUNK)
        dst_ref[rows, 2 * c * FF_CHUNK:(2 * c + 1) * FF_CHUNK] = gate_blk[:, src].astype(BF16)
        dst_ref[rows, (2 * c + 1) * FF_CHUNK:(2 * c + 2) * FF_CHUNK] = up_blk[:, src].astype(BF16)


def _swiglu_act(hb_ref, rscale, wgu_ref, act_ref):
    half_rscale = 0.5 * rscale
    for c in range(D_FF // FF_CHUNK):
        gu = jnp.dot(hb_ref[...], wgu_ref[:, 2 * c * FF_CHUNK:(2 * c + 2) * FF_CHUNK],
                     preferred_element_type=F32)
        hg = _row_scale(gu[:, :FF_CHUNK], half_rscale)
        up = _row_scale(gu[:, FF_CHUNK:], rscale)
        act_ref[:, c * FF_CHUNK:(c + 1) * FF_CHUNK] = ((hg + hg * jnp.tanh(hg)) * up).astype(BF16)


def _ffn1_inproj_kernel(x_ref, n1_ref, wg_blk, wu_blk, wd_blk, nm_ref, win_blk,
                        wg2_blk, wu2_blk, *rest):
    n_side = len(SIDE_ROWS)
    side_in = rest[:n_side]
    x1_ref, up_ref, qkv_ref, wgu2_out = rest[n_side:n_side + 4]
    side_out = rest[n_side + 4:2 * n_side + 4]
    wgu_ref, wd_ref, win_ref, hb_ref, act_ref = rest[2 * n_side + 4:]
    i = pl.program_id(0)

    @pl.when(i < N_CAST)
    def _stage_weights():
        def rows_of(blk):
            return pl.ds(pl.multiple_of(i * blk.shape[0], blk.shape[0]), blk.shape[0])

        _interleave_gate_up(wgu_ref, rows_of(wg_blk), wg_blk, wu_blk)
        wd_ref[rows_of(wd_blk), :] = (wd_blk[...] * 0.5).astype(BF16)
        win_ref[rows_of(win_blk), :] = win_blk[...].astype(BF16)

    @pl.when(i >= N_CAST)
    def _token_tile():
        hb_ref[...], rscale = _norm_split(x_ref[...], n1_ref[...])
        _swiglu_act(hb_ref, rscale, wgu_ref, act_ref)
        for lo in range(0, D_MODEL, COL_CHUNK):
            cs = slice(lo, lo + COL_CHUNK)
            x1_ref[:, cs] = x_ref[:, cs] + jnp.dot(act_ref[...], wd_ref[:, cs],
                                                   preferred_element_type=F32)
        hb_ref[...], rscale = _norm_split(x1_ref[...], nm_ref[...])
        q_rscale = rscale * Q_SCALE
        for lo in range(0, IN_WIDTH, COL_CHUNK):
            y = jnp.dot(hb_ref[...], win_ref[:, lo:lo + COL_CHUNK], preferred_element_type=F32)
            if lo < POOL_WIDTH:
                up_ref[:, lo:lo + COL_CHUNK] = _row_scale(y, rscale)
            else:
                is_q = lo < POOL_WIDTH + ATTN_WIDTH
                qkv_ref[:, lo - POOL_WIDTH:lo - POOL_WIDTH + COL_CHUNK] = _row_scale(
                    y, q_rscale if is_q else rscale).astype(BF16)
        _interleave_gate_up(wgu2_out, slice(None), wg2_blk, wu2_blk)
        for src, dst, (_, _, scale) in zip(side_in, side_out, SIDE_ROWS):
            w = src[...] if scale == 1.0 else src[...] * scale
            dst[...] = w.astype(BF16)


def _band_kind(dchunk):
    if dchunk < 0 or dchunk > LEFT_CHUNKS:
        return 0
    if dchunk < LEFT_CHUNKS - 1:
        return 1
    return 2 if dchunk == LEFT_CHUNKS - 1 else 3


def _attn_kernel(tiles_per_seq, q_ref, kp_ref, kc_ref, vp_ref, vc_ref, pieces_ref,
                 o_ref, bias_ref, v_aug_ref):
    i = pl.program_id(0)
    is_first = (i % tiles_per_seq) == 0
    n_first_blocks = ATTN_PREV // Q_BLOCK

    @pl.when(i == 0)
    def _build_bias_tables():
        low = lax.broadcasted_iota(jnp.int32, (CHUNK, LANES), 1) < CHUNK

        def per_head(h, carry):
            for variant in range(1 + n_first_blocks):
                n_before = 0 if variant == 0 else (ATTN_PREV - (variant - 1) * Q_BLOCK) // LANES
                for qc in range(Q_BLOCK // CHUNK):
                    for kp in range(KEY_BLOCK // LANES):
                        ka, kb = _band_kind(2 * kp - qc), _band_kind(2 * kp + 1 - qc)
                        if kp < n_before:
                            ka = kb = 0
                        blk = pieces_ref[h, ka]
                        if kb != ka:
                            blk = jnp.where(low, blk, pieces_ref[h, kb])
                        bias_ref[variant, h, qc * CHUNK:(qc + 1) * CHUNK,
                                 kp * LANES:(kp + 1) * LANES] = blk
            return carry

        lax.fori_loop(0, N_HEADS, per_head, 0)

    for src, row0 in ((vp_ref, 0), (vc_ref, ATTN_PREV)):
        n_rows = src.shape[0]
        low_v = lax.broadcasted_iota(jnp.int32, (n_rows, LANES), 1) < HEAD_DIM
        ones = jnp.ones((n_rows, LANES), BF16)
        for p in range(N_HEADS // 2):
            v = src[:, p * LANES:(p + 1) * LANES]
            v_aug_ref[row0:row0 + n_rows, 2 * p * LANES:(2 * p + 1) * LANES] = jnp.where(low_v, v, ones)
            v_aug_ref[row0:row0 + n_rows, (2 * p + 1) * LANES:(2 * p + 2) * LANES] = jnp.where(
                low_v, ones, v)

    low_half = lax.broadcasted_iota(jnp.int32, (Q_BLOCK, LANES), 1) < HEAD_DIM
    units = [(sb, p, par) for sb in range(ATTN_TM // Q_BLOCK) for p in range(N_HEADS // 2)
             for par in range(2)]

    def scores(unit):
        sb, p, par = unit
        r0, cs = sb * Q_BLOCK, slice(p * LANES, (p + 1) * LANES)
        qp = q_ref[r0:r0 + Q_BLOCK, cs]
        qm = jnp.where(low_half if par == 0 else jnp.logical_not(low_half), qp, jnp.zeros_like(qp))
        nt = (((1,), (1,)), ((), ()))
        parts = []
        if r0 < ATTN_PREV:
            parts.append(lax.dot_general(qm, kp_ref[r0:ATTN_PREV, cs], nt,
                                         preferred_element_type=F32))
        parts.append(lax.dot_general(
            qm, kc_ref[max(r0 - ATTN_PREV, 0):r0 + KEY_BLOCK - ATTN_PREV, cs], nt,
            preferred_element_type=F32))
        s = parts[0] if len(parts) == 1 else jnp.concatenate(parts, axis=1)
        variant = jnp.where(is_first, 1 + sb, 0) if sb < n_first_blocks else 0
        return s + bias_ref[variant, 2 * p + par]

    pending = [scores(u) for u in units[:SCORE_LOOKAHEAD]]
    outs = []
    for n, (sb, p, par) in enumerate(units):
        r0, cs = sb * Q_BLOCK, slice(p * LANES, (p + 1) * LANES)
        s = pending.pop(0)
        outs.append(jnp.exp2(s - jnp.max(s, axis=-1, keepdims=True)).astype(BF16))
        if n + SCORE_LOOKAHEAD < len(units):
            pending.append(scores(units[n + SCORE_LOOKAHEAD]))
        if par == 1:
            o = jnp.dot(jnp.concatenate(outs[-2:], axis=0),
                        v_aug_ref[r0:r0 + KEY_BLOCK, 2 * p * LANES:(2 * p + 2) * LANES],
                        preferred_element_type=F32)
            o_even, o_odd = o[:Q_BLOCK, :LANES], o[Q_BLOCK:, LANES:]
            num = jnp.where(low_half, o_even, o_odd)
            den = pltpu.roll(jnp.where(low_half, o_odd, o_even), HEAD_DIM, axis=1)
            o_ref[r0:r0 + Q_BLOCK, cs] = (num / den).astype(BF16)


def _mix_ffn2_kernel(tiles_per_seq, x1_ref, up_ref, halo_ref, attn_ref, nm_ref, wg_ref,
                     bg_ref, poolw_ref, pscale_ref, wbp_ref, wba_ref, wout_ref,
                     n2_ref, wgu2_ref, wd_ref, nf_ref,
                     out_ref, hb_ref, act_ref, ext_ref, lvl_a_ref, lvl_b_ref, pm_ref, mg_ref, ga_ref):
    i = pl.program_id(0)
    tile_in_seq = i % tiles_per_seq
    hb_ref[...], rscale = _norm_split(x1_ref[...], nm_ref[...])

    base = POOL_PAD + HALO
    keep_halo = jnp.where(tile_in_seq == 0, 0.0, 1.0).astype(F32)
    lvl_refs = (lvl_a_ref, lvl_b_ref)
    for ref in (ext_ref,) + lvl_refs:
        ref[0:POOL_PAD, :] = jnp.zeros((POOL_PAD, POOL_WIDTH), F32)
    ext_ref[POOL_PAD:base, :] = halo_ref[...] * keep_halo
    ext_ref[base:base + TM, :] = up_ref[...]
    pos_top = tile_in_seq * TM + lax.broadcasted_iota(jnp.int32, (HALO, 1), 0)

    def pool_group(g):
        w = POOL_WINDOWS[g]
        cs = slice(g * POOL_GROUP, (g + 1) * POOL_GROUP)
        src, shift, level = ext_ref, 1, 0
        while 2 * shift < w:
            lvl_ref = lvl_refs[level % 2]
            lvl_ref[POOL_PAD:base + TM, cs] = (src[POOL_PAD:base + TM, cs]
                                              + src[POOL_PAD - shift:base + TM - shift, cs])
            src, shift, level = lvl_ref, 2 * shift, level + 1
        sums = src[base:base + TM, cs] + src[base - shift:base + TM - shift, cs]
        inv_top = 1.0 / jnp.minimum(pos_top + 1, w).astype(F32)
        mean = jnp.concatenate([sums[:HALO] * inv_top, sums[HALO:] * (1.0 / w)], axis=0)
        mixed = (mean - ext_ref[base:base + TM, cs]).astype(BF16)
        y = jnp.dot(mixed, poolw_ref[g], preferred_element_type=F32)
        pm_ref[:, cs] = (y * pscale_ref[:, cs]).astype(BF16)

    col_chunks = [slice(lo, lo + COL_CHUNK) for lo in range(0, D_MODEL, COL_CHUNK)]

    def gate_tanh(cols):
        z_half = jnp.dot(hb_ref[...], wg_ref[:, cols], preferred_element_type=F32)
        return jnp.tanh(_row_scale(z_half, rscale) + 0.5 * bg_ref[:, cols])

    for cs in col_chunks:
        y_half = jnp.dot(attn_ref[...], wba_ref[:, cs], preferred_element_type=F32)
        t1 = gate_tanh(slice(D_MODEL + cs.start, D_MODEL + cs.stop))
        ga_ref[:, cs] = y_half + t1 * y_half
    g0 = [gate_tanh(cs) for cs in col_chunks[:2]]
    for g in (3, 0, 1, 2):
        pool_group(g)

    for c, cs in enumerate(col_chunks):
        if c == len(g0):
            g0.append(gate_tanh(cs))
        y_half = jnp.dot(pm_ref[...], wbp_ref[:, cs], preferred_element_type=F32)
        mg_ref[:, cs] = (y_half + g0[c] * y_half + ga_ref[:, cs]).astype(BF16)
    for cs in col_chunks:
        ga_ref[:, cs] = x1_ref[:, cs] + jnp.dot(mg_ref[...], wout_ref[:, cs],
                                                preferred_element_type=F32)

    hb_ref[...], rscale2 = _norm_split(ga_ref[...], n2_ref[...])
    _swiglu_act(hb_ref, rscale2, wgu2_ref, act_ref)
    hm = TM // 2
    for rows in (slice(0, hm), slice(hm, TM)):
        for cs in col_chunks:
            out_ref[rows, cs] = ga_ref[rows, cs] + jnp.dot(act_ref[rows, :], wd_ref[:, cs],
                                                           preferred_element_type=F32)
        out_ref[rows, :] = _rmsnorm(out_ref[rows, :], nf_ref[...])


def _resident(shape):
    return pl.BlockSpec(shape, lambda i: (0,) * len(shape), pipeline_mode=pl.Buffered(1))


def _toeplitz(w):
    n_heads, width = w.shape
    flat = jnp.tile(jnp.pad(w, ((0, 0), (0, 1))), (1, CHUNK))
    rows = flat[:, :CHUNK * width].reshape(n_heads, CHUNK, width)
    return rows[:, :, CHUNK - 1:]


def _bias_pieces(rel_bias):
    assert MAX_REL == CHUNK
    rb = rel_bias.astype(F32) * LOG2E
    n_heads = rb.shape[0]
    far = rb[:, 2 * MAX_REL]
    t_own = _toeplitz(rb[:, 1:2 * MAX_REL][:, ::-1])
    t_prev = _toeplitz(jnp.concatenate(
        [jnp.broadcast_to(far[:, None], (n_heads, CHUNK)), rb[:, MAX_REL + 1:2 * MAX_REL][:, ::-1]],
        axis=1))
    full = (n_heads, CHUNK, LANES)
    return jnp.stack([jnp.full(full, MASK_VALUE, F32),
                      jnp.broadcast_to(far[:, None, None], full),
                      jnp.concatenate([t_prev, t_prev], axis=2),
                      jnp.concatenate([t_own, t_own], axis=2)], axis=1)


def kernel(x, ffn1_norm, ffn1_w_gate, ffn1_w_up, ffn1_w_down, mix_norm, w_in, pool_w, pool_scale,
           rel_bias, w_branch_pool, w_branch_attn, w_gate, b_gate, w_out, ffn2_norm, ffn2_w_gate,
           ffn2_w_up, ffn2_w_down, final_norm):
    b, s, d = x.shape
    assert d == D_MODEL and s % TM == 0 and ffn1_norm.shape[0] == 1
    n_tok = b * s
    n_tiles = n_tok // TM
    assert n_tiles == N_TILES
    tiles_per_seq = s // TM
    xf = x.reshape(n_tok, d)

    row = lambda v: v.reshape(1, -1).astype(F32)
    cparams = pltpu.CompilerParams(dimension_semantics=("arbitrary",),
                                   vmem_limit_bytes=VMEM_LIMIT)
    tile = lambda width: pl.BlockSpec((TM, width), lambda i: (i, 0))

    tok = lambda i: jnp.maximum(i - N_CAST, 0)
    tile1 = lambda width: pl.BlockSpec((TM, width), lambda i: (tok(i), 0))
    staged = lambda w: pl.BlockSpec((w.shape[0] // N_CAST, w.shape[1]),
                                    lambda i: (jnp.minimum(i, N_CAST - 1), 0))
    own_w = [ffn1_w_gate[0], ffn1_w_up[0], ffn1_w_down[0], w_in[0]]
    side_w = [w_gate[0], w_branch_pool[0], w_branch_attn[0], w_out[0], ffn2_w_down[0],
              pool_w[0].reshape(POOL_WIDTH, POOL_GROUP)]
    gate_up_rows = D_MODEL // n_tiles
    gate_up_spec = lambda width: pl.BlockSpec((gate_up_rows, width), lambda i: (tok(i), 0))

    def side_specs():
        specs = []
        for w, (rows, steps, _) in zip(side_w, SIDE_ROWS):
            assert w.shape[0] == rows
            specs.append(pl.BlockSpec(
                (rows // steps, w.shape[1]),
                functools.partial(lambda steps, i: (jnp.minimum(tok(i), steps - 1), 0), steps)))
        return specs

    res = pl.pallas_call(
        _ffn1_inproj_kernel,
        grid=(N_CAST + n_tiles,),
        in_specs=[tile1(D_MODEL), _resident((1, D_MODEL)), staged(own_w[0]), staged(own_w[1]),
                  staged(own_w[2]), _resident((1, D_MODEL)), staged(own_w[3]),
                  gate_up_spec(D_FF), gate_up_spec(D_FF)] + side_specs(),
        out_specs=[tile1(D_MODEL), tile1(POOL_WIDTH), tile1(3 * ATTN_WIDTH),
                   gate_up_spec(2 * D_FF)] + side_specs(),
        out_shape=[jax.ShapeDtypeStruct((n_tok, D_MODEL), F32),
                   jax.ShapeDtypeStruct((n_tok, POOL_WIDTH), F32),
                   jax.ShapeDtypeStruct((n_tok, 3 * ATTN_WIDTH), BF16),
                   jax.ShapeDtypeStruct((D_MODEL, 2 * D_FF), BF16)]
                  + [jax.ShapeDtypeStruct(w.shape, BF16) for w in side_w],
        scratch_shapes=[pltpu.VMEM((D_MODEL, 2 * D_FF), BF16), pltpu.VMEM((D_FF, D_MODEL), BF16),
                        pltpu.VMEM((D_MODEL, IN_WIDTH), BF16),
                        pltpu.VMEM((TM, D_MODEL), BF16), pltpu.VMEM((TM, D_FF), BF16)],
        compiler_params=cparams,
        name="ffn1_inproj",
    )(xf, row(ffn1_norm), own_w[0], own_w[1], own_w[2], row(mix_norm), own_w[3],
      ffn2_w_gate[0], ffn2_w_up[0], *side_w)
    x1, up, qkv, wgu2_b = res[:4]
    wgate_b, wbp_b, wba_b, wout_b, wd2_b, poolw_b = res[4:]

    assert s % ATTN_TM == 0 and ATTN_TM % ATTN_PREV == 0
    prev = lambda i: jnp.maximum(i * (ATTN_TM // ATTN_PREV) - 1, 0)
    cur_spec = lambda col: pl.BlockSpec((ATTN_TM, ATTN_WIDTH), lambda i: (i, col))
    prev_spec = lambda col: pl.BlockSpec((ATTN_PREV, ATTN_WIDTH), lambda i: (prev(i), col))
    attn = pl.pallas_call(
        functools.partial(_attn_kernel, s // ATTN_TM),
        grid=(n_tok // ATTN_TM,),
        in_specs=[cur_spec(0), prev_spec(1), cur_spec(1), prev_spec(2), cur_spec(2),
                  _resident((N_HEADS, 4, CHUNK, LANES))],
        out_specs=cur_spec(0),
        out_shape=jax.ShapeDtypeStruct((n_tok, ATTN_WIDTH), BF16),
        scratch_shapes=[pltpu.VMEM((1 + ATTN_PREV // Q_BLOCK, N_HEADS, Q_BLOCK, KEY_BLOCK), F32),
                        pltpu.VMEM((ATTN_PREV + ATTN_TM, 2 * ATTN_WIDTH), BF16)],
        compiler_params=cparams,
        name="chunk_attn",
    )(qkv, qkv, qkv, qkv, qkv, _bias_pieces(rel_bias[0]))

    halo_blocks = TM // HALO
    out = pl.pallas_call(
        functools.partial(_mix_ffn2_kernel, tiles_per_seq),
        grid=(n_tiles,),
        in_specs=[tile(D_MODEL), tile(POOL_WIDTH),
                  pl.BlockSpec((HALO, POOL_WIDTH),
                               lambda i: (jnp.maximum(i * halo_blocks - 1, 0), 0)),
                  tile(ATTN_WIDTH), _resident((1, D_MODEL)),
                  _resident((D_MODEL, 2 * D_MODEL)), _resident((1, 2 * D_MODEL)),
                  _resident((len(POOL_WINDOWS), POOL_GROUP, POOL_GROUP)),
                  _resident((1, POOL_WIDTH)), _resident((POOL_WIDTH, D_MODEL)),
                  _resident((ATTN_WIDTH, D_MODEL)), _resident((D_MODEL, D_MODEL)),
                  _resident((1, D_MODEL)), _resident((D_MODEL, 2 * D_FF)),
                  _resident((D_FF, D_MODEL)), _resident((1, D_MODEL))],
        out_specs=tile(D_MODEL),
        out_shape=jax.ShapeDtypeStruct((n_tok, D_MODEL), F32),
        scratch_shapes=[pltpu.VMEM((TM, D_MODEL), BF16), pltpu.VMEM((TM, D_FF), BF16),
                        pltpu.VMEM((POOL_PAD + HALO + TM, POOL_WIDTH), F32),
                        pltpu.VMEM((POOL_PAD + HALO + TM, POOL_WIDTH), F32),
                        pltpu.VMEM((POOL_PAD + HALO + TM, POOL_WIDTH), F32),
                        pltpu.VMEM((TM, POOL_WIDTH), BF16), pltpu.VMEM((TM, D_MODEL), BF16),
                        pltpu.VMEM((TM, D_MODEL), F32)],
        compiler_params=cparams,
        name="mix_ffn2",
    )(x1, up, up, attn, row(mix_norm), wgate_b, row(b_gate),
      poolw_b.reshape(len(POOL_WINDOWS), POOL_GROUP, POOL_GROUP), row(pool_scale), wbp_b, wba_b,
      wout_b, row(ffn2_norm), wgu2_b, wd2_b, row(final_norm))
    return out.reshape(b, s, d)
```

```python
import functools

import jax
import jax.numpy as jnp
from jax import lax
from jax.experimental import pallas as pl
from jax.experimental.pallas import tpu as pltpu

D_MODEL = 1024
D_FF = 2816
CHUNK = 64
LEFT_CHUNKS = 8
POOL_WIDTH = 512
POOL_WINDOWS = (2, 4, 8, 16)
POOL_GROUP = 128
N_HEADS = 8
HEAD_DIM = 64
ATTN_WIDTH = 512
MAX_REL = 64
IN_WIDTH = POOL_WIDTH + 3 * ATTN_WIDTH
EPS = 1e-6

LANES = 128
TM = 512
FF_CHUNK = 256
COL_CHUNK = 256
Q_BLOCK = 256
ATTN_PREV = LEFT_CHUNKS * CHUNK
ATTN_TM = 1024
KEY_BLOCK = Q_BLOCK + ATTN_PREV
SCORE_LOOKAHEAD = 2
HALO = 16
POOL_PAD = 8
N_CAST = 8
N_TILES = 32
SIDE_ROWS = ((D_MODEL, N_TILES, 0.5),
             (POOL_WIDTH, N_TILES, 0.5),
             (ATTN_WIDTH, N_TILES, 0.5),
             (D_MODEL, N_TILES, 1.0),
             (D_FF, N_TILES // 2, 0.5),
             (POOL_WIDTH, N_TILES, 1.0))
MASK_VALUE = -2.0 ** 126
LOG2E = 1.4426950408889634
Q_SCALE = HEAD_DIM ** -0.5 * LOG2E
VMEM_LIMIT = 56 * 1024 * 1024

F32 = jnp.float32
BF16 = jnp.bfloat16


def _rmsnorm(x, g):
    ms = jnp.mean(x * x, axis=-1, keepdims=True)
    return x * lax.rsqrt(ms + EPS) * g


def _norm_split(x, g):
    rscale = lax.rsqrt(jnp.mean(x * x, axis=-1, keepdims=True) + EPS)
    return (x * g).astype(BF16), jnp.broadcast_to(rscale, (x.shape[0], LANES))


def _row_scale(y, rscale):
    return jnp.concatenate([y[:, lo:lo + LANES] * rscale for lo in range(0, y.shape[1], LANES)],
                           axis=1)


def _interleave_gate_up(dst_ref, rows, gate_blk, up_blk):
    for c in range(D_FF // FF_CHUNK):
        src = slice(c * FF_CHUNK, (c + 1) * FF_CHUNK)
        dst_ref[rows, 2 * c * FF_CHUNK:(2 * c + 1) * FF_CHUNK] = gate_blk[:, src].astype(BF16)
        dst_ref[rows, (2 * c + 1) * FF_CHUNK:(2 * c + 2) * FF_CHUNK] = up_blk[:, src].astype(BF16)


def _swiglu_act(hb_ref, rscale, wgu_ref, act_ref):
    half_rscale = 0.5 * rscale
    for c in range(D_FF // FF_CHUNK):
        gu = jnp.dot(hb_ref[...], wgu_ref[:, 2 * c * FF_CHUNK:(2 * c + 2) * FF_CHUNK],
                     preferred_element_type=F32)
        hg = _row_scale(gu[:, :FF_CHUNK], half_rscale)
        up = _row_scale(gu[:, FF_CHUNK:], rscale)
        act_ref[:, c * FF_CHUNK:(c + 1) * FF_CHUNK] = ((hg + hg * jnp.tanh(hg)) * up).astype(BF16)


def _ffn1_inproj_kernel(x_ref, n1_ref, wg_blk, wu_blk, wd_blk, nm_ref, win_blk,
                        wg2_blk, wu2_blk, *rest):
    n_side = len(SIDE_ROWS)
    side_in = rest[:n_side]
    x1_ref, up_ref, qkv_ref, wgu2_out = rest[n_side:n_side + 4]
    side_out = rest[n_side + 4:2 * n_side + 4]
    wgu_ref, wd_ref, win_ref, hb_ref, act_ref = rest[2 * n_side + 4:]
    i = pl.program_id(0)

    @pl.when(i < N_CAST)
    def _stage_weights():
        def rows_of(blk):
            return pl.ds(pl.multiple_of(i * blk.shape[0], blk.shape[0]), blk.shape[0])

        _interleave_gate_up(wgu_ref, rows_of(wg_blk), wg_blk, wu_blk)
        wd_ref[rows_of(wd_blk), :] = (wd_blk[...] * 0.5).astype(BF16)
        win_ref[rows_of(win_blk), :] = win_blk[...].astype(BF16)

    @pl.when(i >= N_CAST)
    def _token_tile():
        hb_ref[...], rscale = _norm_split(x_ref[...], n1_ref[...])
        _swiglu_act(hb_ref, rscale, wgu_ref, act_ref)
        for lo in range(0, D_MODEL, COL_CHUNK):
            cs = slice(lo, lo + COL_CHUNK)
            x1_ref[:, cs] = x_ref[:, cs] + jnp.dot(act_ref[...], wd_ref[:, cs],
                                                   preferred_element_type=F32)
        hb_ref[...], rscale = _norm_split(x1_ref[...], nm_ref[...])
        q_rscale = rscale * Q_SCALE
        for lo in range(0, IN_WIDTH, COL_CHUNK):
            y = jnp.dot(hb_ref[...], win_ref[:, lo:lo + COL_CHUNK], preferred_element_type=F32)
            if lo < POOL_WIDTH:
                up_ref[:, lo:lo + COL_CHUNK] = _row_scale(y, rscale)
            else:
                is_q = lo < POOL_WIDTH + ATTN_WIDTH
                qkv_ref[:, lo - POOL_WIDTH:lo - POOL_WIDTH + COL_CHUNK] = _row_scale(
                    y, q_rscale if is_q else rscale).astype(BF16)
        _interleave_gate_up(wgu2_out, slice(None), wg2_blk, wu2_blk)
        for src, dst, (_, _, scale) in zip(side_in, side_out, SIDE_ROWS):
            w = src[...] if scale == 1.0 else src[...] * scale
            dst[...] = w.astype(BF16)


def _band_kind(dchunk):
    if dchunk < 0 or dchunk > LEFT_CHUNKS:
        return 0
    if dchunk < LEFT_CHUNKS - 1:
        return 1
    return 2 if dchunk == LEFT_CHUNKS - 1 else 3


def _attn_kernel(tiles_per_seq, q_ref, kp_ref, kc_ref, vp_ref, vc_ref, pieces_ref,
                 o_ref, bias_ref, v_even_ref, v_odd_ref):
    i = pl.program_id(0)
    is_first = (i % tiles_per_seq) == 0
    n_first_blocks = ATTN_PREV // Q_BLOCK

    @pl.when(i == 0)
    def _build_bias_tables():
        low = lax.broadcasted_iota(jnp.int32, (CHUNK, LANES), 1) < CHUNK

        def per_head(h, carry):
            for variant in range(1 + n_first_blocks):
                n_before = 0 if variant == 0 else (ATTN_PREV - (variant - 1) * Q_BLOCK) // LANES
                for qc in range(Q_BLOCK // CHUNK):
                    for kp in range(KEY_BLOCK // LANES):
                        ka, kb = _band_kind(2 * kp - qc), _band_kind(2 * kp + 1 - qc)
                        if kp < n_before:
                            ka = kb = 0
                        blk = pieces_ref[h, ka]
                        if kb != ka:
                            blk = jnp.where(low, blk, pieces_ref[h, kb])
                        bias_ref[variant, h, qc * CHUNK:(qc + 1) * CHUNK,
                                 kp * LANES:(kp + 1) * LANES] = blk
            return carry

        lax.fori_loop(0, N_HEADS, per_head, 0)

    for src, row0 in ((vp_ref, 0), (vc_ref, ATTN_PREV)):
        v = src[...]
        even_lanes = (lax.broadcasted_iota(jnp.int32, v.shape, 1) % LANES) < HEAD_DIM
        ones = jnp.ones(v.shape, BF16)
        v_even_ref[row0:row0 + v.shape[0], :] = jnp.where(even_lanes, v, ones)
        v_odd_ref[row0:row0 + v.shape[0], :] = jnp.where(even_lanes, ones, v)

    low_half = lax.broadcasted_iota(jnp.int32, (Q_BLOCK, LANES), 1) < HEAD_DIM
    units = [(sb, p, par) for sb in range(ATTN_TM // Q_BLOCK) for p in range(N_HEADS // 2)
             for par in range(2)]

    def scores(unit):
        sb, p, par = unit
        r0, cs = sb * Q_BLOCK, slice(p * LANES, (p + 1) * LANES)
        qp = q_ref[r0:r0 + Q_BLOCK, cs]
        qm = jnp.where(low_half if par == 0 else jnp.logical_not(low_half), qp, jnp.zeros_like(qp))
        nt = (((1,), (1,)), ((), ()))
        parts = []
        if r0 < ATTN_PREV:
            parts.append(lax.dot_general(qm, kp_ref[r0:ATTN_PREV, cs], nt,
                                         preferred_element_type=F32))
        parts.append(lax.dot_general(
            qm, kc_ref[max(r0 - ATTN_PREV, 0):r0 + KEY_BLOCK - ATTN_PREV, cs], nt,
            preferred_element_type=F32))
        s = parts[0] if len(parts) == 1 else jnp.concatenate(parts, axis=1)
        variant = jnp.where(is_first, 1 + sb, 0) if sb < n_first_blocks else 0
        return s + bias_ref[variant, 2 * p + par]

    pending = [scores(u) for u in units[:SCORE_LOOKAHEAD]]
    outs = []
    for n, (sb, p, par) in enumerate(units):
        r0, cs = sb * Q_BLOCK, slice(p * LANES, (p + 1) * LANES)
        s = pending.pop(0)
        e = jnp.exp2(s - jnp.max(s, axis=-1, keepdims=True))
        if n + SCORE_LOOKAHEAD < len(units):
            pending.append(scores(units[n + SCORE_LOOKAHEAD]))
        v_ref = v_even_ref if par == 0 else v_odd_ref
        o = jnp.dot(e.astype(BF16), v_ref[r0:r0 + KEY_BLOCK, cs], preferred_element_type=F32)
        outs.append(o)
        if par == 1:
            num = jnp.where(low_half, outs[-2], outs[-1])
            den = pltpu.roll(jnp.where(low_half, outs[-1], outs[-2]), HEAD_DIM, axis=1)
            o_ref[r0:r0 + Q_BLOCK, cs] = (num / den).astype(BF16)


N_MIX_WEIGHTS = 12


def _mix_ffn2_pipelined(tiles_per_seq, n_tiles, x1_hbm, up_hbm, attn_hbm, *rest):
    weights, out_hbm = rest[:N_MIX_WEIGHTS], rest[N_MIX_WEIGHTS]
    scratch, step_ref = rest[N_MIX_WEIGHTS + 1:-1], rest[-1]
    step_ref[0] = 0

    def tile_body(x1_ref, up_ref, halo_ref, attn_ref, out_ref):
        _mix_ffn2_kernel(tiles_per_seq, step_ref[0], x1_ref, up_ref, halo_ref, attn_ref,
                         *weights, out_ref, *scratch)
        step_ref[0] = step_ref[0] + 1

    tile = lambda width: pl.BlockSpec((TM, width), lambda i: (i, 0))
    halo = pl.BlockSpec((HALO, POOL_WIDTH),
                        lambda i: (jnp.maximum(i * (TM // HALO) - 1, 0), 0))
    pltpu.emit_pipeline(
        tile_body, grid=(n_tiles,),
        in_specs=[tile(D_MODEL), tile(POOL_WIDTH), halo, tile(ATTN_WIDTH)],
        out_specs=[tile(D_MODEL)],
    )(x1_hbm, up_hbm, up_hbm, attn_hbm, out_hbm)


def _mix_ffn2_kernel(tiles_per_seq, i, x1_ref, up_ref, halo_ref, attn_ref, nm_ref, wg_ref,
                     bg_ref, poolw_ref, pscale_ref, wbp_ref, wba_ref, wout_ref,
                     n2_ref, wgu2_ref, wd_ref, nf_ref,
                     out_ref, hb_ref, act_ref, ext_ref, lvl_a_ref, lvl_b_ref, pm_ref, mg_ref, ga_ref):
    tile_in_seq = i % tiles_per_seq
    hb_ref[...], rscale = _norm_split(x1_ref[...], nm_ref[...])

    base = POOL_PAD + HALO
    keep_halo = jnp.where(tile_in_seq == 0, 0.0, 1.0).astype(F32)
    lvl_refs = (lvl_a_ref, lvl_b_ref)
    for ref in (ext_ref,) + lvl_refs:
        ref[0:POOL_PAD, :] = jnp.zeros((POOL_PAD, POOL_WIDTH), F32)
    ext_ref[POOL_PAD:base, :] = halo_ref[...] * keep_halo
    ext_ref[base:base + TM, :] = up_ref[...]
    pos_top = tile_in_seq * TM + lax.broadcasted_iota(jnp.int32, (HALO, 1), 0)

    def pool_group(g):
        w = POOL_WINDOWS[g]
        cs = slice(g * POOL_GROUP, (g + 1) * POOL_GROUP)
        src, shift, level = ext_ref, 1, 0
        while 2 * shift < w:
            lvl_ref = lvl_refs[level % 2]
            lvl_ref[POOL_PAD:base + TM, cs] = (src[POOL_PAD:base + TM, cs]
                                              + src[POOL_PAD - shift:base + TM - shift, cs])
            src, shift, level = lvl_ref, 2 * shift, level + 1
        sums = src[base:base + TM, cs] + src[base - shift:base + TM - shift, cs]
        inv_top = 1.0 / jnp.minimum(pos_top + 1, w).astype(F32)
        mean = jnp.concatenate([sums[:HALO] * inv_top, sums[HALO:] * (1.0 / w)], axis=0)
        mixed = (mean - ext_ref[base:base + TM, cs]).astype(BF16)
        y = jnp.dot(mixed, poolw_ref[g], preferred_element_type=F32)
        pm_ref[:, cs] = (y * pscale_ref[:, cs]).astype(BF16)

    col_chunks = [slice(lo, lo + COL_CHUNK) for lo in range(0, D_MODEL, COL_CHUNK)]

    def gate_tanh(cols):
        z_half = jnp.dot(hb_ref[...], wg_ref[:, cols], preferred_element_type=F32)
        return jnp.tanh(_row_scale(z_half, rscale) + 0.5 * bg_ref[:, cols])

    for cs in col_chunks:
        y_half = jnp.dot(attn_ref[...], wba_ref[:, cs], preferred_element_type=F32)
        t1 = gate_tanh(slice(D_MODEL + cs.start, D_MODEL + cs.stop))
        ga_ref[:, cs] = y_half + t1 * y_half
    g0 = [gate_tanh(cs) for cs in col_chunks[:2]]
    for g in (3, 0, 1, 2):
        pool_group(g)

    for c, cs in enumerate(col_chunks):
        if c == len(g0):
            g0.append(gate_tanh(cs))
        y_half = jnp.dot(pm_ref[...], wbp_ref[:, cs], preferred_element_type=F32)
        mg_ref[:, cs] = (y_half + g0[c] * y_half + ga_ref[:, cs]).astype(BF16)
    for cs in col_chunks:
        ga_ref[:, cs] = x1_ref[:, cs] + jnp.dot(mg_ref[...], wout_ref[:, cs],
                                                preferred_element_type=F32)

    hb_ref[...], rscale2 = _norm_split(ga_ref[...], n2_ref[...])
    _swiglu_act(hb_ref, rscale2, wgu2_ref, act_ref)
    hm = TM // 2
    for rows in (slice(0, hm), slice(hm, TM)):
        for cs in col_chunks:
            out_ref[rows, cs] = ga_ref[rows, cs] + jnp.dot(act_ref[rows, :], wd_ref[:, cs],
                                                           preferred_element_type=F32)
        out_ref[rows, :] = _rmsnorm(out_ref[rows, :], nf_ref[...])


def _resident(shape):
    return pl.BlockSpec(shape, lambda i: (0,) * len(shape), pipeline_mode=pl.Buffered(1))


def _toeplitz(w):
    n_heads, width = w.shape
    flat = jnp.tile(jnp.pad(w, ((0, 0), (0, 1))), (1, CHUNK))
    rows = flat[:, :CHUNK * width].reshape(n_heads, CHUNK, width)
    return rows[:, :, CHUNK - 1:]


def _bias_pieces(rel_bias):
    assert MAX_REL == CHUNK
    rb = rel_bias.astype(F32) * LOG2E
    n_heads = rb.shape[0]
    far = rb[:, 2 * MAX_REL]
    t_own = _toeplitz(rb[:, 1:2 * MAX_REL][:, ::-1])
    t_prev = _toeplitz(jnp.concatenate(
        [jnp.broadcast_to(far[:, None], (n_heads, CHUNK)), rb[:, MAX_REL + 1:2 * MAX_REL][:, ::-1]],
        axis=1))
    full = (n_heads, CHUNK, LANES)
    return jnp.stack([jnp.full(full, MASK_VALUE, F32),
                      jnp.broadcast_to(far[:, None, None], full),
                      jnp.concatenate([t_prev, t_prev], axis=2),
                      jnp.concatenate([t_own, t_own], axis=2)], axis=1)


def kernel(x, ffn1_norm, ffn1_w_gate, ffn1_w_up, ffn1_w_down, mix_norm, w_in, pool_w, pool_scale,
           rel_bias, w_branch_pool, w_branch_attn, w_gate, b_gate, w_out, ffn2_norm, ffn2_w_gate,
           ffn2_w_up, ffn2_w_down, final_norm):
    b, s, d = x.shape
    assert d == D_MODEL and s % TM == 0 and ffn1_norm.shape[0] == 1
    n_tok = b * s
    n_tiles = n_tok // TM
    assert n_tiles == N_TILES
    tiles_per_seq = s // TM
    xf = x.reshape(n_tok, d)

    row = lambda v: v.reshape(1, -1).astype(F32)
    cparams = pltpu.CompilerParams(dimension_semantics=("arbitrary",),
                                   vmem_limit_bytes=VMEM_LIMIT)
    tile = lambda width: pl.BlockSpec((TM, width), lambda i: (i, 0))

    tok = lambda i: jnp.maximum(i - N_CAST, 0)
    tile1 = lambda width: pl.BlockSpec((TM, width), lambda i: (tok(i), 0))
    staged = lambda w: pl.BlockSpec((w.shape[0] // N_CAST, w.shape[1]),
                                    lambda i: (jnp.minimum(i, N_CAST - 1), 0))
    own_w = [ffn1_w_gate[0], ffn1_w_up[0], ffn1_w_down[0], w_in[0]]
    side_w = [w_gate[0], w_branch_pool[0], w_branch_attn[0], w_out[0], ffn2_w_down[0],
              pool_w[0].reshape(POOL_WIDTH, POOL_GROUP)]
    gate_up_rows = D_MODEL // n_tiles
    gate_up_spec = lambda width: pl.BlockSpec((gate_up_rows, width), lambda i: (tok(i), 0))

    def side_specs():
        specs = []
        for w, (rows, steps, _) in zip(side_w, SIDE_ROWS):
            assert w.shape[0] == rows
            specs.append(pl.BlockSpec(
                (rows // steps, w.shape[1]),
                functools.partial(lambda steps, i: (jnp.minimum(tok(i), steps - 1), 0), steps)))
        return specs

    res = pl.pallas_call(
        _ffn1_inproj_kernel,
        grid=(N_CAST + n_tiles,),
        in_specs=[tile1(D_MODEL), _resident((1, D_MODEL)), staged(own_w[0]), staged(own_w[1]),
                  staged(own_w[2]), _resident((1, D_MODEL)), staged(own_w[3]),
                  gate_up_spec(D_FF), gate_up_spec(D_FF)] + side_specs(),
        out_specs=[tile1(D_MODEL), tile1(POOL_WIDTH), tile1(3 * ATTN_WIDTH),
                   gate_up_spec(2 * D_FF)] + side_specs(),
        out_shape=[jax.ShapeDtypeStruct((n_tok, D_MODEL), F32),
                   jax.ShapeDtypeStruct((n_tok, POOL_WIDTH), F32),
                   jax.ShapeDtypeStruct((n_tok, 3 * ATTN_WIDTH), BF16),
                   jax.ShapeDtypeStruct((D_MODEL, 2 * D_FF), BF16)]
                  + [jax.ShapeDtypeStruct(w.shape, BF16) for w in side_w],
        scratch_shapes=[pltpu.VMEM((D_MODEL, 2 * D_FF), BF16), pltpu.VMEM((D_FF, D_MODEL), BF16),
                        pltpu.VMEM((D_MODEL, IN_WIDTH), BF16),
                        pltpu.VMEM((TM, D_MODEL), BF16), pltpu.VMEM((TM, D_FF), BF16)],
        compiler_params=cparams,
        name="ffn1_inproj",
    )(xf, row(ffn1_norm), own_w[0], own_w[1], own_w[2], row(mix_norm), own_w[3],
      ffn2_w_gate[0], ffn2_w_up[0], *side_w)
    x1, up, qkv, wgu2_b = res[:4]
    wgate_b, wbp_b, wba_b, wout_b, wd2_b, poolw_b = res[4:]

    assert s % ATTN_TM == 0 and ATTN_TM % ATTN_PREV == 0
    prev = lambda i: jnp.maximum(i * (ATTN_TM // ATTN_PREV) - 1, 0)
    cur_spec = lambda col: pl.BlockSpec((ATTN_TM, ATTN_WIDTH), lambda i: (i, col))
    prev_spec = lambda col: pl.BlockSpec((ATTN_PREV, ATTN_WIDTH), lambda i: (prev(i), col))
    attn = pl.pallas_call(
        functools.partial(_attn_kernel, s // ATTN_TM),
        grid=(n_tok // ATTN_TM,),
        in_specs=[cur_spec(0), prev_spec(1), cur_spec(1), prev_spec(2), cur_spec(2),
                  _resident((N_HEADS, 4, CHUNK, LANES))],
        out_specs=cur_spec(0),
        out_shape=jax.ShapeDtypeStruct((n_tok, ATTN_WIDTH), BF16),
        scratch_shapes=[pltpu.VMEM((1 + ATTN_PREV // Q_BLOCK, N_HEADS, Q_BLOCK, KEY_BLOCK), F32),
                        pltpu.VMEM((ATTN_PREV + ATTN_TM, ATTN_WIDTH), BF16),
                        pltpu.VMEM((ATTN_PREV + ATTN_TM, ATTN_WIDTH), BF16)],
        compiler_params=cparams,
        name="chunk_attn",
    )(qkv, qkv, qkv, qkv, qkv, _bias_pieces(rel_bias[0]))

    in_hbm = pl.BlockSpec(memory_space=pl.ANY)
    in_vmem = pl.BlockSpec(memory_space=pltpu.VMEM)
    out = pl.pallas_call(
        functools.partial(_mix_ffn2_pipelined, tiles_per_seq, n_tiles),
        in_specs=[in_hbm] * 3 + [in_vmem] * N_MIX_WEIGHTS,
        out_specs=pl.BlockSpec(memory_space=pl.ANY),
        out_shape=jax.ShapeDtypeStruct((n_tok, D_MODEL), F32),
        scratch_shapes=[pltpu.VMEM((TM, D_MODEL), BF16), pltpu.VMEM((TM, D_FF), BF16),
                        pltpu.VMEM((POOL_PAD + HALO + TM, POOL_WIDTH), F32),
                        pltpu.VMEM((POOL_PAD + HALO + TM, POOL_WIDTH), F32),
                        pltpu.VMEM((POOL_PAD + HALO + TM, POOL_WIDTH), F32),
                        pltpu.VMEM((TM, POOL_WIDTH), BF16), pltpu.VMEM((TM, D_MODEL), BF16),
                        pltpu.VMEM((TM, D_MODEL), F32), pltpu.SMEM((1,), jnp.int32)],
        compiler_params=pltpu.CompilerParams(vmem_limit_bytes=VMEM_LIMIT),
        name="mix_ffn2",
    )(x1, up, attn, row(mix_norm), wgate_b, row(b_gate),
      poolw_b.reshape(len(POOL_WINDOWS), POOL_GROUP, POOL_GROUP), row(pool_scale), wbp_b, wba_b,
      wout_b, row(ffn2_norm), wgu2_b, wd2_b, row(final_norm))
    return out.reshape(b, s, d)
```

```python
import functools

import jax
import jax.numpy as jnp
from jax import lax
from jax.experimental import pallas as pl
from jax.experimental.pallas import tpu as pltpu

D_MODEL = 1024
D_FF = 2816
CHUNK = 64
LEFT_CHUNKS = 8
POOL_WIDTH = 512
POOL_WINDOWS = (2, 4, 8, 16)
POOL_GROUP = 128
N_HEADS = 8
HEAD_DIM = 64
ATTN_WIDTH = 512
MAX_REL = 64
IN_WIDTH = POOL_WIDTH + 3 * ATTN_WIDTH
EPS = 1e-6

LANES = 128
TM = 512
FF_CHUNK = 256
COL_CHUNK = 256
Q_BLOCK = 256
ATTN_PREV = LEFT_CHUNKS * CHUNK
ATTN_TM = 1024
KEY_BLOCK = Q_BLOCK + ATTN_PREV
SCORE_LOOKAHEAD = 2
HALO = 16
POOL_PAD = 8
N_CAST = 8
N_TILES = 32
SIDE_ROWS = ((D_MODEL, N_TILES, 0.5),
             (POOL_WIDTH, N_TILES, 0.5),
             (ATTN_WIDTH, N_TILES, 0.5),
             (D_MODEL, N_TILES, 1.0),
             (D_FF, N_TILES // 2, 0.5),
             (POOL_WIDTH, N_TILES, 1.0))
MASK_VALUE = -2.0 ** 126
LOG2E = 1.4426950408889634
Q_SCALE = HEAD_DIM ** -0.5 * LOG2E
VMEM_LIMIT = 56 * 1024 * 1024

F32 = jnp.float32
BF16 = jnp.bfloat16


def _rmsnorm(x, g):
    ms = jnp.mean(x * x, axis=-1, keepdims=True)
    return x * lax.rsqrt(ms + EPS) * g


def _norm_split(x, g):
    rscale = lax.rsqrt(jnp.mean(x * x, axis=-1, keepdims=True) + EPS)
    return (x * g).astype(BF16), jnp.broadcast_to(rscale, (x.shape[0], LANES))


def _row_scale(y, rscale):
    return jnp.concatenate([y[:, lo:lo + LANES] * rscale for lo in range(0, y.shape[1], LANES)],
                           axis=1)


def _interleave_gate_up(dst_ref, rows, gate_blk, up_blk):
    for c in range(D_FF // FF_CHUNK):
        src = slice(c * FF_CHUNK, (c + 1) * FF_CHUNK)
        dst_ref[rows, 2 * c * FF_CHUNK:(2 * c + 1) * FF_CHUNK] = gate_blk[:, src].astype(BF16)
        dst_ref[rows, (2 * c + 1) * FF_CHUNK:(2 * c + 2) * FF_CHUNK] = up_blk[:, src].astype(BF16)


def _swiglu_act(hb_ref, rscale, wgu_ref, act_ref):
    half_rscale = 0.5 * rscale
    for c in range(D_FF // FF_CHUNK):
        gu = jnp.dot(hb_ref[...], wgu_ref[:, 2 * c * FF_CHUNK:(2 * c + 2) * FF_CHUNK],
                     preferred_element_type=F32)
        hg = _row_scale(gu[:, :FF_CHUNK], half_rscale)
        up = _row_scale(gu[:, FF_CHUNK:], rscale)
        act_ref[:, c * FF_CHUNK:(c + 1) * FF_CHUNK] = ((hg + hg * jnp.tanh(hg)) * up).astype(BF16)


def _ffn1_inproj_kernel(x_ref, n1_ref, wg_blk, wu_blk, wd_blk, nm_ref, win_blk,
                        wg2_blk, wu2_blk, *rest):
    n_side = len(SIDE_ROWS)
    side_in = rest[:n_side]
    x1_ref, up_ref, qkv_ref, wgu2_out = rest[n_side:n_side + 4]
    side_out = rest[n_side + 4:2 * n_side + 4]
    wgu_ref, wd_ref, win_ref, hb_ref, act_ref = rest[2 * n_side + 4:]
    i = pl.program_id(0)

    @pl.when(i < N_CAST)
    def _stage_weights():
        def rows_of(blk):
            return pl.ds(pl.multiple_of(i * blk.shape[0], blk.shape[0]), blk.shape[0])

        _interleave_gate_up(wgu_ref, rows_of(wg_blk), wg_blk, wu_blk)
        wd_ref[rows_of(wd_blk), :] = (wd_blk[...] * 0.5).astype(BF16)
        win_ref[rows_of(win_blk), :] = win_blk[...].astype(BF16)

    @pl.when(i >= N_CAST)
    def _token_tile():
        hb_ref[...], rscale = _norm_split(x_ref[...], n1_ref[...])
        _swiglu_act(hb_ref, rscale, wgu_ref, act_ref)
        for lo in range(0, D_MODEL, COL_CHUNK):
            cs = slice(lo, lo + COL_CHUNK)
            x1_ref[:, cs] = x_ref[:, cs] + jnp.dot(act_ref[...], wd_ref[:, cs],
                                                   preferred_element_type=F32)
        hb_ref[...], rscale = _norm_split(x1_ref[...], nm_ref[...])
        q_rscale = rscale * Q_SCALE
        for lo in range(0, IN_WIDTH, COL_CHUNK):
            y = jnp.dot(hb_ref[...], win_ref[:, lo:lo + COL_CHUNK], preferred_element_type=F32)
            if lo < POOL_WIDTH:
                up_ref[:, lo:lo + COL_CHUNK] = _row_scale(y, rscale)
            else:
                is_q = lo < POOL_WIDTH + ATTN_WIDTH
                qkv_ref[:, lo - POOL_WIDTH:lo - POOL_WIDTH + COL_CHUNK] = _row_scale(
                    y, q_rscale if is_q else rscale).astype(BF16)
        _interleave_gate_up(wgu2_out, slice(None), wg2_blk, wu2_blk)
        for src, dst, (_, _, scale) in zip(side_in, side_out, SIDE_ROWS):
            w = src[...] if scale == 1.0 else src[...] * scale
            dst[...] = w.astype(BF16)


def _band_kind(dchunk):
    if dchunk < 0 or dchunk > LEFT_CHUNKS:
        return 0
    if dchunk < LEFT_CHUNKS - 1:
        return 1
    return 2 if dchunk == LEFT_CHUNKS - 1 else 3


def _attn_kernel(tiles_per_seq, q_ref, kp_ref, kc_ref, vp_ref, vc_ref, pieces_ref,
                 o_ref, bias_ref, v_even_ref, v_odd_ref):
    i = pl.program_id(0)
    is_first = (i % tiles_per_seq) == 0
    n_first_blocks = ATTN_PREV // Q_BLOCK

    @pl.when(i == 0)
    def _build_bias_tables():
        low = lax.broadcasted_iota(jnp.int32, (CHUNK, LANES), 1) < CHUNK

        def per_head(h, carry):
            for qc in range(Q_BLOCK // CHUNK):
                for kp in range(KEY_BLOCK // LANES):
                    ka, kb = _band_kind(2 * kp - qc), _band_kind(2 * kp + 1 - qc)
                    blk = pieces_ref[h, ka]
                    if kb != ka:
                        blk = jnp.where(low, blk, pieces_ref[h, kb])
                    bias_ref[h, qc * CHUNK:(qc + 1) * CHUNK, kp * LANES:(kp + 1) * LANES] = blk
            return carry

        lax.fori_loop(0, N_HEADS, per_head, 0)

    for src, row0 in ((vp_ref, 0), (vc_ref, ATTN_PREV)):
        v = src[...]
        even_lanes = (lax.broadcasted_iota(jnp.int32, v.shape, 1) % LANES) < HEAD_DIM
        ones = jnp.ones(v.shape, BF16)
        v_even_ref[row0:row0 + v.shape[0], :] = jnp.where(even_lanes, v, ones)
        v_odd_ref[row0:row0 + v.shape[0], :] = jnp.where(even_lanes, ones, v)

    low_half = lax.broadcasted_iota(jnp.int32, (Q_BLOCK, LANES), 1) < HEAD_DIM
    key_col = lax.broadcasted_iota(jnp.int32, (1, KEY_BLOCK), 1)
    units =[(sb, p, par) for sb in range(ATTN_TM // Q_BLOCK) for p in range(N_HEADS // 2)
             for par in range(2)]

    def scores(unit):
        sb, p, par = unit
        r0, cs = sb * Q_BLOCK, slice(p * LANES, (p + 1) * LANES)
        qp = q_ref[r0:r0 + Q_BLOCK, cs]
        qm = jnp.where(low_half if par == 0 else jnp.logical_not(low_half), qp, jnp.zeros_like(qp))
        nt = (((1,), (1,)), ((), ()))
        parts = []
        if r0 < ATTN_PREV:
            parts.append(lax.dot_general(qm, kp_ref[r0:ATTN_PREV, cs], nt,
                                         preferred_element_type=F32))
        parts.append(lax.dot_general(
            qm, kc_ref[max(r0 - ATTN_PREV, 0):r0 + KEY_BLOCK - ATTN_PREV, cs], nt,
            preferred_element_type=F32))
        s = parts[0] if len(parts) == 1 else jnp.concatenate(parts, axis=1)
        s = s + bias_ref[2 * p + par]
        if sb < n_first_blocks:
            n_before = jnp.where(is_first, ATTN_PREV - r0, 0)
            s = s + jnp.where(key_col < n_before, MASK_VALUE, 0.0).astype(F32)
        return s

    pending = [scores(u) for u in units[:SCORE_LOOKAHEAD]]
    outs = []
    for n, (sb, p, par) in enumerate(units):
        r0, cs = sb * Q_BLOCK, slice(p * LANES, (p + 1) * LANES)
        s = pending.pop(0)
        e = jnp.exp2(s - jnp.max(s, axis=-1, keepdims=True))
        if n + SCORE_LOOKAHEAD < len(units):
            pending.append(scores(units[n + SCORE_LOOKAHEAD]))
        v_ref = v_even_ref if par == 0 else v_odd_ref
        o = jnp.dot(e.astype(BF16), v_ref[r0:r0 + KEY_BLOCK, cs], preferred_element_type=F32)
        outs.append(o)
        if par == 1:
            num = jnp.where(low_half, outs[-2], outs[-1])
            den = pltpu.roll(jnp.where(low_half, outs[-1], outs[-2]), HEAD_DIM, axis=1)
            o_ref[r0:r0 + Q_BLOCK, cs] = (num / den).astype(BF16)


def _mix_ffn2_kernel(tiles_per_seq, x1_ref, up_ref, halo_ref, attn_ref, nm_ref, wg_ref,
                     bg_ref, poolw_ref, pscale_ref, wbp_ref, wba_ref, wout_ref,
                     n2_ref, wgu2_ref, wd_ref, nf_ref,
                     out_ref, hb_ref, act_ref, ext_ref, lvl_a_ref, lvl_b_ref, pm_ref, mg_ref, ga_ref):
    i = pl.program_id(0)
    tile_in_seq = i % tiles_per_seq
    hb_ref[...], rscale = _norm_split(x1_ref[...], nm_ref[...])

    base = POOL_PAD + HALO
    keep_halo = jnp.where(tile_in_seq == 0, 0.0, 1.0).astype(F32)
    lvl_refs = (lvl_a_ref, lvl_b_ref)
    for ref in (ext_ref,) + lvl_refs:
        ref[0:POOL_PAD, :] = jnp.zeros((POOL_PAD, POOL_WIDTH), F32)
    ext_ref[POOL_PAD:base, :] = halo_ref[...] * keep_halo
    ext_ref[base:base + TM, :] = up_ref[...]
    pos_top = tile_in_seq * TM + lax.broadcasted_iota(jnp.int32, (HALO, 1), 0)

    def pool_group(g):
        w = POOL_WINDOWS[g]
        cs = slice(g * POOL_GROUP, (g + 1) * POOL_GROUP)
        src, shift, level = ext_ref, 1, 0
        while 2 * shift < w:
            lvl_ref = lvl_refs[level % 2]
            lvl_ref[POOL_PAD:base + TM, cs] = (src[POOL_PAD:base + TM, cs]
                                              + src[POOL_PAD - shift:base + TM - shift, cs])
            src, shift, level = lvl_ref, 2 * shift, level + 1
        sums = src[base:base + TM, cs] + src[base - shift:base + TM - shift, cs]
        inv_top = 1.0 / jnp.minimum(pos_top + 1, w).astype(F32)
        mean = jnp.concatenate([sums[:HALO] * inv_top, sums[HALO:] * (1.0 / w)], axis=0)
        mixed = (mean - ext_ref[base:base + TM, cs]).astype(BF16)
        y = jnp.dot(mixed, poolw_ref[g], preferred_element_type=F32)
        pm_ref[:, cs] = (y * pscale_ref[:, cs]).astype(BF16)

    col_chunks = [slice(lo, lo + COL_CHUNK) for lo in range(0, D_MODEL, COL_CHUNK)]

    def gate_tanh(cols):
        z_half = jnp.dot(hb_ref[...], wg_ref[:, cols], preferred_element_type=F32)
        return jnp.tanh(_row_scale(z_half, rscale) + 0.5 * bg_ref[:, cols])

    for cs in col_chunks:
        y_half = jnp.dot(attn_ref[...], wba_ref[:, cs], preferred_element_type=F32)
        t1 = gate_tanh(slice(D_MODEL + cs.start, D_MODEL + cs.stop))
        ga_ref[:, cs] = y_half + t1 * y_half
    g0 = [gate_tanh(cs) for cs in col_chunks[:2]]
    for g in (3, 0, 1, 2):
        pool_group(g)

    for c, cs in enumerate(col_chunks):
        if c == len(g0):
            g0.append(gate_tanh(cs))
        y_half = jnp.dot(pm_ref[...], wbp_ref[:, cs], preferred_element_type=F32)
        mg_ref[:, cs] = (y_half + g0[c] * y_half + ga_ref[:, cs]).astype(BF16)
    for cs in col_chunks:
        ga_ref[:, cs] = x1_ref[:, cs] + jnp.dot(mg_ref[...], wout_ref[:, cs],
                                                preferred_element_type=F32)

    hb_ref[...], rscale2 = _norm_split(ga_ref[...], n2_ref[...])
    _swiglu_act(hb_ref, rscale2, wgu2_ref, act_ref)
    hm = TM // 2
    for rows in (slice(0, hm), slice(hm, TM)):
        for cs in col_chunks:
            out_ref[rows, cs] = ga_ref[rows, cs] + jnp.dot(act_ref[rows, :], wd_ref[:, cs],
                                                           preferred_element_type=F32)
        out_ref[rows, :] = _rmsnorm(out_ref[rows, :], nf_ref[...])


def _resident(shape):
    return pl.BlockSpec(shape, lambda i: (0,) * len(shape), pipeline_mode=pl.Buffered(1))


def _toeplitz(w):
    n_heads, width = w.shape
    flat = jnp.tile(jnp.pad(w, ((0, 0), (0, 1))), (1, CHUNK))
    rows = flat[:, :CHUNK * width].reshape(n_heads, CHUNK, width)
    return rows[:, :, CHUNK - 1:]


def _bias_pieces(rel_bias):
    assert MAX_REL == CHUNK
    rb = rel_bias.astype(F32) * LOG2E
    n_heads = rb.shape[0]
    far = rb[:, 2 * MAX_REL]
    t_own = _toeplitz(rb[:, 1:2 * MAX_REL][:, ::-1])
    t_prev = _toeplitz(jnp.concatenate(
        [jnp.broadcast_to(far[:, None], (n_heads, CHUNK)), rb[:, MAX_REL + 1:2 * MAX_REL][:, ::-1]],
        axis=1))
    full = (n_heads, CHUNK, LANES)
    return jnp.stack([jnp.full(full, MASK_VALUE, F32),
                      jnp.broadcast_to(far[:, None, None], full),
                      jnp.concatenate([t_prev, t_prev], axis=2),
                      jnp.concatenate([t_own, t_own], axis=2)], axis=1)


def kernel(x, ffn1_norm, ffn1_w_gate, ffn1_w_up, ffn1_w_down, mix_norm, w_in, pool_w, pool_scale,
           rel_bias, w_branch_pool, w_branch_attn, w_gate, b_gate, w_out, ffn2_norm, ffn2_w_gate,
           ffn2_w_up, ffn2_w_down, final_norm):
    b, s, d = x.shape
    assert d == D_MODEL and s % TM == 0 and ffn1_norm.shape[0] == 1
    n_tok = b * s
    n_tiles = n_tok // TM
    assert n_tiles == N_TILES
    tiles_per_seq = s // TM
    xf = x.reshape(n_tok, d)

    row = lambda v: v.reshape(1, -1).astype(F32)
    cparams = pltpu.CompilerParams(dimension_semantics=("arbitrary",),
                                   vmem_limit_bytes=VMEM_LIMIT)
    tile = lambda width: pl.BlockSpec((TM, width), lambda i: (i, 0))

    tok = lambda i: jnp.maximum(i - N_CAST, 0)
    tile1 = lambda width: pl.BlockSpec((TM, width), lambda i: (tok(i), 0))
    staged = lambda w: pl.BlockSpec((w.shape[0] // N_CAST, w.shape[1]),
                                    lambda i: (jnp.minimum(i, N_CAST - 1), 0))
    own_w = [ffn1_w_gate[0], ffn1_w_up[0], ffn1_w_down[0], w_in[0]]
    side_w = [w_gate[0], w_branch_pool[0], w_branch_attn[0], w_out[0], ffn2_w_down[0],
              pool_w[0].reshape(POOL_WIDTH, POOL_GROUP)]
    gate_up_rows = D_MODEL // n_tiles
    gate_up_spec = lambda width: pl.BlockSpec((gate_up_rows, width), lambda i: (tok(i), 0))

    def side_specs():
        specs = []
        for w, (rows, steps, _) in zip(side_w, SIDE_ROWS):
            assert w.shape[0] == rows
            specs.append(pl.BlockSpec(
                (rows // steps, w.shape[1]),
                functools.partial(lambda steps, i: (jnp.minimum(tok(i), steps - 1), 0), steps)))
        return specs

    res = pl.pallas_call(
        _ffn1_inproj_kernel,
        grid=(N_CAST + n_tiles,),
        in_specs=[tile1(D_MODEL), _resident((1, D_MODEL)), staged(own_w[0]), staged(own_w[1]),
                  staged(own_w[2]), _resident((1, D_MODEL)), staged(own_w[3]),
                  gate_up_spec(D_FF), gate_up_spec(D_FF)] + side_specs(),
        out_specs=[tile1(D_MODEL), tile1(POOL_WIDTH), tile1(3 * ATTN_WIDTH),
                   gate_up_spec(2 * D_FF)] + side_specs(),
        out_shape=[jax.ShapeDtypeStruct((n_tok, D_MODEL), F32),
                   jax.ShapeDtypeStruct((n_tok, POOL_WIDTH), F32),
                   jax.ShapeDtypeStruct((n_tok, 3 * ATTN_WIDTH), BF16),
                   jax.ShapeDtypeStruct((D_MODEL, 2 * D_FF), BF16)]
                  + [jax.ShapeDtypeStruct(w.shape, BF16) for w in side_w],
        scratch_shapes=[pltpu.VMEM((D_MODEL, 2 * D_FF), BF16), pltpu.VMEM((D_FF, D_MODEL), BF16),
                        pltpu.VMEM((D_MODEL, IN_WIDTH), BF16),
                        pltpu.VMEM((TM, D_MODEL), BF16), pltpu.VMEM((TM, D_FF), BF16)],
        compiler_params=cparams,
        name="ffn1_inproj",
    )(xf, row(ffn1_norm), own_w[0], own_w[1], own_w[2], row(mix_norm), own_w[3],
      ffn2_w_gate[0], ffn2_w_up[0], *side_w)
    x1, up, qkv, wgu2_b = res[:4]
    wgate_b, wbp_b, wba_b, wout_b, wd2_b, poolw_b = res[4:]

    assert s % ATTN_TM == 0 and ATTN_TM % ATTN_PREV == 0
    prev = lambda i: jnp.maximum(i * (ATTN_TM // ATTN_PREV) - 1, 0)
    cur_spec = lambda col: pl.BlockSpec((ATTN_TM, ATTN_WIDTH), lambda i: (i, col))
    prev_spec = lambda col: pl.BlockSpec((ATTN_PREV, ATTN_WIDTH), lambda i: (prev(i), col))
    attn = pl.pallas_call(
        functools.partial(_attn_kernel, s // ATTN_TM),
        grid=(n_tok // ATTN_TM,),
        in_specs=[cur_spec(0), prev_spec(1), cur_spec(1), prev_spec(2), cur_spec(2),
                  _resident((N_HEADS, 4, CHUNK, LANES))],
        out_specs=cur_spec(0),
        out_shape=jax.ShapeDtypeStruct((n_tok, ATTN_WIDTH), BF16),
        scratch_shapes=[pltpu.VMEM((N_HEADS, Q_BLOCK, KEY_BLOCK), F32),
                        pltpu.VMEM((ATTN_PREV + ATTN_TM, ATTN_WIDTH), BF16),
                        pltpu.VMEM((ATTN_PREV + ATTN_TM, ATTN_WIDTH), BF16)],
        compiler_params=cparams,
        name="chunk_attn",
    )(qkv, qkv, qkv, qkv, qkv, _bias_pieces(rel_bias[0]))

    halo_blocks = TM // HALO
    out = pl.pallas_call(
        functools.partial(_mix_ffn2_kernel, tiles_per_seq),
        grid=(n_tiles,),
        in_specs=[tile(D_MODEL), tile(POOL_WIDTH),
                  pl.BlockSpec((HALO, POOL_WIDTH),
                               lambda i: (jnp.maximum(i * halo_blocks - 1, 0), 0)),
                  tile(ATTN_WIDTH), _resident((1, D_MODEL)),
                  _resident((D_MODEL, 2 * D_MODEL)), _resident((1, 2 * D_MODEL)),
                  _resident((len(POOL_WINDOWS), POOL_GROUP, POOL_GROUP)),
                  _resident((1, POOL_WIDTH)), _resident((POOL_WIDTH, D_MODEL)),
                  _resident((ATTN_WIDTH, D_MODEL)), _resident((D_MODEL, D_MODEL)),
                  _resident((1, D_MODEL)), _resident((D_MODEL, 2 * D_FF)),
                  _resident((D_FF, D_MODEL)), _resident((1, D_MODEL))],
        out_specs=tile(D_MODEL),
        out_shape=jax.ShapeDtypeStruct((n_tok, D_MODEL), F32),
        scratch_shapes=[pltpu.VMEM((TM, D_MODEL), BF16), pltpu.VMEM((TM, D_FF), BF16),
                        pltpu.VMEM((POOL_PAD + HALO + TM, POOL_WIDTH), F32),
                        pltpu.VMEM((POOL_PAD + HALO + TM, POOL_WIDTH), F32),
                        pltpu.VMEM((POOL_PAD + HALO + TM, POOL_WIDTH), F32),
                        pltpu.VMEM((TM, POOL_WIDTH), BF16), pltpu.VMEM((TM, D_MODEL), BF16),
                        pltpu.VMEM((TM, D_MODEL), F32)],
        compiler_params=cparams,
        name="mix_ffn2",
    )(x1, up, up, attn, row(mix_norm), wgate_b, row(b_gate),
      poolw_b.reshape(len(POOL_WINDOWS), POOL_GROUP, POOL_GROUP), row(pool_scale), wbp_b, wba_b,
      wout_b, row(ffn2_norm), wgu2_b, wd2_b, row(final_norm))
    return out.reshape(b, s, d)
```

```python
import functools

import jax
import jax.numpy as jnp
from jax import lax
from jax.experimental import pallas as pl
from jax.experimental.pallas import tpu as pltpu

D_MODEL = 1024
D_FF = 2816
CHUNK = 64
LEFT_CHUNKS = 8
POOL_WIDTH = 512
POOL_WINDOWS = (2, 4, 8, 16)
POOL_GROUP = 128
N_HEADS = 8
HEAD_DIM = 64
ATTN_WIDTH = 512
MAX_REL = 64
IN_WIDTH = POOL_WIDTH + 3 * ATTN_WIDTH
EPS = 1e-6

LANES = 128
TM = 512
FF_CHUNK = 256
COL_CHUNK = 256
Q_BLOCK = 256
ATTN_PREV = LEFT_CHUNKS * CHUNK
ATTN_TM = 1024
KEY_BLOCK = Q_BLOCK + ATTN_PREV
SCORE_LOOKAHEAD = 2
HALO = 16
POOL_PAD = 8
N_CAST = 8
N_TILES = 32
SIDE_ROWS = ((D_MODEL, N_TILES, 0.5),
             (POOL_WIDTH, N_TILES, 0.5),
             (ATTN_WIDTH, N_TILES, 0.5),
             (D_MODEL, N_TILES, 1.0),
             (D_FF, N_TILES // 2, 0.5),
             (POOL_WIDTH, N_TILES, 1.0))
MASK_VALUE = -2.0 ** 126
LOG2E = 1.4426950408889634
Q_SCALE = HEAD_DIM ** -0.5 * LOG2E
VMEM_LIMIT = 56 * 1024 * 1024

F32 = jnp.float32
BF16 = jnp.bfloat16


def _rmsnorm(x, g):
    ms = jnp.mean(x * x, axis=-1, keepdims=True)
    return x * lax.rsqrt(ms + EPS) * g


def _norm_split(x, g):
    rscale = lax.rsqrt(jnp.mean(x * x, axis=-1, keepdims=True) + EPS)
    return (x * g).astype(BF16), jnp.broadcast_to(rscale, (x.shape[0], LANES))


def _row_scale(y, rscale):
    return jnp.concatenate([y[:, lo:lo + LANES] * rscale for lo in range(0, y.shape[1], LANES)],
                           axis=1)


def _interleave_gate_up(dst_ref, rows, gate_blk, up_blk):
    for c in range(D_FF // FF_CHUNK):
        src = slice(c * FF_CHUNK, (c + 1) * FF_CHUNK)
        dst_ref[rows, 2 * c * FF_CHUNK:(2 * c + 1) * FF_CHUNK] = gate_blk[:, src].astype(BF16)
        dst_ref[rows, (2 * c + 1) * FF_CHUNK:(2 * c + 2) * FF_CHUNK] = up_blk[:, src].astype(BF16)


def _swiglu_act(hb_ref, rscale, wgu_ref, act_ref):
    half_rscale = 0.5 * rscale
    for c in range(D_FF // FF_CHUNK):
        gu = jnp.dot(hb_ref[...], wgu_ref[:, 2 * c * FF_CHUNK:(2 * c + 2) * FF_CHUNK],
                     preferred_element_type=F32)
        hg = _row_scale(gu[:, :FF_CHUNK], half_rscale)
        up = _row_scale(gu[:, FF_CHUNK:], rscale)
        act_ref[:, c * FF_CHUNK:(c + 1) * FF_CHUNK] = ((hg + hg * jnp.tanh(hg)) * up).astype(BF16)


def _ffn1_inproj_kernel(x_ref, n1_ref, wg_blk, wu_blk, wd_blk, nm_ref, win_blk,
                        wg2_blk, wu2_blk, *rest):
    n_side = len(SIDE_ROWS)
    side_in = rest[:n_side]
    x1_ref, up_ref, qkv_ref, wgu2_out = rest[n_side:n_side + 4]
    side_out = rest[n_side + 4:2 * n_side + 4]
    wgu_ref, wd_ref, win_ref, hb_ref, act_ref = rest[2 * n_side + 4:]
    i = pl.program_id(0)

    @pl.when(i < N_CAST)
    def _stage_weights():
        def rows_of(blk):
            return pl.ds(pl.multiple_of(i * blk.shape[0], blk.shape[0]), blk.shape[0])

        _interleave_gate_up(wgu_ref, rows_of(wg_blk), wg_blk, wu_blk)
        wd_ref[rows_of(wd_blk), :] = (wd_blk[...] * 0.5).astype(BF16)
        win_ref[rows_of(win_blk), :] = win_blk[...].astype(BF16)

    @pl.when(i >= N_CAST)
    def _token_tile():
        hb_ref[...], rscale = _norm_split(x_ref[...], n1_ref[...])
        _swiglu_act(hb_ref, rscale, wgu_ref, act_ref)
        for lo in range(0, D_MODEL, COL_CHUNK):
            cs = slice(lo, lo + COL_CHUNK)
            x1_ref[:, cs] = x_ref[:, cs] + jnp.dot(act_ref[...], wd_ref[:, cs],
                                                   preferred_element_type=F32)
        hb_ref[...], rscale = _norm_split(x1_ref[...], nm_ref[...])
        q_rscale = rscale * Q_SCALE
        for lo in range(0, IN_WIDTH, COL_CHUNK):
            y = jnp.dot(hb_ref[...], win_ref[:, lo:lo + COL_CHUNK], preferred_element_type=F32)
            if lo < POOL_WIDTH:
                up_ref[:, lo:lo + COL_CHUNK] = _row_scale(y, rscale)
            else:
                is_q = lo < POOL_WIDTH + ATTN_WIDTH
                qkv_ref[:, lo - POOL_WIDTH:lo - POOL_WIDTH + COL_CHUNK] = _row_scale(
                    y, q_rscale if is_q else rscale).astype(BF16)
        _interleave_gate_up(wgu2_out, slice(None), wg2_blk, wu2_blk)
        for src, dst, (_, _, scale) in zip(side_in, side_out, SIDE_ROWS):
            w = src[...] if scale == 1.0 else src[...] * scale
            dst[...] = w.astype(BF16)


def _band_kind(dchunk):
    if dchunk < 0 or dchunk > LEFT_CHUNKS:
        return 0
    if dchunk < LEFT_CHUNKS - 1:
        return 1
    return 2 if dchunk == LEFT_CHUNKS - 1 else 3


def _attn_kernel(tiles_per_seq, q_ref, kc_ref, vc_ref, pieces_ref,
                 o_ref, bias_ref, v_even_ref, v_odd_ref, kp_ref):
    i = pl.program_id(0)
    is_first = (i % tiles_per_seq) == 0
    n_first_blocks = ATTN_PREV // Q_BLOCK

    @pl.when(i == 0)
    def _build_bias_tables():
        kp_ref[...] = jnp.zeros(kp_ref.shape, BF16)
        v_even_ref[0:ATTN_PREV, :] = jnp.zeros((ATTN_PREV, ATTN_WIDTH), BF16)
        v_odd_ref[0:ATTN_PREV, :] = jnp.zeros((ATTN_PREV, ATTN_WIDTH), BF16)
        low = lax.broadcasted_iota(jnp.int32, (CHUNK, LANES), 1) < CHUNK

        def per_head(h, carry):
            for variant in range(1 + n_first_blocks):
                n_before = 0 if variant == 0 else (ATTN_PREV - (variant - 1) * Q_BLOCK) // LANES
                for qc in range(Q_BLOCK // CHUNK):
                    for kp in range(KEY_BLOCK // LANES):
                        ka, kb = _band_kind(2 * kp - qc), _band_kind(2 * kp + 1 - qc)
                        if kp < n_before:
                            ka = kb = 0
                        blk = pieces_ref[h, ka]
                        if kb != ka:
                            blk = jnp.where(low, blk, pieces_ref[h, kb])
                        bias_ref[variant, h, qc * CHUNK:(qc + 1) * CHUNK,
                                 kp * LANES:(kp + 1) * LANES] = blk
            return carry

        lax.fori_loop(0, N_HEADS, per_head, 0)

    for src, row0 in ((vc_ref, ATTN_PREV),):
        v = src[...]
        even_lanes = (lax.broadcasted_iota(jnp.int32, v.shape, 1) % LANES) < HEAD_DIM
        ones = jnp.ones(v.shape, BF16)
        v_even_ref[row0:row0 + v.shape[0], :] = jnp.where(even_lanes, v, ones)
        v_odd_ref[row0:row0 + v.shape[0], :] = jnp.where(even_lanes, ones, v)

    low_half = lax.broadcasted_iota(jnp.int32, (Q_BLOCK, LANES), 1) < HEAD_DIM
    units = [(sb, p, par) for sb in range(ATTN_TM // Q_BLOCK) for p in range(N_HEADS // 2)
             for par in range(2)]

    def scores(unit):
        sb, p, par = unit
        r0, cs = sb * Q_BLOCK, slice(p * LANES, (p + 1) * LANES)
        qp = q_ref[r0:r0 + Q_BLOCK, cs]
        qm = jnp.where(low_half if par == 0 else jnp.logical_not(low_half), qp, jnp.zeros_like(qp))
        nt = (((1,), (1,)), ((), ()))
        parts = []
        if r0 < ATTN_PREV:
            parts.append(lax.dot_general(qm, kp_ref[r0:ATTN_PREV, cs], nt,
                                         preferred_element_type=F32))
        parts.append(lax.dot_general(
            qm, kc_ref[max(r0 - ATTN_PREV, 0):r0 + KEY_BLOCK - ATTN_PREV, cs], nt,
            preferred_element_type=F32))
        s = parts[0] if len(parts) == 1 else jnp.concatenate(parts, axis=1)
        variant = jnp.where(is_first, 1 + sb, 0) if sb < n_first_blocks else 0
        return s + bias_ref[variant, 2 * p + par]

    pending = [scores(u) for u in units[:SCORE_LOOKAHEAD]]
    outs = []
    for n, (sb, p, par) in enumerate(units):
        r0, cs = sb * Q_BLOCK, slice(p * LANES, (p + 1) * LANES)
        s = pending.pop(0)
        e = jnp.exp2(s - jnp.max(s, axis=-1, keepdims=True))
        if n + SCORE_LOOKAHEAD < len(units):
            pending.append(scores(units[n + SCORE_LOOKAHEAD]))
        v_ref = v_even_ref if par == 0 else v_odd_ref
        o = jnp.dot(e.astype(BF16), v_ref[r0:r0 + KEY_BLOCK, cs], preferred_element_type=F32)
        outs.append(o)
        if par == 1:
            num = jnp.where(low_half, outs[-2], outs[-1])
            den = pltpu.roll(jnp.where(low_half, outs[-1], outs[-2]), HEAD_DIM, axis=1)
            o_ref[r0:r0 + Q_BLOCK, cs] = (num / den).astype(BF16)

    kp_ref[...] = kc_ref[ATTN_TM - ATTN_PREV:ATTN_TM, :]
    v_even_ref[0:ATTN_PREV, :] = v_even_ref[ATTN_TM:ATTN_TM + ATTN_PREV, :]
    v_odd_ref[0:ATTN_PREV, :] = v_odd_ref[ATTN_TM:ATTN_TM + ATTN_PREV, :]


def _mix_ffn2_kernel(tiles_per_seq, x1_ref, up_ref, halo_ref, attn_ref, nm_ref, wg_ref,
                     bg_ref, poolw_ref, pscale_ref, wbp_ref, wba_ref, wout_ref,
                     n2_ref, wgu2_ref, wd_ref, nf_ref,
                     out_ref, hb_ref, act_ref, ext_ref, lvl_a_ref, lvl_b_ref, pm_ref, mg_ref, ga_ref):
    i = pl.program_id(0)
    tile_in_seq = i % tiles_per_seq
    hb_ref[...], rscale = _norm_split(x1_ref[...], nm_ref[...])

    base = POOL_PAD + HALO
    keep_halo = jnp.where(tile_in_seq == 0, 0.0, 1.0).astype(F32)
    lvl_refs = (lvl_a_ref, lvl_b_ref)
    for ref in (ext_ref,) + lvl_refs:
        ref[0:POOL_PAD, :] = jnp.zeros((POOL_PAD, POOL_WIDTH), F32)
    ext_ref[POOL_PAD:base, :] = halo_ref[...] * keep_halo
    ext_ref[base:base + TM, :] = up_ref[...]
    pos_top = tile_in_seq * TM + lax.broadcasted_iota(jnp.int32, (HALO, 1), 0)

    def pool_group(g):
        w = POOL_WINDOWS[g]
        cs = slice(g * POOL_GROUP, (g + 1) * POOL_GROUP)
        src, shift, level = ext_ref, 1, 0
        while 2 * shift < w:
            lvl_ref = lvl_refs[level % 2]
            lvl_ref[POOL_PAD:base + TM, cs] = (src[POOL_PAD:base + TM, cs]
                                              + src[POOL_PAD - shift:base + TM - shift, cs])
            src, shift, level = lvl_ref, 2 * shift, level + 1
        sums = src[base:base + TM, cs] + src[base - shift:base + TM - shift, cs]
        inv_top = 1.0 / jnp.minimum(pos_top + 1, w).astype(F32)
        mean = jnp.concatenate([sums[:HALO] * inv_top, sums[HALO:] * (1.0 / w)], axis=0)
        mixed = (mean - ext_ref[base:base + TM, cs]).astype(BF16)
        y = jnp.dot(mixed, poolw_ref[g], preferred_element_type=F32)
        pm_ref[:, cs] = (y * pscale_ref[:, cs]).astype(BF16)

    col_chunks = [slice(lo, lo + COL_CHUNK) for lo in range(0, D_MODEL, COL_CHUNK)]

    def gate_tanh(cols):
        z_half = jnp.dot(hb_ref[...], wg_ref[:, cols], preferred_element_type=F32)
        return jnp.tanh(_row_scale(z_half, rscale) + 0.5 * bg_ref[:, cols])

    for cs in col_chunks:
        y_half = jnp.dot(attn_ref[...], wba_ref[:, cs], preferred_element_type=F32)
        t1 = gate_tanh(slice(D_MODEL + cs.start, D_MODEL + cs.stop))
        ga_ref[:, cs] = y_half + t1 * y_half
    g0 = [gate_tanh(cs) for cs in col_chunks[:2]]
    for g in (3, 0, 1, 2):
        pool_group(g)

    for c, cs in enumerate(col_chunks):
        if c == len(g0):
            g0.append(gate_tanh(cs))
        y_half = jnp.dot(pm_ref[...], wbp_ref[:, cs], preferred_element_type=F32)
        mg_ref[:, cs] = (y_half + g0[c] * y_half + ga_ref[:, cs]).astype(BF16)
    for cs in col_chunks:
        ga_ref[:, cs] = x1_ref[:, cs] + jnp.dot(mg_ref[...], wout_ref[:, cs],
                                                preferred_element_type=F32)

    hb_ref[...], rscale2 = _norm_split(ga_ref[...], n2_ref[...])
    _swiglu_act(hb_ref, rscale2, wgu2_ref, act_ref)
    hm = TM // 2
    for rows in (slice(0, hm), slice(hm, TM)):
        for cs in col_chunks:
            out_ref[rows, cs] = ga_ref[rows, cs] + jnp.dot(act_ref[rows, :], wd_ref[:, cs],
                                                           preferred_element_type=F32)
        out_ref[rows, :] = _rmsnorm(out_ref[rows, :], nf_ref[...])


def _resident(shape):
    return pl.BlockSpec(shape, lambda i: (0,) * len(shape), pipeline_mode=pl.Buffered(1))


def _toeplitz(w):
    n_heads, width = w.shape
    flat = jnp.tile(jnp.pad(w, ((0, 0), (0, 1))), (1, CHUNK))
    rows = flat[:, :CHUNK * width].reshape(n_heads, CHUNK, width)
    return rows[:, :, CHUNK - 1:]


def _bias_pieces(rel_bias):
    assert MAX_REL == CHUNK
    rb = rel_bias.astype(F32) * LOG2E
    n_heads = rb.shape[0]
    far = rb[:, 2 * MAX_REL]
    t_own = _toeplitz(rb[:, 1:2 * MAX_REL][:, ::-1])
    t_prev = _toeplitz(jnp.concatenate(
        [jnp.broadcast_to(far[:, None], (n_heads, CHUNK)), rb[:, MAX_REL + 1:2 * MAX_REL][:, ::-1]],
        axis=1))
    full = (n_heads, CHUNK, LANES)
    return jnp.stack([jnp.full(full, MASK_VALUE, F32),
                      jnp.broadcast_to(far[:, None, None], full),
                      jnp.concatenate([t_prev, t_prev], axis=2),
                      jnp.concatenate([t_own, t_own], axis=2)], axis=1)


def kernel(x, ffn1_norm, ffn1_w_gate, ffn1_w_up, ffn1_w_down, mix_norm, w_in, pool_w, pool_scale,
           rel_bias, w_branch_pool, w_branch_attn, w_gate, b_gate, w_out, ffn2_norm, ffn2_w_gate,
           ffn2_w_up, ffn2_w_down, final_norm):
    b, s, d = x.shape
    assert d == D_MODEL and s % TM == 0 and ffn1_norm.shape[0] == 1
    n_tok = b * s
    n_tiles = n_tok // TM
    assert n_tiles == N_TILES
    tiles_per_seq = s // TM
    xf = x.reshape(n_tok, d)

    row = lambda v: v.reshape(1, -1).astype(F32)
    cparams = pltpu.CompilerParams(dimension_semantics=("arbitrary",),
                                   vmem_limit_bytes=VMEM_LIMIT)
    tile = lambda width: pl.BlockSpec((TM, width), lambda i: (i, 0))

    tok = lambda i: jnp.maximum(i - N_CAST, 0)
    tile1 = lambda width: pl.BlockSpec((TM, width), lambda i: (tok(i), 0))
    staged = lambda w: pl.BlockSpec((w.shape[0] // N_CAST, w.shape[1]),
                                    lambda i: (jnp.minimum(i, N_CAST - 1), 0))
    own_w = [ffn1_w_gate[0], ffn1_w_up[0], ffn1_w_down[0], w_in[0]]
    side_w = [w_gate[0], w_branch_pool[0], w_branch_attn[0], w_out[0], ffn2_w_down[0],
              pool_w[0].reshape(POOL_WIDTH, POOL_GROUP)]
    gate_up_rows = D_MODEL // n_tiles
    gate_up_spec = lambda width: pl.BlockSpec((gate_up_rows, width), lambda i: (tok(i), 0))

    def side_specs():
        specs = []
        for w, (rows, steps, _) in zip(side_w, SIDE_ROWS):
            assert w.shape[0] == rows
            specs.append(pl.BlockSpec(
                (rows // steps, w.shape[1]),
                functools.partial(lambda steps, i: (jnp.minimum(tok(i), steps - 1), 0), steps)))
        return specs

    res = pl.pallas_call(
        _ffn1_inproj_kernel,
        grid=(N_CAST + n_tiles,),
        in_specs=[tile1(D_MODEL), _resident((1, D_MODEL)), staged(own_w[0]), staged(own_w[1]),
                  staged(own_w[2]), _resident((1, D_MODEL)), staged(own_w[3]),
                  gate_up_spec(D_FF), gate_up_spec(D_FF)] + side_specs(),
        out_specs=[tile1(D_MODEL), tile1(POOL_WIDTH), tile1(3 * ATTN_WIDTH),
                   gate_up_spec(2 * D_FF)] + side_specs(),
        out_shape=[jax.ShapeDtypeStruct((n_tok, D_MODEL), F32),
                   jax.ShapeDtypeStruct((n_tok, POOL_WIDTH), F32),
                   jax.ShapeDtypeStruct((n_tok, 3 * ATTN_WIDTH), BF16),
                   jax.ShapeDtypeStruct((D_MODEL, 2 * D_FF), BF16)]
                  + [jax.ShapeDtypeStruct(w.shape, BF16) for w in side_w],
        scratch_shapes=[pltpu.VMEM((D_MODEL, 2 * D_FF), BF16), pltpu.VMEM((D_FF, D_MODEL), BF16),
                        pltpu.VMEM((D_MODEL, IN_WIDTH), BF16),
                        pltpu.VMEM((TM, D_MODEL), BF16), pltpu.VMEM((TM, D_FF), BF16)],
        compiler_params=cparams,
        name="ffn1_inproj",
    )(xf, row(ffn1_norm), own_w[0], own_w[1], own_w[2], row(mix_norm), own_w[3],
      ffn2_w_gate[0], ffn2_w_up[0], *side_w)
    x1, up, qkv, wgu2_b = res[:4]
    wgate_b, wbp_b, wba_b, wout_b, wd2_b, poolw_b = res[4:]

    assert s % ATTN_TM == 0 and ATTN_TM % ATTN_PREV == 0
    prev = lambda i: jnp.maximum(i * (ATTN_TM // ATTN_PREV) - 1, 0)
    cur_spec = lambda col: pl.BlockSpec((ATTN_TM, ATTN_WIDTH), lambda i: (i, col))
    prev_spec = lambda col: pl.BlockSpec((ATTN_PREV, ATTN_WIDTH), lambda i: (prev(i), col))
    attn = pl.pallas_call(
        functools.partial(_attn_kernel, s // ATTN_TM),
        grid=(n_tok // ATTN_TM,),
        in_specs=[cur_spec(0), cur_spec(1), cur_spec(2),
                  _resident((N_HEADS, 4, CHUNK, LANES))],
        out_specs=cur_spec(0),
        out_shape=jax.ShapeDtypeStruct((n_tok, ATTN_WIDTH), BF16),
        scratch_shapes=[pltpu.VMEM((1 + ATTN_PREV // Q_BLOCK, N_HEADS, Q_BLOCK, KEY_BLOCK), F32),
                        pltpu.VMEM((ATTN_PREV + ATTN_TM, ATTN_WIDTH), BF16),
                        pltpu.VMEM((ATTN_PREV + ATTN_TM, ATTN_WIDTH), BF16),
                        pltpu.VMEM((ATTN_PREV, ATTN_WIDTH), BF16)],
        compiler_params=cparams,
        name="chunk_attn",
    )(qkv, qkv, qkv, _bias_pieces(rel_bias[0]))

    halo_blocks = TM // HALO
    out = pl.pallas_call(
        functools.partial(_mix_ffn2_kernel, tiles_per_seq),
        grid=(n_tiles,),
        in_specs=[tile(D_MODEL), tile(POOL_WIDTH),
                  pl.BlockSpec((HALO, POOL_WIDTH),
                               lambda i: (jnp.maximum(i * halo_blocks - 1, 0), 0)),
                  tile(ATTN_WIDTH), _resident((1, D_MODEL)),
                  _resident((D_MODEL, 2 * D_MODEL)), _resident((1, 2 * D_MODEL)),
                  _resident((len(POOL_WINDOWS), POOL_GROUP, POOL_GROUP)),
                  _resident((1, POOL_WIDTH)), _resident((POOL_WIDTH, D_MODEL)),
                  _resident((ATTN_WIDTH, D_MODEL)), _resident((D_MODEL, D_MODEL)),
                  _resident((1, D_MODEL)), _resident((D_MODEL, 2 * D_FF)),
                  _resident((D_FF, D_MODEL)), _resident((1, D_MODEL))],
        out_specs=tile(D_MODEL),
        out_shape=jax.ShapeDtypeStruct((n_tok, D_MODEL), F32),
        scratch_shapes=[pltpu.VMEM((TM, D_MODEL), BF16), pltpu.VMEM((TM, D_FF), BF16),
                        pltpu.VMEM((POOL_PAD + HALO + TM, POOL_WIDTH), F32),
                        pltpu.VMEM((POOL_PAD + HALO + TM, POOL_WIDTH), F32),
                        pltpu.VMEM((POOL_PAD + HALO + TM, POOL_WIDTH), F32),
                        pltpu.VMEM((TM, POOL_WIDTH), BF16), pltpu.VMEM((TM, D_MODEL), BF16),
                        pltpu.VMEM((TM, D_MODEL), F32)],
        compiler_params=cparams,
        name="mix_ffn2",
    )(x1, up, up, attn, row(mix_norm), wgate_b, row(b_gate),
      poolw_b.reshape(len(POOL_WINDOWS), POOL_GROUP, POOL_GROUP), row(pool_scale), wbp_b, wba_b,
      wout_b, row(ffn2_norm), wgu2_b, wd2_b, row(final_norm))
    return out.reshape(b, s, d)
```
